```python
import math
import jax, jax.numpy as jnp
from jax import lax
import numpy as np

D_MODEL = 4096
BATCH = 8
SEQ = 2048
DEPTH = 2

GRID_W = 64
CTX_LEN = 256
Q_BLOCK = 128
ROPE_BASE = 10000.0
EPS = 1e-6

MLA_HEADS = 16
MLA_NOPE = 128
MLA_ROPE = 64
MLA_QK = MLA_NOPE + MLA_ROPE
MLA_V = 128
MLA_Q_LORA = 1024
MLA_KV_LORA = 512

CONV_W = 1024
CONV_K = 3

GQA_HEADS = 8
GQA_KV_HEADS = 2
GQA_HD = 128

N_BRANCH = 3

N_EXPERTS = 32
TOP_K = 4
D_EXPERT = 512
D_SHARED = 512
ROUTE_SCALE = 2.5

IN_SIZES = (MLA_KV_LORA, MLA_ROPE, GQA_KV_HEADS * GQA_HD, GQA_KV_HEADS * GQA_HD,
            MLA_Q_LORA, GQA_HEADS * GQA_HD, CONV_W, CONV_W, CONV_W, N_BRANCH * D_MODEL)
D_IN = (MLA_KV_LORA + MLA_ROPE + 2 * GQA_KV_HEADS * GQA_HD + MLA_Q_LORA + GQA_HEADS * GQA_HD
        + 3 * CONV_W + N_BRANCH * D_MODEL)

kernel_name = 'hybrid_mla_shortconv_gqa_moe_dit'


def split_cols(p, sizes):
    parts, start = [], 0
    for s in sizes:
        parts.append(p[..., start:start + s])
        start += s
    return parts


def rms_norm(x, g):
    xf = x.astype(jnp.float32)
    y = xf * lax.rsqrt(jnp.mean(xf * xf, axis=-1, keepdims=True) + EPS)
    return (y * g.astype(jnp.float32)).astype(x.dtype)


def modulate(h, shift, scale):
    return h * (1.0 + scale[:, None, :]) + shift[:, None, :]


def axial_angles(n, rope_dim):
    rows = n // GRID_W
    row = jnp.repeat(jnp.arange(rows, dtype=jnp.int32), GRID_W).astype(jnp.float32)
    col = jnp.tile(jnp.arange(GRID_W, dtype=jnp.int32), rows).astype(jnp.float32)
    nf = rope_dim // 4
    inv = ROPE_BASE ** (-jnp.arange(nf, dtype=jnp.float32) / nf)
    return jnp.concatenate([row[:, None] * inv, col[:, None] * inv], axis=-1)


def apply_rope(x, ang):
    B, N, H, R = x.shape
    nf = R // 4
    xf = x.astype(jnp.float32).reshape(B, N, H, 2, 2, nf)
    a = ang.reshape(N, 2, nf)[None, :, None]
    cos, sin = jnp.cos(a), jnp.sin(a)
    x1, x2 = xf[..., 0, :], xf[..., 1, :]
    out = jnp.stack([x1 * cos - x2 * sin, x1 * sin + x2 * cos], axis=-2)
    return out.reshape(B, N, H, R).astype(x.dtype)


def attend(q, k, v):
    B, Nq, Hq, dk = q.shape
    Hkv, dv = k.shape[2], v.shape[-1]
    G = Hq // Hkv
    nb = Nq // Q_BLOCK
    scale = 1.0 / math.sqrt(dk)
    qb = q.reshape(B, nb, Q_BLOCK, Hkv, G, dk).transpose(1, 0, 2, 3, 4, 5)

    def one_block(q_blk):
        s = jnp.einsum('bqhgd,bkhd->bhgqk', q_blk, k).astype(jnp.float32) * scale
        p = jax.nn.softmax(s, axis=-1).astype(v.dtype)
        return jnp.einsum('bhgqk,bkhd->bqhgd', p, v)

    o = lax.map(one_block, qb)
    return o.transpose(1, 0, 2, 3, 4, 5).reshape(B, Nq, Hq * dv)


def mla_kv(ckv, k_rope, kv_norm_g, w_ukv, k_head_g, ang):
    B, N, _ = ckv.shape
    kv = (rms_norm(ckv, kv_norm_g) @ w_ukv).reshape(B, N, MLA_HEADS, MLA_NOPE + MLA_V)
    k_nope, v = kv[..., :MLA_NOPE], kv[..., MLA_NOPE:]
    k_r = jnp.broadcast_to(k_rope[:, :, None, :], (B, N, MLA_HEADS, MLA_ROPE))
    k = rms_norm(jnp.concatenate([k_nope, k_r], axis=-1), k_head_g)
    if ang is not None:
        k = jnp.concatenate([k[..., :MLA_NOPE], apply_rope(k[..., MLA_NOPE:], ang)], axis=-1)
    return k, v


def mla_q(cq, q_norm_g, w_uq, q_head_g, ang):
    B, N, _ = cq.shape
    q = (rms_norm(cq, q_norm_g) @ w_uq).reshape(B, N, MLA_HEADS, MLA_QK)
    q = rms_norm(q, q_head_g)
    if ang is not None:
        q = jnp.concatenate([q[..., :MLA_NOPE], apply_rope(q[..., MLA_NOPE:], ang)], axis=-1)
    return q


def gqa_kv(gk, gv, k_g, ang):
    B, N, _ = gk.shape
    k = rms_norm(gk.reshape(B, N, GQA_KV_HEADS, GQA_HD), k_g)
    if ang is not None:
        k = apply_rope(k, ang)
    return k, gv.reshape(B, N, GQA_KV_HEADS, GQA_HD)


def gqa_q(gq, q_g, ang):
    B, N, _ = gq.shape
    q = rms_norm(gq.reshape(B, N, GQA_HEADS, GQA_HD), q_g)
    if ang is not None:
        q = apply_rope(q, ang)
    return q


def short_conv(u, w):
    up = jnp.pad(u, ((0, 0), (1, 1), (0, 0)))
    return up[:, :-2] * w[0] + up[:, 1:-1] * w[1] + up[:, 2:] * w[2]


def conv_branch(b_gate, c_gate, u, w):
    return b_gate * short_conv(c_gate * u, w)


def merge_branches(o_a, o_b, o_c, gates, w_pa, w_pb, w_pc, w_o):
    g_a, g_b, g_c = split_cols(gates, (D_MODEL,) * N_BRANCH)
    h = (jax.nn.sigmoid(g_a) * (o_a @ w_pa) + jax.nn.sigmoid(g_b) * (o_b @ w_pb)
         + jax.nn.sigmoid(g_c) * (o_c @ w_pc))
    return h @ w_o


def swiglu(t, w_gu, w_down):
    gu = t @ w_gu
    f = w_down.shape[0]
    return (jax.nn.silu(gu[..., :f]) * gu[..., f:]) @ w_down


def moe_ffn(h, w_router, b_router, w_exp_gu, w_exp_down, w_sh_gu, w_sh_down):
    B, N, D = h.shape
    t = h.reshape(B * N, D)
    scores = jax.nn.sigmoid((t @ w_router).astype(jnp.float32))
    _, idx = lax.top_k(scores + b_router.astype(jnp.float32), TOP_K)
    sel = jnp.take_along_axis(scores, idx, axis=-1)
    wts = sel / jnp.sum(sel, axis=-1, keepdims=True) * ROUTE_SCALE
    gate = jnp.einsum('tk,tke->te', wts, jax.nn.one_hot(idx, N_EXPERTS, dtype=jnp.float32)).astype(h.dtype)
    y = swiglu(t, w_sh_gu, w_sh_down)
    for e in range(N_EXPERTS):
        y = y + gate[:, e:e + 1] * swiglu(t, w_exp_gu[e], w_exp_down[e])
    return y.reshape(B, N, D)


def setup_inputs(seed: int = 0) -> dict:
    key = jax.random.key(seed)
    ks = iter(jax.random.split(key, 40))
    L, D = DEPTH, D_MODEL

    def nrm(shape, fan_in, gain=1.0):
        return jax.random.normal(next(ks), shape, jnp.float32) * (gain * fan_in ** -0.5)

    def gains(shape):
        return 1.0 + 0.02 * jax.random.normal(next(ks), shape, jnp.float32)

    return {
        'x': jax.random.normal(next(ks), (BATCH, SEQ, D), jnp.float32),
        'c': jax.random.normal(next(ks), (BATCH, D), jnp.float32),
        'ctx': jax.random.normal(next(ks), (BATCH, CTX_LEN, D), jnp.float32),
        'c_ctx': jax.random.normal(next(ks), (D,), jnp.float32),
        'w_mod': nrm((L, D, 6 * D), D, 0.5),
        'b_mod': 0.02 * jax.random.normal(next(ks), (L, 6 * D), jnp.float32),
        'g_mix': gains((L, D)),
        'g_ffn': gains((L, D)),
        'w_in': nrm((L, D, D_IN), D),
        'mla_q_norm_g': gains((L, MLA_Q_LORA)),
        'mla_kv_norm_g': gains((L, MLA_KV_LORA)),
        'w_uq': nrm((L, MLA_Q_LORA, MLA_HEADS * MLA_QK), MLA_Q_LORA),
        'w_ukv': nrm((L, MLA_KV_LORA, MLA_HEADS * (MLA_NOPE + MLA_V)), MLA_KV_LORA),
        'mla_q_head_g': gains((L, MLA_QK)),
        'mla_k_head_g': gains((L, MLA_QK)),
        'gqa_q_g': gains((L, GQA_HD)),
        'gqa_k_g': gains((L, GQA_HD)),
        'conv_w': nrm((L, CONV_K, CONV_W), CONV_K),
        'w_pa': nrm((L, MLA_HEADS * MLA_V, D), MLA_HEADS * MLA_V),
        'w_pb': nrm((L, CONV_W, D), CONV_W),
        'w_pc': nrm((L, GQA_HEADS * GQA_HD, D), GQA_HEADS * GQA_HD),
        'w_o': nrm((L, D, D), D),
        'w_router': nrm((L, D, N_EXPERTS), D),
        'b_router': 0.01 * jax.random.normal(next(ks), (L, N_EXPERTS), jnp.float32),
        'w_exp_gu': nrm((L, N_EXPERTS, D, 2 * D_EXPERT), D),
        'w_exp_down': nrm((L, N_EXPERTS, D_EXPERT, D), D_EXPERT),
        'w_sh_gu': nrm((L, D, 2 * D_SHARED), D),
        'w_sh_down': nrm((L, D_SHARED, D), D_SHARED),
    }


def reference(x, c, ctx, c_ctx, w_mod, b_mod, g_mix, g_ffn, w_in, mla_q_norm_g, mla_kv_norm_g, w_uq, w_ukv,
              mla_q_head_g, mla_k_head_g, gqa_q_g, gqa_k_g, conv_w, w_pa, w_pb, w_pc, w_o, w_router, b_router,
              w_exp_gu, w_exp_down, w_sh_gu, w_sh_down):
    B, N, _ = x.shape
    ang_mla = axial_angles(N, MLA_ROPE)
    ang_gqa = axial_angles(N, GQA_HD)
    xc = ctx
    silu_c = jax.nn.silu(c)
    silu_cc = jax.nn.silu(c_ctx)[None, :]
    for l in range(DEPTH):
        last = l == DEPTH - 1
        mx = split_cols(silu_c @ w_mod[l] + b_mod[l], (D_MODEL,) * 6)
        mc = split_cols(silu_cc @ w_mod[l] + b_mod[l], (D_MODEL,) * 6)

        hx = modulate(rms_norm(x, g_mix[l]), mx[0], mx[1])
        hc = modulate(rms_norm(xc, g_mix[l]), mc[0], mc[1])
        px = split_cols(hx @ w_in[l], IN_SIZES)
        c_sizes = IN_SIZES[:4] if last else IN_SIZES
        pc = split_cols(hc @ w_in[l][:, :sum(c_sizes)], c_sizes)

        ka_c, va_c = mla_kv(pc[0], pc[1], mla_kv_norm_g[l], w_ukv[l], mla_k_head_g[l], None)
        kg_c, vg_c = gqa_kv(pc[2], pc[3], gqa_k_g[l], None)
        ka_x, va_x = mla_kv(px[0], px[1], mla_kv_norm_g[l], w_ukv[l], mla_k_head_g[l], ang_mla)
        kg_x, vg_x = gqa_kv(px[2], px[3], gqa_k_g[l], ang_gqa)
        qa_x = mla_q(px[4], mla_q_norm_g[l], w_uq[l], mla_q_head_g[l], ang_mla)
        qg_x = gqa_q(px[5], gqa_q_g[l], ang_gqa)
        oa_x = attend(qa_x, jnp.concatenate([ka_c, ka_x], axis=1), jnp.concatenate([va_c, va_x], axis=1))
        og_x = attend(qg_x, jnp.concatenate([kg_c, kg_x], axis=1), jnp.concatenate([vg_c, vg_x], axis=1))
        ob_x = conv_branch(px[6], px[7], px[8], conv_w[l])
        mix_x = merge_branches(oa_x, ob_x, og_x, px[9], w_pa[l], w_pb[l], w_pc[l], w_o[l])

        if not last:
            oa_c = attend(mla_q(pc[4], mla_q_norm_g[l], w_uq[l], mla_q_head_g[l], None), ka_c, va_c)
            og_c = attend(gqa_q(pc[5], gqa_q_g[l], None), kg_c, vg_c)
            ob_c = conv_branch(pc[6], pc[7], pc[8], conv_w[l])
            xc = xc + mc[2][:, None, :] * merge_branches(oa_c, ob_c, og_c, pc[9], w_pa[l], w_pb[l], w_pc[l], w_o[l])
            hc2 = modulate(rms_norm(xc, g_ffn[l]), mc[3], mc[4])
            xc = xc + mc[5][:, None, :] * moe_ffn(hc2, w_router[l], b_router[l], w_exp_gu[l], w_exp_down[l],
                                                  w_sh_gu[l], w_sh_down[l])

        x = x + mx[2][:, None, :] * mix_x
        hx2 = modulate(rms_norm(x, g_ffn[l]), mx[3], mx[4])
        x = x + mx[5][:, None, :] * moe_ffn(hx2, w_router[l], b_router[l], w_exp_gu[l], w_exp_down[l],
                                            w_sh_gu[l], w_sh_down[l])
    return x
```

```python
import functools
import math

import jax
import jax.numpy as jnp
from jax import lax
from jax.experimental import pallas as pl
from jax.experimental.pallas import tpu as pltpu

F32 = jnp.float32
BF16 = jnp.bfloat16

GRID_W = 64
ROPE_BASE = 10000.0
EPS = 1e-6
MLA_NOPE = 128
MLA_ROPE = 64
MLA_QK = MLA_NOPE + MLA_ROPE
MLA_V = 128
GQA_HD = 128
TOP_K = 4
ROUTE_SCALE = 2.5
N_MOD = 6

LANE = 128
MLA_QK_PAD = 2 * LANE
MOD_ROWS = 16
VMEM_LIMIT_BYTES = 56 * 1024 * 1024


def _tile(dim, pref, unit, *also):
    t = min(pref, dim) // unit * unit
    while t > unit:
        if dim % t == 0 and all(a % t == 0 for a in also):
            return t
        t -= unit
    assert dim % unit == 0 and all(a % unit == 0 for a in also), (dim, unit, also)
    return unit


def _call(kernel, grid, in_specs, out_specs, out_shape, name, scratch=()):
    return pl.pallas_call(
        kernel, grid=grid, in_specs=in_specs, out_specs=out_specs, out_shape=out_shape,
        scratch_shapes=list(scratch), name=name,
        compiler_params=pltpu.CompilerParams(
            dimension_semantics=("arbitrary",) * len(grid), vmem_limit_bytes=VMEM_LIMIT_BYTES))


def _silu(v):
    return v * jax.nn.sigmoid(v)


def _mod_kernel(c_ref, w_ref, b_ref, o_ref):
    s = _silu(c_ref[...]).astype(BF16)
    o_ref[0] = jnp.dot(s, w_ref[0].astype(BF16), preferred_element_type=F32) + b_ref[0]


def _mods(cvec, w_mod, b_mod):
    L, D, N6 = w_mod.shape
    tn = _tile(N6, 512, LANE)
    return _call(
        _mod_kernel, (L, N6 // tn),
        [pl.BlockSpec((MOD_ROWS, D), lambda l, j: (0, 0)),
         pl.BlockSpec((1, D, tn), lambda l, j: (l, 0, j)),
         pl.BlockSpec((1, 1, tn), lambda l, j: (l, 0, j))],
        pl.BlockSpec((1, MOD_ROWS, tn), lambda l, j: (l, 0, j)),
        jax.ShapeDtypeStruct((L, MOD_ROWS, N6), F32), "adaln_mod")(cvec, w_mod, b_mod.reshape(L, 1, N6))


class _Tok:
    def __init__(self, B, N, CTX):
        self.B, self.N, self.CTX = B, N, CTX
        self.n_lat = B * N
        self.n_ctx = B * CTX

    def mod_row(self, tok0):
        return jnp.where(tok0 < self.n_lat, tok0 // self.N, self.B)

    def pos_block(self, tok0, tm):
        return jnp.where(tok0 < self.n_lat, (tok0 % self.N) // tm, self.N // tm)


def _mod_spec(tok, l, which, tm, tok_off, D, ngrid):
    def idx(i, *_):
        row = tok.mod_row(i * tm + tok_off)
        return ((l * MOD_ROWS + row) * N_MOD + which, 0, 0)
    return pl.BlockSpec((None, 1, D), idx)


def _norm_mod_kernel(x_ref, g_ref, sh_ref, sc_ref, o_ref):
    x = x_ref[...]
    r = lax.rsqrt(jnp.mean(x * x, axis=-1, keepdims=True) + EPS)
    o_ref[...] = ((x * r * g_ref[...]) * (1.0 + sc_ref[...]) + sh_ref[...]).astype(o_ref.dtype)


def _norm_mod(X, row_off, n_rows, g, mods3, tok, l, which_shift, tok_off):
    D = X.shape[1]
    tm = _tile(n_rows, 256, 16, tok.N, row_off) if row_off else _tile(n_rows, 256, 16, tok.N)
    ro = row_off // tm
    return _call(
        _norm_mod_kernel, (n_rows // tm,),
        [pl.BlockSpec((tm, D), lambda i: (i + ro, 0)),
         pl.BlockSpec((1, D), lambda i: (0, 0)),
         _mod_spec(tok, l, which_shift, tm, tok_off, D, 1),
         _mod_spec(tok, l, which_shift + 1, tm, tok_off, D, 1)],
        pl.BlockSpec((tm, D), lambda i: (i, 0)),
        jax.ShapeDtypeStruct((n_rows, D), BF16), "norm_mod")(X, g.reshape(1, D), mods3, mods3)


def _norm_router_kernel(x_ref, g_ref, sh_ref, sc_ref, wr_ref, br_ref, h_ref, gate_ref):
    x = x_ref[...]
    r = lax.rsqrt(jnp.mean(x * x, axis=-1, keepdims=True) + EPS)
    h = (x * r * g_ref[...]) * (1.0 + sc_ref[...]) + sh_ref[...]
    h_ref[...] = h.astype(h_ref.dtype)
    logits = jnp.dot(h, wr_ref[...], preferred_element_type=F32, precision=lax.Precision.HIGHEST)
    s = jax.nn.sigmoid(logits)
    sel = s + br_ref[...]
    n_e = sel.shape[-1]
    lane = lax.broadcasted_iota(jnp.int32, sel.shape, 1).astype(F32)
    picked = jnp.zeros(sel.shape, F32)
    for _ in range(TOP_K):
        cur = jnp.where(picked > 0.0, -jnp.inf, sel)
        m = jnp.max(cur, axis=-1, keepdims=True)
        first = jnp.min(jnp.where(cur == m, lane, float(n_e)), axis=-1, keepdims=True)
        picked = jnp.where(lane == first, 1.0, picked)
    w = picked * s
    gate_ref[...] = w / jnp.sum(w, axis=-1, keepdims=True) * ROUTE_SCALE


def _norm_router(X, n_rows, g, mods3, tok, l, w_router, b_router):
    D = X.shape[1]
    E = w_router.shape[1]
    tm = _tile(n_rows, 256, 16, tok.N)
    return _call(
        _norm_router_kernel, (n_rows // tm,),
        [pl.BlockSpec((tm, D), lambda i: (i, 0)),
         pl.BlockSpec((1, D), lambda i: (0, 0)),
         _mod_spec(tok, l, 3, tm, 0, D, 1),
         _mod_spec(tok, l, 4, tm, 0, D, 1),
         pl.BlockSpec((D, E), lambda i: (0, 0)),
         pl.BlockSpec((1, E), lambda i: (0, 0))],
        [pl.BlockSpec((tm, D), lambda i: (i, 0)), pl.BlockSpec((tm, E), lambda i: (i, 0))],
        [jax.ShapeDtypeStruct((n_rows, D), BF16), jax.ShapeDtypeStruct((n_rows, E), F32)],
        "norm_router")(X, g.reshape(1, D), mods3, mods3, w_router, b_router.reshape(1, E))


def _mm_kernel(a_ref, b_ref, o_ref):
    o_ref[...] = jnp.dot(a_ref[...], b_ref[...], preferred_element_type=F32).astype(o_ref.dtype)


def _mm(a, b, tm_pref=1024, tn_pref=1024, name="mm"):
    M, K = a.shape
    N = b.shape[1]
    tm = _tile(M, tm_pref, 16)
    tn = _tile(N, tn_pref, LANE)
    return _call(
        _mm_kernel, (M // tm, N // tn),
        [pl.BlockSpec((tm, K), lambda i, j: (i, 0)), pl.BlockSpec((K, tn), lambda i, j: (0, j))],
        pl.BlockSpec((tm, tn), lambda i, j: (i, j)),
        jax.ShapeDtypeStruct((M, N), BF16), name)(a, b)


def _norm_mm_kernel(a_ref, g_ref, b_ref, o_ref):
    a = a_ref[...].astype(F32)
    r = lax.rsqrt(jnp.mean(a * a, axis=-1, keepdims=True) + EPS)
    an = (a * r * g_ref[...]).astype(BF16)
    o_ref[...] = jnp.dot(an, b_ref[...], preferred_element_type=F32).astype(o_ref.dtype)


def _norm_mm(a, col_off, K, g, b, name):
    M = a.shape[0]
    N = b.shape[1]
    tm = _tile(M, 512, 16)
    cb = col_off // K
    assert col_off % K == 0
    return _call(
        _norm_mm_kernel, (M // tm,),
        [pl.BlockSpec((tm, K), lambda i: (i, cb)),
         pl.BlockSpec((1, K), lambda i: (0, 0)),
         pl.BlockSpec((K, N), lambda i: (0, 0))],
        pl.BlockSpec((tm, N), lambda i: (i, 0)),
        jax.ShapeDtypeStruct((M, N), BF16), name)(a, g.reshape(1, K), b)


def _headprep_kernel(*refs, n_chunks, rope_chunk, inv_d, shift):
    x_refs = refs[:n_chunks]
    g_ref, cos_ref, sin_ref, o_ref = refs[n_chunks:]
    xs = [r[...].astype(F32) for r in x_refs]
    ssq = sum(jnp.sum(x * x, axis=-1, keepdims=True) for x in xs)
    r = lax.rsqrt(ssq * inv_d + EPS)
    for c, x in enumerate(xs):
        y = x * r * g_ref[:, c * LANE:(c + 1) * LANE]
        if c == rope_chunk:
            lane = lax.broadcasted_iota(jnp.int32, y.shape, 1)
            first_half = (lane % (2 * shift)) < shift
            swapped = jnp.where(first_half, pltpu.roll(y, LANE - shift, 1), pltpu.roll(y, shift, 1))
            y = y * cos_ref[...] + swapped * sin_ref[...]
        o_ref[:, c * LANE:(c + 1) * LANE] = y.astype(o_ref.dtype)


def _headprep(chunks, n_rows, n_heads, g, cos, sin, rope_chunk, norm_dim, shift, tok, tok_off, name):
    n_chunks = len(chunks)
    W = n_chunks * LANE
    tm = _tile(n_rows, 512, 16, tok.N)
    assert cos.shape[0] >= tok.N + tm

    def pos_idx(i, h):
        return (tok.pos_block(i * tm + tok_off, tm), 0)

    in_specs = [pl.BlockSpec((tm, LANE), (lambda i, h, f=f: (i, f(h)))) for _, f in chunks]
    in_specs += [pl.BlockSpec((1, W), lambda i, h: (0, 0)),
                 pl.BlockSpec((tm, LANE), pos_idx), pl.BlockSpec((tm, LANE), pos_idx)]
    kern = functools.partial(_headprep_kernel, n_chunks=n_chunks, rope_chunk=rope_chunk,
                             inv_d=1.0 / norm_dim, shift=shift)
    return _call(
        kern, (n_rows // tm, n_heads), in_specs,
        pl.BlockSpec((tm, W), lambda i, h: (i, h)),
        jax.ShapeDtypeStruct((n_rows, n_heads * W), BF16), name)(*[a for a, _ in chunks], g.reshape(1, W), cos, sin)


def _rope_tables(n_pos, rope_dim, pad_rows):
    nf = rope_dim // 4
    t = jnp.arange(n_pos, dtype=jnp.int32)
    row = (t // GRID_W).astype(F32)
    col = (t % GRID_W).astype(F32)
    inv = ROPE_BASE ** (-jnp.arange(nf, dtype=F32) / nf)
    a_row, a_col = row[:, None] * inv, col[:, None] * inv
    cos = jnp.concatenate([jnp.cos(a_row)] * 2 + [jnp.cos(a_col)] * 2, axis=-1)
    sin = jnp.concatenate([-jnp.sin(a_row), jnp.sin(a_row), -jnp.sin(a_col), jnp.sin(a_col)], axis=-1)
    cos = jnp.pad(cos, ((0, pad_rows), (0, LANE - rope_dim)), constant_values=1.0)
    sin = jnp.pad(sin, ((0, pad_rows), (0, LANE - rope_dim)))
    return cos, sin


def _attn_kernel(*refs, n_src, tq, n_q):
    q_ref = refs[0]
    k_refs = refs[1:1 + n_src]
    v_refs = refs[1 + n_src:1 + 2 * n_src]
    o_ref = refs[-1]

    def body(c, carry):
        r0 = pl.multiple_of(c * tq, tq)
        q = q_ref[pl.ds(r0, tq), :]
        s = [lax.dot_general(q, k[...], (((1,), (1,)), ((), ())), preferred_element_type=F32) for k in k_refs]
        m = functools.reduce(jnp.maximum, [jnp.max(si, axis=-1, keepdims=True) for si in s])
        p = [jnp.exp(si - m) for si in s]
        l = sum(jnp.sum(pi, axis=-1, keepdims=True) for pi in p)
        o = sum(jnp.dot(pi.astype(BF16), v[...], preferred_element_type=F32) for pi, v in zip(p, v_refs))
        o_ref[pl.ds(r0, tq), :] = (o * (1.0 / l)).astype(o_ref.dtype)
        return carry

    lax.fori_loop(0, n_q // tq, body, 0)


def _attend(q, q_row_off, n_q, k_srcs, v_srcs, B, n_kv_heads, group, dk, dv, name):
    n_src = len(k_srcs)
    tq = _tile(n_q, 512, 16)
    qb = q_row_off // n_q
    assert q_row_off % n_q == 0
    in_specs = [pl.BlockSpec((n_q, dk), lambda b, hk, g: (qb + b, hk * group + g))]
    args = [q]
    for width, srcs in ((dk, k_srcs), (dv, v_srcs)):
        for arr, row_off, n_k, cb0 in srcs:
            assert row_off % n_k == 0
            rb = row_off // n_k
            in_specs.append(pl.BlockSpec((n_k, width), (lambda b, hk, g, rb=rb, cb0=cb0: (rb + b, cb0 + hk))))
            args.append(arr)
    kern = functools.partial(_attn_kernel, n_src=n_src, tq=tq, n_q=n_q)
    return _call(
        kern, (B, n_kv_heads, group), in_specs,
        pl.BlockSpec((n_q, dv), lambda b, hk, g: (b, hk * group + g)),
        jax.ShapeDtypeStruct((B * n_q, n_kv_heads * group * dv), BF16), name)(*args)


def _conv_kernel(b_ref, c_ref, u_ref, w_ref, o_ref):
    v = c_ref[...].astype(F32) * u_ref[...].astype(F32)
    n = v.shape[0]
    row = lax.broadcasted_iota(jnp.int32, v.shape, 0)
    prev = jnp.where(row == 0, 0.0, pltpu.roll(v, 1, 0))
    nxt = jnp.where(row == n - 1, 0.0, pltpu.roll(v, n - 1, 0))
    y = prev * w_ref[0:1, :] + v * w_ref[1:2, :] + nxt * w_ref[2:3, :]
    o_ref[...] = (b_ref[...].astype(F32) * y).astype(o_ref.dtype)


def _conv(px, row_off, n_seq, seq_len, col_b, col_c, col_u, CW, conv_w):
    tc = _tile(CW, 512, LANE, col_b, col_c, col_u)
    rb = row_off // seq_len
    assert row_off % seq_len == 0

    def spec(col):
        return pl.BlockSpec((seq_len, tc), lambda s, j, col=col: (rb + s, col // tc + j))

    return _call(
        _conv_kernel, (n_seq, CW // tc),
        [spec(col_b), spec(col_c), spec(col_u), pl.BlockSpec((3, tc), lambda s, j: (0, j))],
        pl.BlockSpec((seq_len, tc), lambda s, j: (s, j)),
        jax.ShapeDtypeStruct((n_seq * seq_len, CW), BF16), "short_conv")(px, px, px, conv_w)


def _merge_kernel(oa_ref, ob_ref, oc_ref, wa_ref, wb_ref, wc_ref, ga_ref, gb_ref, gc_ref, o_ref):
    def term(o, w, g):
        return jax.nn.sigmoid(g[...].astype(F32)) * jnp.dot(o[...], w[...], preferred_element_type=F32)
    h = term(oa_ref, wa_ref, ga_ref) + term(ob_ref, wb_ref, gb_ref) + term(oc_ref, wc_ref, gc_ref)
    o_ref[...] = h.astype(o_ref.dtype)


def _merge(oa, ob, oc, w_pa, w_pb, w_pc, px, gate_col, n_rows):
    D = w_pa.shape[1]
    tm = _tile(n_rows, 1024, 16)
    tn = _tile(D, 512, LANE, gate_col)
    gb0 = gate_col // tn
    nb = D // tn

    def a_spec(o):
        return pl.BlockSpec((tm, o.shape[1]), lambda i, j: (i, 0))

    def w_spec(w):
        return pl.BlockSpec((w.shape[0], tn), lambda i, j: (0, j))

    def g_spec(k):
        return pl.BlockSpec((tm, tn), lambda i, j, k=k: (i, gb0 + k * nb + j))

    return _call(
        _merge_kernel, (n_rows // tm, nb),
        [a_spec(oa), a_spec(ob), a_spec(oc), w_spec(w_pa), w_spec(w_pb), w_spec(w_pc),
         g_spec(0), g_spec(1), g_spec(2)],
        pl.BlockSpec((tm, tn), lambda i, j: (i, j)),
        jax.ShapeDtypeStruct((n_rows, D), BF16), "merge")(oa, ob, oc, w_pa, w_pb, w_pc, px, px, px)


def _mm_res_kernel(a_ref, b_ref, x_ref, gt_ref, o_ref):
    acc = jnp.dot(a_ref[...], b_ref[...], preferred_element_type=F32)
    o_ref[...] = x_ref[...] + gt_ref[...] * acc


def _mm_res(a, b, X, mods3, tok, l, which, n_rows):
    K = a.shape[1]
    D = b.shape[1]
    tm = _tile(n_rows, 1024, 16, tok.N)
    tn = _tile(D, 512, LANE)

    def gt_idx(i, j):
        row = tok.mod_row(i * tm)
        return ((l * MOD_ROWS + row) * N_MOD + which, 0, j)

    return _call(
        _mm_res_kernel, (n_rows // tm, D // tn),
        [pl.BlockSpec((tm, K), lambda i, j: (i, 0)), pl.BlockSpec((K, tn), lambda i, j: (0, j)),
         pl.BlockSpec((tm, tn), lambda i, j: (i, j)), pl.BlockSpec((None, 1, tn), gt_idx)],
        pl.BlockSpec((tm, tn), lambda i, j: (i, j)),
        jax.ShapeDtypeStruct((n_rows, D), F32), "proj_residual")(a, b, X, mods3)


def _expert_up_kernel(h_ref, w_ref, gate_ref, o_ref, *, n_experts, f):
    e = pl.program_id(1)
    gu = jnp.dot(h_ref[...], w_ref[...], preferred_element_type=F32)
    gate = gate_ref[...]
    lane = lax.broadcasted_iota(jnp.int32, gate.shape, 1)
    ge = jnp.sum(jnp.where(lane == e, gate, 0.0), axis=-1, keepdims=True)
    ge = jnp.where(e == n_experts, 1.0, ge)
    o_ref[...] = (_silu(gu[:, :f]) * gu[:, f:] * ge).astype(o_ref.dtype)


def _expert_up(h, w_gu_all, gate, f):
    M, D = h.shape
    n_all = w_gu_all.shape[0]
    E = gate.shape[1]
    tm = _tile(M, 1024, 16)
    kern = functools.partial(_expert_up_kernel, n_experts=E, f=f)
    return _call(
        kern, (M // tm, n_all),
        [pl.BlockSpec((tm, D), lambda i, e: (i, 0)),
         pl.BlockSpec((None, D, 2 * f), lambda i, e: (e, 0, 0)),
         pl.BlockSpec((tm, E), lambda i, e: (i, 0))],
        pl.BlockSpec((tm, f), lambda i, e: (i, e)),
        jax.ShapeDtypeStruct((M, n_all * f), BF16), "expert_up")(h, w_gu_all, gate)


def _mm_acc_res_kernel(a_ref, b_ref, x_ref, gt_ref, o_ref, acc_ref):
    k = pl.program_id(2)

    @pl.when(k == 0)
    def _():
        acc_ref[...] = jnp.zeros_like(acc_ref)

    acc_ref[...] += jnp.dot(a_ref[...], b_ref[...], preferred_element_type=F32)

    @pl.when(k == pl.num_programs(2) - 1)
    def _():
        o_ref[...] = x_ref[...] + gt_ref[...] * acc_ref[...]


def _mm_acc_res(a, b, X, mods3, tok, l, which, n_rows):
    K = a.shape[1]
    D = b.shape[1]
    tm = _tile(n_rows, 1024, 16, tok.N)
    tn = _tile(D, 1024, LANE)
    tk = _tile(K, 1536, LANE)

    def gt_idx(i, j, k):
        row = tok.mod_row(i * tm)
        return ((l * MOD_ROWS + row) * N_MOD + which, 0, j)

    return _call(
        _mm_acc_res_kernel, (n_rows // tm, D // tn, K // tk),
        [pl.BlockSpec((tm, tk), lambda i, j, k: (i, k)), pl.BlockSpec((tk, tn), lambda i, j, k: (k, j)),
         pl.BlockSpec((tm, tn), lambda i, j, k: (i, j)), pl.BlockSpec((None, 1, tn), gt_idx)],
        pl.BlockSpec((tm, tn), lambda i, j, k: (i, j)),
        jax.ShapeDtypeStruct((n_rows, D), F32), "expert_down_residual",
        scratch=[pltpu.VMEM((tm, tn), F32)])(a, b, X, mods3)


def kernel(x, c, ctx, c_ctx, w_mod, b_mod, g_mix, g_ffn, w_in, mla_q_norm_g, mla_kv_norm_g, w_uq, w_ukv,
           mla_q_head_g, mla_k_head_g, gqa_q_g, gqa_k_g, conv_w, w_pa, w_pb, w_pc, w_o, w_router, b_router,
           w_exp_gu, w_exp_down, w_sh_gu, w_sh_down):
    B, N, D = x.shape
    CTX = ctx.shape[1]
    L = w_mod.shape[0]
    QL = mla_q_norm_g.shape[1]
    KVL = mla_kv_norm_g.shape[1]
    HA = w_uq.shape[2] // MLA_QK
    HG = w_pc.shape[1] // GQA_HD
    CW = conv_w.shape[2]
    D_IN = w_in.shape[2]
    HKV = (D_IN - (KVL + MLA_ROPE + QL + HG * GQA_HD + 3 * CW + 3 * D)) // (2 * GQA_HD)
    GKV = HKV * GQA_HD
    E = w_router.shape[2]
    F = w_exp_down.shape[2]
    assert w_sh_down.shape[1] == F, "shared expert is evaluated as one more expert of the same width"

    tok = _Tok(B, N, CTX)
    n_lat, n_ctx = tok.n_lat, tok.n_ctx

    KV_COLS = KVL + 2 * GKV + LANE
    col_gk, col_gv, col_kr = KVL, KVL + GKV, KVL + 2 * GKV
    main0 = KVL + MLA_ROPE + 2 * GKV
    col_cq, col_gq = 0, QL
    col_b, col_c, col_u = QL + HG * GQA_HD, QL + HG * GQA_HD + CW, QL + HG * GQA_HD + 2 * CW
    col_gate = col_u + CW

    cvec = jnp.zeros((MOD_ROWS, D), F32).at[:B].set(c).at[B].set(c_ctx)
    mods = _mods(cvec, w_mod, b_mod)
    mods3 = mods.reshape(L * MOD_ROWS * N_MOD, 1, D)

    pad_rows = _tile(n_lat + n_ctx, 512, 16, N)
    cos_a, sin_a = _rope_tables(N, MLA_ROPE, pad_rows)
    cos_g, sin_g = _rope_tables(N, GQA_HD, pad_rows)

    X = jnp.concatenate([x.reshape(n_lat, D), ctx.reshape(n_ctx, D)], axis=0)

    for l in range(L):
        last = l == L - 1
        n_rows = n_lat if last else n_lat + n_ctx

        wl = w_in[l]
        w_main = wl[:, main0:].astype(BF16)
        w_kv = jnp.concatenate(
            [wl[:, :KVL], wl[:, KVL + MLA_ROPE:main0], wl[:, KVL:KVL + MLA_ROPE],
             jnp.zeros((D, LANE - MLA_ROPE), F32)], axis=1).astype(BF16)
        w_uq_p = jnp.pad(w_uq[l].reshape(QL, HA, MLA_QK),
                         ((0, 0), (0, 0), (0, MLA_QK_PAD - MLA_QK))).reshape(QL, HA * MLA_QK_PAD).astype(BF16)
        w_ukv_r = w_ukv[l].reshape(KVL, HA, MLA_NOPE + MLA_V)
        w_ukv_p = jnp.concatenate([w_ukv_r[:, :, :MLA_NOPE].reshape(KVL, HA * MLA_NOPE),
                                   w_ukv_r[:, :, MLA_NOPE:].reshape(KVL, HA * MLA_V)], axis=1).astype(BF16)
        g_qa = jnp.pad(mla_q_head_g[l] * (1.0 / math.sqrt(MLA_QK)), (0, MLA_QK_PAD - MLA_QK))
        g_ka = jnp.pad(mla_k_head_g[l], (0, MLA_QK_PAD - MLA_QK))
        g_qg = gqa_q_g[l] * (1.0 / math.sqrt(GQA_HD))
        g_kg = gqa_k_g[l]
        w_gu_all = jnp.concatenate([w_exp_gu[l], w_sh_gu[l][None]], axis=0).astype(BF16)
        w_down_all = jnp.concatenate([w_exp_down[l].reshape(E * F, D), w_sh_down[l]], axis=0).astype(BF16)

        h = _norm_mod(X, 0, n_rows, g_mix[l], mods3, tok, l, 0, 0)
        px = _mm(h, w_main, name="in_proj")
        if last:
            hc = _norm_mod(X, n_lat, n_ctx, g_mix[l], mods3, tok, l, 0, n_lat)
            pkv_x = _mm(h, w_kv, tn_pref=KV_COLS, name="in_proj_kv")
            pkv_c = _mm(hc, w_kv, tn_pref=KV_COLS, name="in_proj_kv")
            kv_parts = [(pkv_x, n_lat, 0), (pkv_c, n_ctx, n_lat)]
        else:
            pkv = _mm(h, w_kv, tn_pref=KV_COLS, name="in_proj_kv")
            kv_parts = [(pkv, n_rows, 0)]

        ka, kvr, kg, pk = [], [], [], []
        for pkv_p, rows_p, tok_off in kv_parts:
            kv_raw = _norm_mm(pkv_p, 0, KVL, mla_kv_norm_g[l], w_ukv_p, "mla_kv_up")
            ka.append(_headprep(
                [(kv_raw, lambda hh: hh), (pkv_p, lambda hh: col_kr // LANE)], rows_p, HA, g_ka, cos_a, sin_a,
                1, MLA_QK, MLA_ROPE // 4, tok, tok_off, "mla_k_prep"))
            kg.append(_headprep(
                [(pkv_p, lambda hh: col_gk // LANE + hh)], rows_p, HKV, g_kg, cos_g, sin_g,
                0, GQA_HD, GQA_HD // 4, tok, tok_off, "gqa_k_prep"))
            kvr.append(kv_raw)
            pk.append(pkv_p)

        q_raw = _norm_mm(px, col_cq, QL, mla_q_norm_g[l], w_uq_p, "mla_q_up")
        qa = _headprep([(q_raw, lambda hh: 2 * hh), (q_raw, lambda hh: 2 * hh + 1)], n_rows, HA, g_qa,
                       cos_a, sin_a, 1, MLA_QK, MLA_ROPE // 4, tok, 0, "mla_q_prep")
        qg = _headprep([(px, lambda hh: col_gq // LANE + hh)], n_rows, HG, g_qg, cos_g, sin_g,
                       0, GQA_HD, GQA_HD // 4, tok, 0, "gqa_q_prep")

        ctx_i = len(kv_parts) - 1
        ctx_off = 0 if last else n_lat
        vA0 = HA * MLA_NOPE // MLA_V
        ka_x, ka_c = (ka[0], 0, N, 0), (ka[ctx_i], ctx_off, CTX, 0)
        va_x, va_c = (kvr[0], 0, N, vA0), (kvr[ctx_i], ctx_off, CTX, vA0)
        kg_x, kg_c = (kg[0], 0, N, 0), (kg[ctx_i], ctx_off, CTX, 0)
        vg_x, vg_c = (pk[0], 0, N, col_gv // GQA_HD), (pk[ctx_i], ctx_off, CTX, col_gv // GQA_HD)

        oa = _attend(qa, 0, N, [ka_c, ka_x], [va_c, va_x], B, HA, 1, MLA_QK_PAD, MLA_V, "mla_attn")
        og = _attend(qg, 0, N, [kg_c, kg_x], [vg_c, vg_x], B, HKV, HG // HKV, GQA_HD, GQA_HD, "gqa_attn")
        ob = _conv(px, 0, B, N, col_b, col_c, col_u, CW, conv_w[l])
        if not last:
            oa_c = _attend(qa, n_lat, CTX, [ka_c], [va_c], B, HA, 1, MLA_QK_PAD, MLA_V, "mla_attn_ctx")
            og_c = _attend(qg, n_lat, CTX, [kg_c], [vg_c], B, HKV, HG // HKV, GQA_HD, GQA_HD, "gqa_attn_ctx")
            ob_c = _conv(px, n_lat, B, CTX, col_b, col_c, col_u, CW, conv_w[l])
            oa = jnp.concatenate([oa, oa_c], axis=0)
            og = jnp.concatenate([og, og_c], axis=0)
            ob = jnp.concatenate([ob, ob_c], axis=0)

        hm = _merge(oa, ob, og, w_pa[l].astype(BF16), w_pb[l].astype(BF16), w_pc[l].astype(BF16),
                    px, col_gate, n_rows)
        X = _mm_res(hm, w_o[l].astype(BF16), X, mods3, tok, l, 2, n_rows)

        h2, gate = _norm_router(X, n_rows, g_ffn[l], mods3, tok, l, w_router[l], b_router[l])
        act = _expert_up(h2, w_gu_all, gate, F)
        X = _mm_acc_res(act, w_down_all, X, mods3, tok, l, 5, n_rows)

    return X.reshape(B, N, D)
```

```python
import functools
import math

import jax
import jax.numpy as jnp
from jax import lax
from jax.experimental import pallas as pl
from jax.experimental.pallas import tpu as pltpu

F32 = jnp.float32
BF16 = jnp.bfloat16

GRID_W = 64
ROPE_BASE = 10000.0
EPS = 1e-6
MLA_NOPE = 128
MLA_ROPE = 64
MLA_QK = MLA_NOPE + MLA_ROPE
MLA_V = 128
GQA_HD = 128
TOP_K = 4
ROUTE_SCALE = 2.5
N_MOD = 6

LANE = 128
MLA_QK_PAD = 2 * LANE
MOD_ROWS = 16
VMEM_LIMIT_BYTES = 56 * 1024 * 1024


def _tile(dim, pref, unit, *also):
    t = min(pref, dim) // unit * unit
    while t > unit:
        if dim % t == 0 and all(a % t == 0 for a in also):
            return t
        t -= unit
    assert dim % unit == 0 and all(a % unit == 0 for a in also), (dim, unit, also)
    return unit


def _call(kernel, grid, in_specs, out_specs, out_shape, name, scratch=()):
    return pl.pallas_call(
        kernel, grid=grid, in_specs=in_specs, out_specs=out_specs, out_shape=out_shape,
        scratch_shapes=list(scratch), name=name,
        compiler_params=pltpu.CompilerParams(
            dimension_semantics=("arbitrary",) * len(grid), vmem_limit_bytes=VMEM_LIMIT_BYTES))


def _silu(v):
    return v * jax.nn.sigmoid(v)


def _mod_kernel(c_ref, w_ref, b_ref, o_ref):
    s = _silu(c_ref[...]).astype(BF16)
    o_ref[0] = jnp.dot(s, w_ref[0].astype(BF16), preferred_element_type=F32) + b_ref[0]


def _mods(cvec, w_mod, b_mod):
    L, D, N6 = w_mod.shape
    tn = _tile(N6, 512, LANE)
    return _call(
        _mod_kernel, (L, N6 // tn),
        [pl.BlockSpec((MOD_ROWS, D), lambda l, j: (0, 0)),
         pl.BlockSpec((1, D, tn), lambda l, j: (l, 0, j)),
         pl.BlockSpec((1, 1, tn), lambda l, j: (l, 0, j))],
        pl.BlockSpec((1, MOD_ROWS, tn), lambda l, j: (l, 0, j)),
        jax.ShapeDtypeStruct((L, MOD_ROWS, N6), F32), "adaln_mod")(cvec, w_mod, b_mod.reshape(L, 1, N6))


class _Tok:
    def __init__(self, B, N, CTX):
        self.B, self.N, self.CTX = B, N, CTX
        self.n_lat = B * N
        self.n_ctx = B * CTX

    def mod_row(self, tok0):
        return jnp.where(tok0 < self.n_lat, tok0 // self.N, self.B)

    def pos_block(self, tok0, tm):
        return jnp.where(tok0 < self.n_lat, (tok0 % self.N) // tm, self.N // tm)


def _mod_spec(tok, l, which, tm, tok_off, D, ngrid):
    def idx(i, *_):
        row = tok.mod_row(i * tm + tok_off)
        return ((l * MOD_ROWS + row) * N_MOD + which, 0, 0)
    return pl.BlockSpec((None, 1, D), idx)


def _norm_mod_kernel(x_ref, g_ref, sh_ref, sc_ref, o_ref):
    x = x_ref[...]
    r = lax.rsqrt(jnp.mean(x * x, axis=-1, keepdims=True) + EPS)
    o_ref[...] = ((x * r * g_ref[...]) * (1.0 + sc_ref[...]) + sh_ref[...]).astype(o_ref.dtype)


def _norm_mod(X, row_off, n_rows, g, mods3, tok, l, which_shift, tok_off):
    D = X.shape[1]
    tm = _tile(n_rows, 256, 16, tok.N, row_off) if row_off else _tile(n_rows, 256, 16, tok.N)
    ro = row_off // tm
    return _call(
        _norm_mod_kernel, (n_rows // tm,),
        [pl.BlockSpec((tm, D), lambda i: (i + ro, 0)),
         pl.BlockSpec((1, D), lambda i: (0, 0)),
         _mod_spec(tok, l, which_shift, tm, tok_off, D, 1),
         _mod_spec(tok, l, which_shift + 1, tm, tok_off, D, 1)],
        pl.BlockSpec((tm, D), lambda i: (i, 0)),
        jax.ShapeDtypeStruct((n_rows, D), BF16), "norm_mod")(X, g.reshape(1, D), mods3, mods3)


def _norm_router_kernel(x_ref, g_ref, sh_ref, sc_ref, wr_ref, br_ref,
                        h_ref, idx_ref, wt_ref, rank_ref, cnt_ref, carry_ref):
    @pl.when(pl.program_id(0) == 0)
    def _():
        carry_ref[...] = jnp.zeros_like(carry_ref)

    x = x_ref[...]
    r = lax.rsqrt(jnp.mean(x * x, axis=-1, keepdims=True) + EPS)
    h = (x * r * g_ref[...]) * (1.0 + sc_ref[...]) + sh_ref[...]
    h_ref[...] = h.astype(h_ref.dtype)
    logits = jnp.dot(h, wr_ref[...], preferred_element_type=F32, precision=lax.Precision.HIGHEST)
    s = jax.nn.sigmoid(logits)
    sel = s + br_ref[...]
    tm, n_e = sel.shape
    lane = lax.broadcasted_iota(jnp.int32, sel.shape, 1).astype(F32)
    picked = jnp.zeros(sel.shape, F32)
    firsts = []
    for _ in range(TOP_K):
        cur = jnp.where(picked > 0.0, -jnp.inf, sel)
        m = jnp.max(cur, axis=-1, keepdims=True)
        first = jnp.min(jnp.where(cur == m, lane, float(n_e)), axis=-1, keepdims=True)
        picked = jnp.where(lane == first, 1.0, picked)
        firsts.append(first)
    w = picked * s
    gate = w / jnp.sum(w, axis=-1, keepdims=True) * ROUTE_SCALE

    earlier = (lax.broadcasted_iota(jnp.int32, (tm, tm), 1) < lax.broadcasted_iota(jnp.int32, (tm, tm), 0))
    cum = jnp.dot(earlier.astype(BF16), picked.astype(BF16), preferred_element_type=F32) + carry_ref[...]
    idx_m = jnp.zeros(sel.shape, F32)
    wt_m = jnp.zeros(sel.shape, F32)
    rank_m = jnp.zeros(sel.shape, F32)
    for k, first in enumerate(firsts):
        mine = lane == first
        idx_m = jnp.where(lane == float(k), first, idx_m)
        wt_m = jnp.where(lane == float(k), jnp.sum(jnp.where(mine, gate, 0.0), axis=-1, keepdims=True), wt_m)
        rank_m = jnp.where(lane == float(k), jnp.sum(jnp.where(mine, cum, 0.0), axis=-1, keepdims=True), rank_m)
    idx_ref[...] = idx_m.astype(jnp.int32)
    wt_ref[...] = wt_m
    rank_ref[...] = rank_m.astype(jnp.int32)
    carry_ref[...] += jnp.sum(picked, axis=0, keepdims=True)
    cnt_ref[...] = carry_ref[...].astype(jnp.int32)


def _norm_router(X, n_rows, g, mods3, tok, l, w_router, b_router):
    D = X.shape[1]
    E = w_router.shape[1]
    tm = _tile(n_rows, 256, 16, tok.N)
    row = lambda i: (i, 0)
    fixed = lambda i: (0, 0)
    return _call(
        _norm_router_kernel, (n_rows // tm,),
        [pl.BlockSpec((tm, D), row), pl.BlockSpec((1, D), fixed),
         _mod_spec(tok, l, 3, tm, 0, D, 1), _mod_spec(tok, l, 4, tm, 0, D, 1),
         pl.BlockSpec((D, E), fixed), pl.BlockSpec((1, E), fixed)],
        [pl.BlockSpec((tm, D), row), pl.BlockSpec((tm, E), row), pl.BlockSpec((tm, E), row),
         pl.BlockSpec((tm, E), row), pl.BlockSpec((1, E), fixed)],
        [jax.ShapeDtypeStruct((n_rows, D), BF16), jax.ShapeDtypeStruct((n_rows, E), jnp.int32),
         jax.ShapeDtypeStruct((n_rows, E), F32), jax.ShapeDtypeStruct((n_rows, E), jnp.int32),
         jax.ShapeDtypeStruct((1, E), jnp.int32)],
        "norm_router", scratch=[pltpu.VMEM((1, E), F32)])(
            X, g.reshape(1, D), mods3, mods3, w_router, b_router.reshape(1, E))


def _mm_kernel(a_ref, b_ref, o_ref):
    o_ref[...] = jnp.dot(a_ref[...], b_ref[...], preferred_element_type=F32).astype(o_ref.dtype)


def _mm(a, b, tm_pref=1024, tn_pref=1024, name="mm"):
    M, K = a.shape
    N = b.shape[1]
    tm = _tile(M, tm_pref, 16)
    tn = _tile(N, tn_pref, LANE)
    return _call(
        _mm_kernel, (M // tm, N // tn),
        [pl.BlockSpec((tm, K), lambda i, j: (i, 0)), pl.BlockSpec((K, tn), lambda i, j: (0, j))],
        pl.BlockSpec((tm, tn), lambda i, j: (i, j)),
        jax.ShapeDtypeStruct((M, N), BF16), name)(a, b)


def _norm_mm_kernel(a_ref, g_ref, b_ref, o_ref):
    a = a_ref[...].astype(F32)
    r = lax.rsqrt(jnp.mean(a * a, axis=-1, keepdims=True) + EPS)
    an = (a * r * g_ref[...]).astype(BF16)
    o_ref[...] = jnp.dot(an, b_ref[...], preferred_element_type=F32).astype(o_ref.dtype)


def _norm_mm(a, col_off, K, g, b, name):
    M = a.shape[0]
    N = b.shape[1]
    tm = _tile(M, 512, 16)
    cb = col_off // K
    assert col_off % K == 0
    return _call(
        _norm_mm_kernel, (M // tm,),
        [pl.BlockSpec((tm, K), lambda i: (i, cb)),
         pl.BlockSpec((1, K), lambda i: (0, 0)),
         pl.BlockSpec((K, N), lambda i: (0, 0))],
        pl.BlockSpec((tm, N), lambda i: (i, 0)),
        jax.ShapeDtypeStruct((M, N), BF16), name)(a, g.reshape(1, K), b)


def _headprep_kernel(*refs, n_chunks, rope_chunk, inv_d, shift):
    x_refs = refs[:n_chunks]
    g_ref, cos_ref, sin_ref, o_ref = refs[n_chunks:]
    xs = [r[...].astype(F32) for r in x_refs]
    ssq = sum(jnp.sum(x * x, axis=-1, keepdims=True) for x in xs)
    r = lax.rsqrt(ssq * inv_d + EPS)
    for c, x in enumerate(xs):
        y = x * r * g_ref[:, c * LANE:(c + 1) * LANE]
        if c == rope_chunk:
            lane = lax.broadcasted_iota(jnp.int32, y.shape, 1)
            first_half = (lane % (2 * shift)) < shift
            swapped = jnp.where(first_half, pltpu.roll(y, LANE - shift, 1), pltpu.roll(y, shift, 1))
            y = y * cos_ref[...] + swapped * sin_ref[...]
        o_ref[:, c * LANE:(c + 1) * LANE] = y.astype(o_ref.dtype)


def _headprep(chunks, n_rows, n_heads, g, cos, sin, rope_chunk, norm_dim, shift, tok, tok_off, name):
    n_chunks = len(chunks)
    W = n_chunks * LANE
    tm = _tile(n_rows, 512, 16, tok.N)
    assert cos.shape[0] >= tok.N + tm

    def pos_idx(i, h):
        return (tok.pos_block(i * tm + tok_off, tm), 0)

    in_specs = [pl.BlockSpec((tm, LANE), (lambda i, h, f=f: (i, f(h)))) for _, f in chunks]
    in_specs += [pl.BlockSpec((1, W), lambda i, h: (0, 0)),
                 pl.BlockSpec((tm, LANE), pos_idx), pl.BlockSpec((tm, LANE), pos_idx)]
    kern = functools.partial(_headprep_kernel, n_chunks=n_chunks, rope_chunk=rope_chunk,
                             inv_d=1.0 / norm_dim, shift=shift)
    return _call(
        kern, (n_rows // tm, n_heads), in_specs,
        pl.BlockSpec((tm, W), lambda i, h: (i, h)),
        jax.ShapeDtypeStruct((n_rows, n_heads * W), BF16), name)(*[a for a, _ in chunks], g.reshape(1, W), cos, sin)


def _rope_tables(n_pos, rope_dim, pad_rows):
    nf = rope_dim // 4
    t = jnp.arange(n_pos, dtype=jnp.int32)
    row = (t // GRID_W).astype(F32)
    col = (t % GRID_W).astype(F32)
    inv = ROPE_BASE ** (-jnp.arange(nf, dtype=F32) / nf)
    a_row, a_col = row[:, None] * inv, col[:, None] * inv
    cos = jnp.concatenate([jnp.cos(a_row)] * 2 + [jnp.cos(a_col)] * 2, axis=-1)
    sin = jnp.concatenate([-jnp.sin(a_row), jnp.sin(a_row), -jnp.sin(a_col), jnp.sin(a_col)], axis=-1)
    cos = jnp.pad(cos, ((0, pad_rows), (0, LANE - rope_dim)), constant_values=1.0)
    sin = jnp.pad(sin, ((0, pad_rows), (0, LANE - rope_dim)))
    return cos, sin


def _attn_kernel(*refs, n_src, tq, n_q):
    q_ref = refs[0]
    k_refs = refs[1:1 + n_src]
    v_refs = refs[1 + n_src:1 + 2 * n_src]
    o_ref = refs[-1]

    for c in range(n_q // tq):
        rows = pl.ds(c * tq, tq)
        q = q_ref[rows, :]
        s = [lax.dot_general(q, k[...], (((1,), (1,)), ((), ())), preferred_element_type=F32) for k in k_refs]
        m = functools.reduce(jnp.maximum, [jnp.max(si, axis=-1, keepdims=True) for si in s])
        p = [jnp.exp(si - m) for si in s]
        l = sum(jnp.sum(pi, axis=-1, keepdims=True) for pi in p)
        o = sum(jnp.dot(pi.astype(BF16), v[...], preferred_element_type=F32) for pi, v in zip(p, v_refs))
        o_ref[rows, :] = (o * (1.0 / l)).astype(o_ref.dtype)


def _attend(q, q_row_off, n_q, k_srcs, v_srcs, B, n_kv_heads, group, dk, dv, name):
    n_src = len(k_srcs)
    tq = _tile(n_q, 512, 16)
    qb = q_row_off // n_q
    assert q_row_off % n_q == 0
    in_specs = [pl.BlockSpec((n_q, dk), lambda b, hk, g: (qb + b, hk * group + g))]
    args = [q]
    for width, srcs in ((dk, k_srcs), (dv, v_srcs)):
        for arr, row_off, n_k, cb0 in srcs:
            assert row_off % n_k == 0
            rb = row_off // n_k
            in_specs.append(pl.BlockSpec((n_k, width), (lambda b, hk, g, rb=rb, cb0=cb0: (rb + b, cb0 + hk))))
            args.append(arr)
    kern = functools.partial(_attn_kernel, n_src=n_src, tq=tq, n_q=n_q)
    return _call(
        kern, (B, n_kv_heads, group), in_specs,
        pl.BlockSpec((n_q, dv), lambda b, hk, g: (b, hk * group + g)),
        jax.ShapeDtypeStruct((B * n_q, n_kv_heads * group * dv), BF16), name)(*args)


def _conv_kernel(b_ref, c_ref, u_ref, w_ref, o_ref):
    v = c_ref[...].astype(F32) * u_ref[...].astype(F32)
    n = v.shape[0]
    row = lax.broadcasted_iota(jnp.int32, v.shape, 0)
    prev = jnp.where(row == 0, 0.0, pltpu.roll(v, 1, 0))
    nxt = jnp.where(row == n - 1, 0.0, pltpu.roll(v, n - 1, 0))
    y = prev * w_ref[0:1, :] + v * w_ref[1:2, :] + nxt * w_ref[2:3, :]
    o_ref[...] = (b_ref[...].astype(F32) * y).astype(o_ref.dtype)


def _conv(px, row_off, n_seq, seq_len, col_b, col_c, col_u, CW, conv_w):
    tc = _tile(CW, 512, LANE, col_b, col_c, col_u)
    rb = row_off // seq_len
    assert row_off % seq_len == 0

    def spec(col):
        return pl.BlockSpec((seq_len, tc), lambda s, j, col=col: (rb + s, col // tc + j))

    return _call(
        _conv_kernel, (n_seq, CW // tc),
        [spec(col_b), spec(col_c), spec(col_u), pl.BlockSpec((3, tc), lambda s, j: (0, j))],
        pl.BlockSpec((seq_len, tc), lambda s, j: (s, j)),
        jax.ShapeDtypeStruct((n_seq * seq_len, CW), BF16), "short_conv")(px, px, px, conv_w)


def _merge_kernel(oa_ref, ob_ref, oc_ref, wa_ref, wb_ref, wc_ref, ga_ref, gb_ref, gc_ref, o_ref):
    def term(o, w, g):
        return jax.nn.sigmoid(g[...].astype(F32)) * jnp.dot(o[...], w[...], preferred_element_type=F32)
    h = term(oa_ref, wa_ref, ga_ref) + term(ob_ref, wb_ref, gb_ref) + term(oc_ref, wc_ref, gc_ref)
    o_ref[...] = h.astype(o_ref.dtype)


def _merge(oa, ob, oc, w_pa, w_pb, w_pc, px, gate_col, n_rows):
    D = w_pa.shape[1]
    tm = _tile(n_rows, 1024, 16)
    tn = _tile(D, 512, LANE, gate_col)
    gb0 = gate_col // tn
    nb = D // tn

    def a_spec(o):
        return pl.BlockSpec((tm, o.shape[1]), lambda i, j: (i, 0))

    def w_spec(w):
        return pl.BlockSpec((w.shape[0], tn), lambda i, j: (0, j))

    def g_spec(k):
        return pl.BlockSpec((tm, tn), lambda i, j, k=k: (i, gb0 + k * nb + j))

    return _call(
        _merge_kernel, (n_rows // tm, nb),
        [a_spec(oa), a_spec(ob), a_spec(oc), w_spec(w_pa), w_spec(w_pb), w_spec(w_pc),
         g_spec(0), g_spec(1), g_spec(2)],
        pl.BlockSpec((tm, tn), lambda i, j: (i, j)),
        jax.ShapeDtypeStruct((n_rows, D), BF16), "merge")(oa, ob, oc, w_pa, w_pb, w_pc, px, px, px)


def _mm_res_kernel(a_ref, b_ref, x_ref, gt_ref, o_ref):
    acc = jnp.dot(a_ref[...], b_ref[...], preferred_element_type=F32)
    o_ref[...] = x_ref[...] + gt_ref[...] * acc


def _mm_res(a, b, X, mods3, tok, l, which, n_rows):
    K = a.shape[1]
    D = b.shape[1]
    tm = _tile(n_rows, 1024, 16, tok.N)
    tn = _tile(D, 512, LANE)

    def gt_idx(i, j):
        row = tok.mod_row(i * tm)
        return ((l * MOD_ROWS + row) * N_MOD + which, 0, j)

    return _call(
        _mm_res_kernel, (n_rows // tm, D // tn),
        [pl.BlockSpec((tm, K), lambda i, j: (i, 0)), pl.BlockSpec((K, tn), lambda i, j: (0, j)),
         pl.BlockSpec((tm, tn), lambda i, j: (i, j)), pl.BlockSpec((None, 1, tn), gt_idx)],
        pl.BlockSpec((tm, tn), lambda i, j: (i, j)),
        jax.ShapeDtypeStruct((n_rows, D), F32), "proj_residual")(a, b, X, mods3)


def _swiglu_up_kernel(h_ref, w_ref, o_ref):
    gu = jnp.dot(h_ref[...], w_ref[...], preferred_element_type=F32)
    f = o_ref.shape[1]
    o_ref[...] = (_silu(gu[:, :f]) * gu[:, f:]).astype(o_ref.dtype)


def _swiglu_up(h, w_gu):
    M, D = h.shape
    f = w_gu.shape[1] // 2
    tm = _tile(M, 1024, 16)
    return _call(
        _swiglu_up_kernel, (M // tm,),
        [pl.BlockSpec((tm, D), lambda i: (i, 0)), pl.BlockSpec((D, 2 * f), lambda i: (0, 0))],
        pl.BlockSpec((tm, f), lambda i: (i, 0)),
        jax.ShapeDtypeStruct((M, f), BF16), "shared_up")(h, w_gu)


SLAB_ROWS = 16
EXPERT_TILE = 256
HI_MASK = -65536


def _pack_pairs(v):
    half = v.shape[1] // 2
    bits = lax.bitcast_convert_type(v.astype(BF16).astype(F32), jnp.int32)
    return lax.shift_right_logical(bits[:, :half], 16) | (bits[:, half:] & HI_MASK)


def _unpack_lo(w):
    return lax.bitcast_convert_type(lax.shift_left(w, 16), F32)


def _unpack_hi(w):
    return lax.bitcast_convert_type(w & HI_MASK, F32)


def _dispatch_kernel(pos_ref, zt_ref, h_ref, xs_hbm, slab_ref, zero_ref, sem, *, tm, n_experts):
    W = slab_ref.shape[1]
    trows = EXPERT_TILE * SLAB_ROWS

    @pl.when(pl.program_id(0) == 0)
    def _():
        zero_ref[...] = jnp.zeros_like(zero_ref)
        def zcopy(t):
            return pltpu.make_async_copy(zero_ref, xs_hbm.at[pl.ds(pl.multiple_of(t * trows, trows), trows), :], sem)
        for e in range(n_experts):
            zcopy(zt_ref[0, e]).start()
        for e in range(n_experts):
            zcopy(zt_ref[0, e]).wait()
        n_used = zt_ref[0, n_experts]
        n_tiles = xs_hbm.shape[0] // trows

        def zstart(t, carry):
            zcopy(t).start()
            return carry

        def zwait(t, carry):
            zcopy(t).wait()
            return carry

        lax.fori_loop(n_used, n_tiles, zstart, 0)
        lax.fori_loop(n_used, n_tiles, zwait, 0)

    words = _pack_pairs(h_ref[...])
    for s in range(SLAB_ROWS):
        slab_ref[pl.ds(s, tm, stride=SLAB_ROWS), :] = words[:, s * W:(s + 1) * W]

    def row_copy(r, k):
        p = pos_ref[0, r * TOP_K + k]
        return pltpu.make_async_copy(
            slab_ref.at[pl.ds(pl.multiple_of(r * SLAB_ROWS, SLAB_ROWS), SLAB_ROWS), :],
            xs_hbm.at[pl.ds(pl.multiple_of(p * SLAB_ROWS, SLAB_ROWS), SLAB_ROWS), :], sem)

    def issue(r, carry):
        for k in range(TOP_K):
            row_copy(r, k).start()
        return carry

    lax.fori_loop(0, tm, issue, 0)
    for k in range(TOP_K):
        pltpu.make_async_copy(slab_ref, xs_hbm.at[pl.ds(0, tm * SLAB_ROWS), :], sem).wait()


def _dispatch(h2, pos, zero_tiles, n_slots):
    M, D = h2.shape
    W = D // (2 * SLAB_ROWS)
    assert W == LANE, "strided slab access needs LANE-wide slab rows"
    E = zero_tiles.shape[0] - 1
    tm = _tile(M, 256, 16)
    kern = functools.partial(_dispatch_kernel, tm=tm, n_experts=E)
    return _call(
        kern, (M // tm,),
        [pl.BlockSpec((None, 1, tm * TOP_K), lambda i: (i, 0, 0), memory_space=pltpu.SMEM),
         pl.BlockSpec((1, E + 1), lambda i: (0, 0), memory_space=pltpu.SMEM),
         pl.BlockSpec((tm, D), lambda i: (i, 0))],
        pl.BlockSpec(memory_space=pl.ANY),
        jax.ShapeDtypeStruct((n_slots * SLAB_ROWS, W), jnp.int32), "moe_dispatch",
        scratch=[pltpu.VMEM((tm * SLAB_ROWS, W), jnp.int32),
                 pltpu.VMEM((EXPERT_TILE * SLAB_ROWS, W), jnp.int32),
                 pltpu.SemaphoreType.DMA(())])(
            pos.reshape(M // tm, 1, tm * TOP_K), zero_tiles.reshape(1, E + 1), h2)


def _expert_kernel(te_ref, nu_ref, x_ref, wgu_ref, wdn_ref, o_ref):
    W = x_ref.shape[1]
    f = wdn_ref.shape[0]

    @pl.when(pl.program_id(0) < nu_ref[0])
    def _():
        words = [x_ref[pl.ds(s, EXPERT_TILE, stride=SLAB_ROWS), :] for s in range(SLAB_ROWS)]
        x = jnp.concatenate([_unpack_lo(w).astype(BF16) for w in words]
                            + [_unpack_hi(w).astype(BF16) for w in words], axis=1)
        gu = jnp.dot(x, wgu_ref[...], preferred_element_type=F32)
        act = (_silu(gu[:, :f]) * gu[:, f:]).astype(BF16)
        y = _pack_pairs(jnp.dot(act, wdn_ref[...], preferred_element_type=F32))
        for s in range(SLAB_ROWS):
            o_ref[pl.ds(s, EXPERT_TILE, stride=SLAB_ROWS), :] = y[:, s * W:(s + 1) * W]

    @pl.when(pl.program_id(0) >= nu_ref[0])
    def _():
        o_ref[...] = jnp.zeros_like(o_ref)


def _experts(xs, tile_expert, n_used, w_gu, w_down):
    E, D, F2 = w_gu.shape
    W = xs.shape[1]
    trows = EXPERT_TILE * SLAB_ROWS
    n_tiles = xs.shape[0] // trows
    grid_spec = pltpu.PrefetchScalarGridSpec(
        num_scalar_prefetch=2, grid=(n_tiles,),
        in_specs=[pl.BlockSpec((trows, W), lambda i, te, nu: (jnp.minimum(i, nu[0] - 1), 0)),
                  pl.BlockSpec((None, D, F2), lambda i, te, nu: (te[i], 0, 0)),
                  pl.BlockSpec((None, F2 // 2, D), lambda i, te, nu: (te[i], 0, 0))],
        out_specs=pl.BlockSpec((trows, W), lambda i, te, nu: (i, 0)))
    return pl.pallas_call(
        _expert_kernel, grid_spec=grid_spec, out_shape=jax.ShapeDtypeStruct(xs.shape, jnp.int32),
        name="moe_experts",
        compiler_params=pltpu.CompilerParams(dimension_semantics=("arbitrary",),
                                             vmem_limit_bytes=VMEM_LIMIT_BYTES))(
            tile_expert, n_used, xs, w_gu, w_down)


def _combine_kernel(pos_ref, posn_ref, wt_ref, sh_ref, x_ref, gt_ref, ys_hbm, o_ref, buf_a, buf_b, sem, *, tm):
    j = pl.program_id(0)
    W = buf_a.shape[1]
    half = SLAB_ROWS * W

    def issue(pref, first_tok, buf, s_idx):
        def body(r, carry):
            for k in range(TOP_K):
                p = pref[0, (first_tok + r) * TOP_K + k]
                pltpu.make_async_copy(
                    ys_hbm.at[pl.ds(pl.multiple_of(p * SLAB_ROWS, SLAB_ROWS), SLAB_ROWS), :],
                    buf.at[pl.ds(pl.multiple_of((k * tm + r) * SLAB_ROWS, SLAB_ROWS), SLAB_ROWS), :],
                    sem.at[s_idx]).start()
            return carry
        lax.fori_loop(0, tm, body, 0)

    def drain(buf, s_idx):
        for k in range(TOP_K):
            pltpu.make_async_copy(ys_hbm.at[pl.ds(0, tm * SLAB_ROWS), :],
                                  buf.at[pl.ds(0, tm * SLAB_ROWS), :], sem.at[s_idx]).wait()

    def reduce(buf, r0):
        rows = pl.ds(r0, tm)
        wt = wt_ref[rows, :]
        wk = [wt[:, k:k + 1] for k in range(TOP_K)]
        for s in range(SLAB_ROWS):
            lo = hi = None
            for k in range(TOP_K):
                w = buf[pl.ds(k * tm * SLAB_ROWS + s, tm, stride=SLAB_ROWS), :]
                lo_k, hi_k = wk[k] * _unpack_lo(w), wk[k] * _unpack_hi(w)
                lo, hi = (lo_k, hi_k) if lo is None else (lo + lo_k, hi + hi_k)
            for c0, routed in ((s * W, lo), (half + s * W, hi)):
                cols = slice(c0, c0 + W)
                o_ref[rows, cols] = x_ref[rows, cols] + gt_ref[:, cols] * (sh_ref[rows, cols].astype(F32) + routed)

    @pl.when(j == 0)
    def _():
        issue(pos_ref, 0, buf_a, 0)

    issue(pos_ref, tm, buf_b, 1)
    drain(buf_a, 0)
    reduce(buf_a, 0)

    @pl.when(j + 1 < pl.num_programs(0))
    def _():
        issue(posn_ref, 0, buf_a, 0)

    drain(buf_b, 1)
    reduce(buf_b, tm)


def _combine(ys, pos, wt, sh, X, mods3, tok, l, which, n_rows):
    D = X.shape[1]
    W = ys.shape[1]
    E = wt.shape[1]
    tm = _tile(n_rows // 2, 128, 16, tok.N // 2)
    tb = 2 * tm
    n_steps = n_rows // tb
    pos3 = pos.reshape(n_steps, 1, tb * TOP_K)

    def gt_idx(j):
        row = tok.mod_row(j * tb)
        return ((l * MOD_ROWS + row) * N_MOD + which, 0, 0)

    kern = functools.partial(_combine_kernel, tm=tm)
    return _call(
        kern, (n_steps,),
        [pl.BlockSpec((None, 1, tb * TOP_K), lambda j: (j, 0, 0), memory_space=pltpu.SMEM),
         pl.BlockSpec((None, 1, tb * TOP_K), lambda j: (jnp.minimum(j + 1, n_steps - 1), 0, 0),
                      memory_space=pltpu.SMEM),
         pl.BlockSpec((tb, E), lambda j: (j, 0)),
         pl.BlockSpec((tb, D), lambda j: (j, 0)),
         pl.BlockSpec((tb, D), lambda j: (j, 0)),
         pl.BlockSpec((None, 1, D), gt_idx),
         pl.BlockSpec(memory_space=pl.ANY)],
        pl.BlockSpec((tb, D), lambda j: (j, 0)),
        jax.ShapeDtypeStruct((n_rows, D), F32), "moe_combine",
        scratch=[pltpu.VMEM((TOP_K * tm * SLAB_ROWS, W), jnp.int32),
                 pltpu.VMEM((TOP_K * tm * SLAB_ROWS, W), jnp.int32),
                 pltpu.SemaphoreType.DMA((2,))])(pos3, pos3, wt, sh, X, mods3, ys)


def _routing_plan(idx, rank, counts, n_tiles):
    E = counts.shape[0]
    padded = (counts + EXPERT_TILE - 1) // EXPERT_TILE * EXPERT_TILE
    ends = jnp.cumsum(padded)
    base = ends - padded
    pos = jnp.take(base, idx) + rank
    tile_expert = jnp.minimum(
        jnp.searchsorted(ends, jnp.arange(n_tiles, dtype=jnp.int32) * EXPERT_TILE, side="right"), E - 1)
    n_used = (ends[-1] // EXPERT_TILE).reshape(1)
    last_tile = jnp.maximum(ends // EXPERT_TILE - 1, 0)
    return pos.astype(jnp.int32), tile_expert.astype(jnp.int32), n_used.astype(jnp.int32), last_tile.astype(jnp.int32)


def kernel(x, c, ctx, c_ctx, w_mod, b_mod, g_mix, g_ffn, w_in, mla_q_norm_g, mla_kv_norm_g, w_uq, w_ukv,
           mla_q_head_g, mla_k_head_g, gqa_q_g, gqa_k_g, conv_w, w_pa, w_pb, w_pc, w_o, w_router, b_router,
           w_exp_gu, w_exp_down, w_sh_gu, w_sh_down):
    B, N, D = x.shape
    CTX = ctx.shape[1]
    L = w_mod.shape[0]
    QL = mla_q_norm_g.shape[1]
    KVL = mla_kv_norm_g.shape[1]
    HA = w_uq.shape[2] // MLA_QK
    HG = w_pc.shape[1] // GQA_HD
    CW = conv_w.shape[2]
    D_IN = w_in.shape[2]
    HKV = (D_IN - (KVL + MLA_ROPE + QL + HG * GQA_HD + 3 * CW + 3 * D)) // (2 * GQA_HD)
    GKV = HKV * GQA_HD
    E = w_router.shape[2]

    tok = _Tok(B, N, CTX)
    n_lat, n_ctx = tok.n_lat, tok.n_ctx

    KV_COLS = KVL + 2 * GKV + LANE
    col_gk, col_gv, col_kr = KVL, KVL + GKV, KVL + 2 * GKV
    main0 = KVL + MLA_ROPE + 2 * GKV
    col_cq, col_gq = 0, QL
    col_b, col_c, col_u = QL + HG * GQA_HD, QL + HG * GQA_HD + CW, QL + HG * GQA_HD + 2 * CW
    col_gate = col_u + CW

    cvec = jnp.zeros((MOD_ROWS, D), F32).at[:B].set(c).at[B].set(c_ctx)
    mods = _mods(cvec, w_mod, b_mod)
    mods3 = mods.reshape(L * MOD_ROWS * N_MOD, 1, D)

    pad_rows = _tile(n_lat + n_ctx, 512, 16, N)
    cos_a, sin_a = _rope_tables(N, MLA_ROPE, pad_rows)
    cos_g, sin_g = _rope_tables(N, GQA_HD, pad_rows)

    X = jnp.concatenate([x.reshape(n_lat, D), ctx.reshape(n_ctx, D)], axis=0)

    for l in range(L):
        last = l == L - 1
        n_rows = n_lat if last else n_lat + n_ctx

        wl = w_in[l]
        w_main = wl[:, main0:].astype(BF16)
        w_kv = jnp.concatenate(
            [wl[:, :KVL], wl[:, KVL + MLA_ROPE:main0], wl[:, KVL:KVL + MLA_ROPE],
             jnp.zeros((D, LANE - MLA_ROPE), F32)], axis=1).astype(BF16)
        w_uq_p = jnp.pad(w_uq[l].reshape(QL, HA, MLA_QK),
                         ((0, 0), (0, 0), (0, MLA_QK_PAD - MLA_QK))).reshape(QL, HA * MLA_QK_PAD).astype(BF16)
        w_ukv_r = w_ukv[l].reshape(KVL, HA, MLA_NOPE + MLA_V)
        w_ukv_p = jnp.concatenate([w_ukv_r[:, :, :MLA_NOPE].reshape(KVL, HA * MLA_NOPE),
                                   w_ukv_r[:, :, MLA_NOPE:].reshape(KVL, HA * MLA_V)], axis=1).astype(BF16)
        g_qa = jnp.pad(mla_q_head_g[l] * (1.0 / math.sqrt(MLA_QK)), (0, MLA_QK_PAD - MLA_QK))
        g_ka = jnp.pad(mla_k_head_g[l], (0, MLA_QK_PAD - MLA_QK))
        g_qg = gqa_q_g[l] * (1.0 / math.sqrt(GQA_HD))
        g_kg = gqa_k_g[l]

        h = _norm_mod(X, 0, n_rows, g_mix[l], mods3, tok, l, 0, 0)
        px = _mm(h, w_main, name="in_proj")
        if last:
            hc = _norm_mod(X, n_lat, n_ctx, g_mix[l], mods3, tok, l, 0, n_lat)
            pkv_x = _mm(h, w_kv, tn_pref=KV_COLS, name="in_proj_kv")
            pkv_c = _mm(hc, w_kv, tn_pref=KV_COLS, name="in_proj_kv")
            kv_parts = [(pkv_x, n_lat, 0), (pkv_c, n_ctx, n_lat)]
        else:
            pkv = _mm(h, w_kv, tn_pref=KV_COLS, name="in_proj_kv")
            kv_parts = [(pkv, n_rows, 0)]

        ka, kvr, kg, pk = [], [], [], []
        for pkv_p, rows_p, tok_off in kv_parts:
            kv_raw = _norm_mm(pkv_p, 0, KVL, mla_kv_norm_g[l], w_ukv_p, "mla_kv_up")
            ka.append(_headprep(
                [(kv_raw, lambda hh: hh), (pkv_p, lambda hh: col_kr // LANE)], rows_p, HA, g_ka, cos_a, sin_a,
                1, MLA_QK, MLA_ROPE // 4, tok, tok_off, "mla_k_prep"))
            kg.append(_headprep(
                [(pkv_p, lambda hh: col_gk // LANE + hh)], rows_p, HKV, g_kg, cos_g, sin_g,
                0, GQA_HD, GQA_HD // 4, tok, tok_off, "gqa_k_prep"))
            kvr.append(kv_raw)
            pk.append(pkv_p)

        q_raw = _norm_mm(px, col_cq, QL, mla_q_norm_g[l], w_uq_p, "mla_q_up")
        qa = _headprep([(q_raw, lambda hh: 2 * hh), (q_raw, lambda hh: 2 * hh + 1)], n_rows, HA, g_qa,
                       cos_a, sin_a, 1, MLA_QK, MLA_ROPE // 4, tok, 0, "mla_q_prep")
        qg = _headprep([(px, lambda hh: col_gq // LANE + hh)], n_rows, HG, g_qg, cos_g, sin_g,
                       0, GQA_HD, GQA_HD // 4, tok, 0, "gqa_q_prep")

        ctx_i = len(kv_parts) - 1
        ctx_off = 0 if last else n_lat
        vA0 = HA * MLA_NOPE // MLA_V
        ka_x, ka_c = (ka[0], 0, N, 0), (ka[ctx_i], ctx_off, CTX, 0)
        va_x, va_c = (kvr[0], 0, N, vA0), (kvr[ctx_i], ctx_off, CTX, vA0)
        kg_x, kg_c = (kg[0], 0, N, 0), (kg[ctx_i], ctx_off, CTX, 0)
        vg_x, vg_c = (pk[0], 0, N, col_gv // GQA_HD), (pk[ctx_i], ctx_off, CTX, col_gv // GQA_HD)

        oa = _attend(qa, 0, N, [ka_c, ka_x], [va_c, va_x], B, HA, 1, MLA_QK_PAD, MLA_V, "mla_attn")
        og = _attend(qg, 0, N, [kg_c, kg_x], [vg_c, vg_x], B, HKV, HG // HKV, GQA_HD, GQA_HD, "gqa_attn")
        ob = _conv(px, 0, B, N, col_b, col_c, col_u, CW, conv_w[l])
        if not last:
            oa_c = _attend(qa, n_lat, CTX, [ka_c], [va_c], B, HA, 1, MLA_QK_PAD, MLA_V, "mla_attn_ctx")
            og_c = _attend(qg, n_lat, CTX, [kg_c], [vg_c], B, HKV, HG // HKV, GQA_HD, GQA_HD, "gqa_attn_ctx")
            ob_c = _conv(px, n_lat, B, CTX, col_b, col_c, col_u, CW, conv_w[l])
            oa = jnp.concatenate([oa, oa_c], axis=0)
            og = jnp.concatenate([og, og_c], axis=0)
            ob = jnp.concatenate([ob, ob_c], axis=0)

        hm = _merge(oa, ob, og, w_pa[l].astype(BF16), w_pb[l].astype(BF16), w_pc[l].astype(BF16),
                    px, col_gate, n_rows)
        X = _mm_res(hm, w_o[l].astype(BF16), X, mods3, tok, l, 2, n_rows)

        h2, idx_m, wt_m, rank_m, cnt = _norm_router(X, n_rows, g_ffn[l], mods3, tok, l, w_router[l], b_router[l])
        n_tiles = n_rows * TOP_K // EXPERT_TILE + E
        pos, tile_expert, n_used, last_tile = _routing_plan(idx_m[:, :TOP_K], rank_m[:, :TOP_K], cnt[0], n_tiles)
        xs = _dispatch(h2, pos, jnp.concatenate([last_tile, n_used]), n_tiles * EXPERT_TILE)
        ys = _experts(xs, tile_expert, n_used, w_exp_gu[l].astype(BF16), w_exp_down[l].astype(BF16))
        sh = _mm(_swiglu_up(h2, w_sh_gu[l].astype(BF16)), w_sh_down[l].astype(BF16), name="shared_down")
        X = _combine(ys, pos, wt_m, sh, X, mods3, tok, l, 5, n_rows)

    return X.reshape(B, N, D)
```

```python
import functools
import math

import jax
import jax.numpy as jnp
from jax import lax
from jax.experimental import pallas as pl
from jax.experimental.pallas import tpu as pltpu

F32 = jnp.float32
BF16 = jnp.bfloat16

GRID_W = 64
ROPE_BASE = 10000.0
EPS = 1e-6
MLA_NOPE = 128
MLA_ROPE = 64
MLA_QK = MLA_NOPE + MLA_ROPE
MLA_V = 128
GQA_HD = 128
TOP_K = 4
ROUTE_SCALE = 2.5
N_MOD = 6

LANE = 128
MLA_QK_PAD = 2 * LANE
MOD_ROWS = 16
VMEM_LIMIT_BYTES = 56 * 1024 * 1024


def _tile(dim, pref, unit, *also):
    t = min(pref, dim) // unit * unit
    while t > unit:
        if dim % t == 0 and all(a % t == 0 for a in also):
            return t
        t -= unit
    assert dim % unit == 0 and all(a % unit == 0 for a in also), (dim, unit, also)
    return unit


def _call(kernel, grid, in_specs, out_specs, out_shape, name, scratch=()):
    return pl.pallas_call(
        kernel, grid=grid, in_specs=in_specs, out_specs=out_specs, out_shape=out_shape,
        scratch_shapes=list(scratch), name=name,
        compiler_params=pltpu.CompilerParams(
            dimension_semantics=("arbitrary",) * len(grid), vmem_limit_bytes=VMEM_LIMIT_BYTES))


def _silu(v):
    return v * jax.nn.sigmoid(v)


def _mod_kernel(c_ref, w_ref, b_ref, o_ref):
    s = _silu(c_ref[...]).astype(BF16)
    o_ref[0] = jnp.dot(s, w_ref[0].astype(BF16), preferred_element_type=F32) + b_ref[0]


def _mods(cvec, w_mod, b_mod):
    L, D, N6 = w_mod.shape
    tn = _tile(N6, 512, LANE)
    return _call(
        _mod_kernel, (L, N6 // tn),
        [pl.BlockSpec((MOD_ROWS, D), lambda l, j: (0, 0)),
         pl.BlockSpec((1, D, tn), lambda l, j: (l, 0, j)),
         pl.BlockSpec((1, 1, tn), lambda l, j: (l, 0, j))],
        pl.BlockSpec((1, MOD_ROWS, tn), lambda l, j: (l, 0, j)),
        jax.ShapeDtypeStruct((L, MOD_ROWS, N6), F32), "adaln_mod")(cvec, w_mod, b_mod.reshape(L, 1, N6))


class _Tok:
    def __init__(self, B, N, CTX):
        self.B, self.N, self.CTX = B, N, CTX
        self.n_lat = B * N
        self.n_ctx = B * CTX

    def mod_row(self, tok0):
        return jnp.where(tok0 < self.n_lat, tok0 // self.N, self.B)

    def pos_block(self, tok0, tm):
        return jnp.where(tok0 < self.n_lat, (tok0 % self.N) // tm, self.N // tm)


def _mod_spec(tok, l, which, tm, tok_off, D, ngrid):
    def idx(i, *_):
        row = tok.mod_row(i * tm + tok_off)
        return ((l * MOD_ROWS + row) * N_MOD + which, 0, 0)
    return pl.BlockSpec((None, 1, D), idx)


def _two_source_specs(X, Xb, block, row_tile, col_of):
    na = X.shape[0] // row_tile
    return ([pl.BlockSpec(block, lambda i, *r: (jnp.minimum(i, na - 1), col_of(*r))),
             pl.BlockSpec(block, lambda i, *r: (jnp.maximum(i - na, 0), col_of(*r)))], [X, Xb], na)


def _norm_mod_kernel(*refs, n_a):
    x_refs, (g_ref, sh_ref, sc_ref, o_ref) = refs[:-4], refs[-4:]
    x = x_refs[0][...]
    if len(x_refs) == 2:
        x = jnp.where(pl.program_id(0) < n_a, x, x_refs[1][...])
    r = lax.rsqrt(jnp.mean(x * x, axis=-1, keepdims=True) + EPS)
    o_ref[...] = ((x * r * g_ref[...]) * (1.0 + sc_ref[...]) + sh_ref[...]).astype(o_ref.dtype)


def _norm_mod(X, row_off, n_rows, g, mods3, tok, l, which_shift, tok_off, Xb=None):
    D = X.shape[1]
    tm = _tile(n_rows, 256, 16, tok.N, row_off) if row_off else _tile(n_rows, 256, 16, tok.N)
    ro = row_off // tm
    if Xb is None:
        x_specs, x_args, na = [pl.BlockSpec((tm, D), lambda i: (i + ro, 0))], [X], 0
    else:
        assert row_off == 0 and X.shape[0] % tm == 0
        x_specs, x_args, na = _two_source_specs(X, Xb, (tm, D), tm, lambda: 0)
    return _call(
        functools.partial(_norm_mod_kernel, n_a=na), (n_rows // tm,),
        x_specs + [pl.BlockSpec((1, D), lambda i: (0, 0)),
                   _mod_spec(tok, l, which_shift, tm, tok_off, D, 1),
                   _mod_spec(tok, l, which_shift + 1, tm, tok_off, D, 1)],
        pl.BlockSpec((tm, D), lambda i: (i, 0)),
        jax.ShapeDtypeStruct((n_rows, D), BF16), "norm_mod")(*x_args, g.reshape(1, D), mods3, mods3)


def _norm_router_kernel(x_ref, g_ref, sh_ref, sc_ref, wr_ref, br_ref,
                        h_ref, idx_ref, wt_ref, rank_ref, cnt_ref, carry_ref):
    @pl.when(pl.program_id(0) == 0)
    def _():
        carry_ref[...] = jnp.zeros_like(carry_ref)

    x = x_ref[...]
    r = lax.rsqrt(jnp.mean(x * x, axis=-1, keepdims=True) + EPS)
    h = (x * r * g_ref[...]) * (1.0 + sc_ref[...]) + sh_ref[...]
    h_ref[...] = h.astype(h_ref.dtype)
    n_e = br_ref.shape[1]
    h_hi = h.astype(BF16)
    h_lo = (h - h_hi.astype(F32)).astype(BF16)
    w_both = wr_ref[...]
    first = jnp.dot(h_hi, w_both, preferred_element_type=F32)
    logits = first[:, :n_e] + first[:, n_e:] + jnp.dot(h_lo, w_both[:, :n_e], preferred_element_type=F32)
    s = jax.nn.sigmoid(logits)
    sel = s + br_ref[...]
    tm, n_e = sel.shape
    lane = lax.broadcasted_iota(jnp.int32, sel.shape, 1).astype(F32)
    picked = jnp.zeros(sel.shape, F32)
    firsts = []
    for _ in range(TOP_K):
        cur = jnp.where(picked > 0.0, -jnp.inf, sel)
        m = jnp.max(cur, axis=-1, keepdims=True)
        first = jnp.min(jnp.where(cur == m, lane, float(n_e)), axis=-1, keepdims=True)
        picked = jnp.where(lane == first, 1.0, picked)
        firsts.append(first)
    w = picked * s
    gate = w / jnp.sum(w, axis=-1, keepdims=True) * ROUTE_SCALE

    earlier = (lax.broadcasted_iota(jnp.int32, (tm, tm), 1) < lax.broadcasted_iota(jnp.int32, (tm, tm), 0))
    cum = jnp.dot(earlier.astype(BF16), picked.astype(BF16), preferred_element_type=F32) + carry_ref[...]
    idx_m = jnp.zeros(sel.shape, F32)
    wt_m = jnp.zeros(sel.shape, F32)
    rank_m = jnp.zeros(sel.shape, F32)
    for k, first in enumerate(firsts):
        mine = lane == first
        idx_m = jnp.where(lane == float(k), first, idx_m)
        wt_m = jnp.where(lane == float(k), jnp.sum(jnp.where(mine, gate, 0.0), axis=-1, keepdims=True), wt_m)
        rank_m = jnp.where(lane == float(k), jnp.sum(jnp.where(mine, cum, 0.0), axis=-1, keepdims=True), rank_m)
    idx_ref[...] = idx_m.astype(jnp.int32)
    wt_ref[...] = wt_m
    rank_ref[...] = rank_m.astype(jnp.int32)
    carry_ref[...] += jnp.sum(picked, axis=0, keepdims=True)
    cnt_ref[...] = carry_ref[...].astype(jnp.int32)


def _norm_router(X, n_rows, g, mods3, tok, l, w_router, b_router):
    D = X.shape[1]
    E = w_router.shape[1]
    w_hi = w_router.astype(BF16)
    w_both = jnp.concatenate([w_hi, (w_router - w_hi.astype(F32)).astype(BF16)], axis=1)
    tm = _tile(n_rows, 256, 16, tok.N)
    row = lambda i: (i, 0)
    fixed = lambda i: (0, 0)
    return _call(
        _norm_router_kernel, (n_rows // tm,),
        [pl.BlockSpec((tm, D), row), pl.BlockSpec((1, D), fixed),
         _mod_spec(tok, l, 3, tm, 0, D, 1), _mod_spec(tok, l, 4, tm, 0, D, 1),
         pl.BlockSpec((D, 2 * E), fixed), pl.BlockSpec((1, E), fixed)],
        [pl.BlockSpec((tm, D), row), pl.BlockSpec((tm, E), row), pl.BlockSpec((tm, E), row),
         pl.BlockSpec((tm, E), row), pl.BlockSpec((1, E), fixed)],
        [jax.ShapeDtypeStruct((n_rows, D), BF16), jax.ShapeDtypeStruct((n_rows, E), jnp.int32),
         jax.ShapeDtypeStruct((n_rows, E), F32), jax.ShapeDtypeStruct((n_rows, E), jnp.int32),
         jax.ShapeDtypeStruct((1, E), jnp.int32)],
        "norm_router", scratch=[pltpu.VMEM((1, E), F32)])(
            X, g.reshape(1, D), mods3, mods3, w_both, b_router.reshape(1, E))


def _mm_kernel(a_ref, b_ref, o_ref):
    o_ref[...] = jnp.dot(a_ref[...], b_ref[...], preferred_element_type=F32).astype(o_ref.dtype)


def _mm(a, b, tm_pref=1024, tn_pref=1024, name="mm"):
    M, K = a.shape
    N = b.shape[1]
    tm = _tile(M, tm_pref, 16)
    tn = _tile(N, tn_pref, LANE)
    return _call(
        _mm_kernel, (M // tm, N // tn),
        [pl.BlockSpec((tm, K), lambda i, j: (i, 0)), pl.BlockSpec((K, tn), lambda i, j: (0, j))],
        pl.BlockSpec((tm, tn), lambda i, j: (i, j)),
        jax.ShapeDtypeStruct((M, N), BF16), name)(a, b)


def _norm_mm_kernel(a_ref, g_ref, b_ref, o_ref):
    a = a_ref[...].astype(F32)
    r = lax.rsqrt(jnp.mean(a * a, axis=-1, keepdims=True) + EPS)
    an = (a * r * g_ref[...]).astype(BF16)
    o_ref[...] = jnp.dot(an, b_ref[...], preferred_element_type=F32).astype(o_ref.dtype)


def _norm_mm(a, col_off, K, g, b, name):
    M = a.shape[0]
    N = b.shape[1]
    tm = _tile(M, 512, 16)
    cb = col_off // K
    assert col_off % K == 0
    return _call(
        _norm_mm_kernel, (M // tm,),
        [pl.BlockSpec((tm, K), lambda i: (i, cb)),
         pl.BlockSpec((1, K), lambda i: (0, 0)),
         pl.BlockSpec((K, N), lambda i: (0, 0))],
        pl.BlockSpec((tm, N), lambda i: (i, 0)),
        jax.ShapeDtypeStruct((M, N), BF16), name)(a, g.reshape(1, K), b)


def _headprep_kernel(*refs, n_src, n_heads, chunks, rope_chunk, inv_d, shift):
    x_refs = refs[:n_src]
    g_ref, cos_ref, sin_ref, o_ref = refs[n_src:]
    n_chunks = len(chunks)
    cos, sin = cos_ref[...], sin_ref[...]
    lane = lax.broadcasted_iota(jnp.int32, cos.shape, 1)
    first_half = (lane % (2 * shift)) < shift
    for h in range(n_heads):
        xs = []
        for src, lane_of in chunks:
            c0 = lane_of(h)
            xs.append(x_refs[src][:, c0:c0 + LANE].astype(F32))
        ssq = sum(jnp.sum(x * x, axis=-1, keepdims=True) for x in xs)
        r = lax.rsqrt(ssq * inv_d + EPS)
        for c, x in enumerate(xs):
            y = x * r * g_ref[:, c * LANE:(c + 1) * LANE]
            if c == rope_chunk:
                swapped = jnp.where(first_half, pltpu.roll(y, LANE - shift, 1), pltpu.roll(y, shift, 1))
                y = y * cos + swapped * sin
            o0 = (h * n_chunks + c) * LANE
            o_ref[:, o0:o0 + LANE] = y.astype(o_ref.dtype)


def _headprep(srcs, chunks, n_rows, n_heads, g, cos, sin, rope_chunk, norm_dim, shift, tok, tok_off, name):
    n_chunks = len(chunks)
    W = n_chunks * LANE
    tm = _tile(n_rows, 256, 16, tok.N)
    assert cos.shape[0] >= tok.N + tm

    def pos_idx(i):
        return (tok.pos_block(i * tm + tok_off, tm), 0)

    in_specs = [pl.BlockSpec((tm, w), (lambda i, cb=cb: (i, cb))) for _, w, cb in srcs]
    in_specs += [pl.BlockSpec((1, W), lambda i: (0, 0)),
                 pl.BlockSpec((tm, LANE), pos_idx), pl.BlockSpec((tm, LANE), pos_idx)]
    kern = functools.partial(_headprep_kernel, n_src=len(srcs), n_heads=n_heads, chunks=tuple(chunks),
                             rope_chunk=rope_chunk, inv_d=1.0 / norm_dim, shift=shift)
    return _call(
        kern, (n_rows // tm,), in_specs,
        pl.BlockSpec((tm, n_heads * W), lambda i: (i, 0)),
        jax.ShapeDtypeStruct((n_rows, n_heads * W), BF16), name)(*[a for a, _, _ in srcs], g.reshape(1, W), cos, sin)


def _rope_tables(n_pos, rope_dim, pad_rows):
    nf = rope_dim // 4
    t = jnp.arange(n_pos, dtype=jnp.int32)
    row = (t // GRID_W).astype(F32)
    col = (t % GRID_W).astype(F32)
    inv = ROPE_BASE ** (-jnp.arange(nf, dtype=F32) / nf)
    a_row, a_col = row[:, None] * inv, col[:, None] * inv
    cos = jnp.concatenate([jnp.cos(a_row)] * 2 + [jnp.cos(a_col)] * 2, axis=-1)
    sin = jnp.concatenate([-jnp.sin(a_row), jnp.sin(a_row), -jnp.sin(a_col), jnp.sin(a_col)], axis=-1)
    cos = jnp.pad(cos, ((0, pad_rows), (0, LANE - rope_dim)), constant_values=1.0)
    sin = jnp.pad(sin, ((0, pad_rows), (0, LANE - rope_dim)))
    return cos, sin


def _attn_kernel(*refs, n_src, tq, n_q):
    q_ref = refs[0]
    k_refs = refs[1:1 + n_src]
    v_refs = refs[1 + n_src:1 + 2 * n_src]
    o_ref = refs[-1]

    for c in range(n_q // tq):
        rows = pl.ds(c * tq, tq)
        q = q_ref[rows, :]
        s = [lax.dot_general(q, k[...], (((1,), (1,)), ((), ())), preferred_element_type=F32) for k in k_refs]
        m = functools.reduce(jnp.maximum, [jnp.max(si, axis=-1, keepdims=True) for si in s])
        p = [jnp.exp(si - m) for si in s]
        l = sum(jnp.sum(pi, axis=-1, keepdims=True) for pi in p)
        o = sum(jnp.dot(pi.astype(BF16), v[...], preferred_element_type=F32) for pi, v in zip(p, v_refs))
        o_ref[rows, :] = (o * (1.0 / l)).astype(o_ref.dtype)


def _attend(q, q_row_off, n_q, k_srcs, v_srcs, B, n_kv_heads, group, dk, dv, name):
    n_src = len(k_srcs)
    tq = _tile(n_q, 512, 16)
    qb = q_row_off // n_q
    assert q_row_off % n_q == 0
    in_specs = [pl.BlockSpec((n_q, dk), lambda b, hk, g: (qb + b, hk * group + g))]
    args = [q]
    for width, srcs in ((dk, k_srcs), (dv, v_srcs)):
        for arr, row_off, n_k, cb0 in srcs:
            assert row_off % n_k == 0
            rb = row_off // n_k
            in_specs.append(pl.BlockSpec((n_k, width), (lambda b, hk, g, rb=rb, cb0=cb0: (rb + b, cb0 + hk))))
            args.append(arr)
    kern = functools.partial(_attn_kernel, n_src=n_src, tq=tq, n_q=n_q)
    return _call(
        kern, (B, n_kv_heads, group), in_specs,
        pl.BlockSpec((n_q, dv), lambda b, hk, g: (b, hk * group + g)),
        jax.ShapeDtypeStruct((B * n_q, n_kv_heads * group * dv), BF16), name)(*args)


def _conv_kernel(b_ref, c_ref, u_ref, w_ref, o_ref):
    v = c_ref[...].astype(F32) * u_ref[...].astype(F32)
    n = v.shape[0]
    row = lax.broadcasted_iota(jnp.int32, v.shape, 0)
    prev = jnp.where(row == 0, 0.0, pltpu.roll(v, 1, 0))
    nxt = jnp.where(row == n - 1, 0.0, pltpu.roll(v, n - 1, 0))
    y = prev * w_ref[0:1, :] + v * w_ref[1:2, :] + nxt * w_ref[2:3, :]
    o_ref[...] = (b_ref[...].astype(F32) * y).astype(o_ref.dtype)


def _conv(px, row_off, n_seq, seq_len, col_b, col_c, col_u, CW, conv_w):
    tc = _tile(CW, 512, LANE, col_b, col_c, col_u)
    rb = row_off // seq_len
    assert row_off % seq_len == 0

    def spec(col):
        return pl.BlockSpec((seq_len, tc), lambda s, j, col=col: (rb + s, col // tc + j))

    return _call(
        _conv_kernel, (n_seq, CW // tc),
        [spec(col_b), spec(col_c), spec(col_u), pl.BlockSpec((3, tc), lambda s, j: (0, j))],
        pl.BlockSpec((seq_len, tc), lambda s, j: (s, j)),
        jax.ShapeDtypeStruct((n_seq * seq_len, CW), BF16), "short_conv")(px, px, px, conv_w)


def _merge_kernel(oa_ref, ob_ref, oc_ref, wa_ref, wb_ref, wc_ref, ga_ref, gb_ref, gc_ref, o_ref):
    def term(o, w, g):
        return jax.nn.sigmoid(g[...].astype(F32)) * jnp.dot(o[...], w[...], preferred_element_type=F32)
    h = term(oa_ref, wa_ref, ga_ref) + term(ob_ref, wb_ref, gb_ref) + term(oc_ref, wc_ref, gc_ref)
    o_ref[...] = h.astype(o_ref.dtype)


def _merge(oa, ob, oc, w_pa, w_pb, w_pc, px, gate_col, n_rows):
    D = w_pa.shape[1]
    tm = _tile(n_rows, 1024, 16)
    tn = _tile(D, 512, LANE, gate_col)
    gb0 = gate_col // tn
    nb = D // tn

    def a_spec(o):
        return pl.BlockSpec((tm, o.shape[1]), lambda i, j: (i, 0))

    def w_spec(w):
        return pl.BlockSpec((w.shape[0], tn), lambda i, j: (0, j))

    def g_spec(k):
        return pl.BlockSpec((tm, tn), lambda i, j, k=k: (i, gb0 + k * nb + j))

    return _call(
        _merge_kernel, (n_rows // tm, nb),
        [a_spec(oa), a_spec(ob), a_spec(oc), w_spec(w_pa), w_spec(w_pb), w_spec(w_pc),
         g_spec(0), g_spec(1), g_spec(2)],
        pl.BlockSpec((tm, tn), lambda i, j: (i, j)),
        jax.ShapeDtypeStruct((n_rows, D), BF16), "merge")(oa, ob, oc, w_pa, w_pb, w_pc, px, px, px)


def _mm_res_kernel(a_ref, b_ref, *refs, n_a):
    x_refs, (gt_ref, o_ref) = refs[:-2], refs[-2:]
    x = x_refs[0][...]
    if len(x_refs) == 2:
        x = jnp.where(pl.program_id(0) < n_a, x, x_refs[1][...])
    acc = jnp.dot(a_ref[...], b_ref[...], preferred_element_type=F32)
    o_ref[...] = x + gt_ref[...] * acc


def _mm_res(a, b, X, mods3, tok, l, which, n_rows, Xb=None):
    K = a.shape[1]
    D = b.shape[1]
    tm = _tile(n_rows, 1024, 16, tok.N)
    tn = _tile(D, 512, LANE)

    def gt_idx(i, j):
        row = tok.mod_row(i * tm)
        return ((l * MOD_ROWS + row) * N_MOD + which, 0, j)

    if Xb is None:
        x_specs, x_args, na = [pl.BlockSpec((tm, tn), lambda i, j: (i, j))], [X], 0
    else:
        assert X.shape[0] % tm == 0
        x_specs, x_args, na = _two_source_specs(X, Xb, (tm, tn), tm, lambda j: j)
    return _call(
        functools.partial(_mm_res_kernel, n_a=na), (n_rows // tm, D // tn),
        [pl.BlockSpec((tm, K), lambda i, j: (i, 0)), pl.BlockSpec((K, tn), lambda i, j: (0, j))]
        + x_specs + [pl.BlockSpec((None, 1, tn), gt_idx)],
        pl.BlockSpec((tm, tn), lambda i, j: (i, j)),
        jax.ShapeDtypeStruct((n_rows, D), F32), "proj_residual")(a, b, *x_args, mods3)


def _swiglu_up_kernel(h_ref, w_ref, o_ref):
    gu = jnp.dot(h_ref[...], w_ref[...], preferred_element_type=F32)
    f = o_ref.shape[1]
    o_ref[...] = (_silu(gu[:, :f]) * gu[:, f:]).astype(o_ref.dtype)


def _swiglu_up(h, w_gu):
    M, D = h.shape
    f = w_gu.shape[1] // 2
    tm = _tile(M, 1024, 16)
    return _call(
        _swiglu_up_kernel, (M // tm,),
        [pl.BlockSpec((tm, D), lambda i: (i, 0)), pl.BlockSpec((D, 2 * f), lambda i: (0, 0))],
        pl.BlockSpec((tm, f), lambda i: (i, 0)),
        jax.ShapeDtypeStruct((M, f), BF16), "shared_up")(h, w_gu)


SLAB_ROWS = 16
STAGE_PITCH = 24
EXPERT_TILE = 256
HI_MASK = -65536


def _pack_pairs(v):
    half = v.shape[1] // 2
    bits = lax.bitcast_convert_type(v.astype(BF16).astype(F32), jnp.int32)
    return lax.shift_right_logical(bits[:, :half], 16) | (bits[:, half:] & HI_MASK)


def _unpack_lo(w):
    return lax.bitcast_convert_type(lax.shift_left(w, 16), F32)


def _unpack_hi(w):
    return lax.bitcast_convert_type(w & HI_MASK, F32)


def _dispatch_kernel(pos_ref, zt_ref, h_ref, xs_hbm, slab_ref, zero_ref, sem, *, tm, n_experts):
    W = slab_ref.shape[1]
    trows = EXPERT_TILE * SLAB_ROWS

    @pl.when(pl.program_id(0) == 0)
    def _():
        zero_ref[...] = jnp.zeros_like(zero_ref)
        def zcopy(t):
            return pltpu.make_async_copy(zero_ref, xs_hbm.at[pl.ds(pl.multiple_of(t * trows, trows), trows), :], sem)
        for e in range(n_experts):
            zcopy(zt_ref[0, e]).start()
        for e in range(n_experts):
            zcopy(zt_ref[0, e]).wait()
        n_used = zt_ref[0, n_experts]
        n_tiles = xs_hbm.shape[0] // trows

        def zstart(t, carry):
            zcopy(t).start()
            return carry

        def zwait(t, carry):
            zcopy(t).wait()
            return carry

        lax.fori_loop(n_used, n_tiles, zstart, 0)
        lax.fori_loop(n_used, n_tiles, zwait, 0)

    words = _pack_pairs(h_ref[...])
    for s in range(SLAB_ROWS):
        slab_ref[pl.ds(s, tm, stride=STAGE_PITCH), :] = words[:, s * W:(s + 1) * W]

    def row_copy(r, k):
        p = pos_ref[0, r * TOP_K + k]
        return pltpu.make_async_copy(
            slab_ref.at[pl.ds(pl.multiple_of(r * STAGE_PITCH, 8), SLAB_ROWS), :],
            xs_hbm.at[pl.ds(pl.multiple_of(p * SLAB_ROWS, SLAB_ROWS), SLAB_ROWS), :], sem)

    def issue(r, carry):
        for k in range(TOP_K):
            row_copy(r, k).start()
        return carry

    lax.fori_loop(0, tm, issue, 0)
    for k in range(TOP_K):
        pltpu.make_async_copy(slab_ref.at[pl.ds(0, tm * SLAB_ROWS), :],
                              xs_hbm.at[pl.ds(0, tm * SLAB_ROWS), :], sem).wait()


def _dispatch(h2, pos, zero_tiles, n_slots):
    M, D = h2.shape
    W = D // (2 * SLAB_ROWS)
    assert W == LANE, "strided slab access needs LANE-wide slab rows"
    E = zero_tiles.shape[0] - 1
    tm = _tile(M, 256, 16)
    kern = functools.partial(_dispatch_kernel, tm=tm, n_experts=E)
    return _call(
        kern, (M // tm,),
        [pl.BlockSpec((None, 1, tm * TOP_K), lambda i: (i, 0, 0), memory_space=pltpu.SMEM),
         pl.BlockSpec((1, E + 1), lambda i: (0, 0), memory_space=pltpu.SMEM),
         pl.BlockSpec((tm, D), lambda i: (i, 0))],
        pl.BlockSpec(memory_space=pl.ANY),
        jax.ShapeDtypeStruct((n_slots * SLAB_ROWS, W), jnp.int32), "moe_dispatch",
        scratch=[pltpu.VMEM((tm * STAGE_PITCH, W), jnp.int32),
                 pltpu.VMEM((EXPERT_TILE * SLAB_ROWS, W), jnp.int32),
                 pltpu.SemaphoreType.DMA(())])(
            pos.reshape(M // tm, 1, tm * TOP_K), zero_tiles.reshape(1, E + 1), h2)


def _expert_kernel(te_ref, nu_ref, x_ref, wgu_ref, wdn_ref, o_ref):
    W = x_ref.shape[1]
    f = wdn_ref.shape[0]

    @pl.when(pl.program_id(0) < nu_ref[0])
    def _():
        words = [x_ref[pl.ds(s, EXPERT_TILE, stride=SLAB_ROWS), :] for s in range(SLAB_ROWS)]
        x = jnp.concatenate([_unpack_lo(w).astype(BF16) for w in words]
                            + [_unpack_hi(w).astype(BF16) for w in words], axis=1)
        gu = jnp.dot(x, wgu_ref[...], preferred_element_type=F32)
        act = (_silu(gu[:, :f]) * gu[:, f:]).astype(BF16)
        y = _pack_pairs(jnp.dot(act, wdn_ref[...], preferred_element_type=F32))
        for s in range(SLAB_ROWS):
            o_ref[pl.ds(s, EXPERT_TILE, stride=SLAB_ROWS), :] = y[:, s * W:(s + 1) * W]

    @pl.when(pl.program_id(0) >= nu_ref[0])
    def _():
        o_ref[...] = jnp.zeros_like(o_ref)


def _experts(xs, tile_expert, n_used, w_gu, w_down):
    E, D, F2 = w_gu.shape
    W = xs.shape[1]
    trows = EXPERT_TILE * SLAB_ROWS
    n_tiles = xs.shape[0] // trows
    grid_spec = pltpu.PrefetchScalarGridSpec(
        num_scalar_prefetch=2, grid=(n_tiles,),
        in_specs=[pl.BlockSpec((trows, W), lambda i, te, nu: (jnp.minimum(i, nu[0] - 1), 0)),
                  pl.BlockSpec((None, D, F2), lambda i, te, nu: (te[i], 0, 0)),
                  pl.BlockSpec((None, F2 // 2, D), lambda i, te, nu: (te[i], 0, 0))],
        out_specs=pl.BlockSpec((trows, W), lambda i, te, nu: (i, 0)))
    return pl.pallas_call(
        _expert_kernel, grid_spec=grid_spec, out_shape=jax.ShapeDtypeStruct(xs.shape, jnp.int32),
        name="moe_experts",
        compiler_params=pltpu.CompilerParams(dimension_semantics=("arbitrary",),
                                             vmem_limit_bytes=VMEM_LIMIT_BYTES))(
            tile_expert, n_used, xs, w_gu, w_down)


def _combine_kernel(pos_ref, posn_ref, wt_ref, sh_ref, x_ref, gt_ref, ys_hbm, o_ref, buf_a, buf_b, sem, *, tm):
    j = pl.program_id(0)
    W = buf_a.shape[1]
    half = SLAB_ROWS * W

    def issue(pref, first_tok, buf, s_idx):
        def body(r, carry):
            for k in range(TOP_K):
                p = pref[0, (first_tok + r) * TOP_K + k]
                pltpu.make_async_copy(
                    ys_hbm.at[pl.ds(pl.multiple_of(p * SLAB_ROWS, SLAB_ROWS), SLAB_ROWS), :],
                    buf.at[pl.ds(pl.multiple_of((k * tm + r) * STAGE_PITCH, 8), SLAB_ROWS), :],
                    sem.at[s_idx]).start()
            return carry
        lax.fori_loop(0, tm, body, 0)

    def drain(buf, s_idx):
        for k in range(TOP_K):
            pltpu.make_async_copy(ys_hbm.at[pl.ds(0, tm * SLAB_ROWS), :],
                                  buf.at[pl.ds(0, tm * SLAB_ROWS), :], sem.at[s_idx]).wait()

    def reduce(buf, r0):
        rows = pl.ds(r0, tm)
        wt = wt_ref[rows, :]
        wk = [wt[:, k:k + 1] for k in range(TOP_K)]
        for s in range(SLAB_ROWS):
            lo = hi = None
            for k in range(TOP_K):
                w = buf[pl.ds(k * tm * STAGE_PITCH + s, tm, stride=STAGE_PITCH), :]
                lo_k, hi_k = wk[k] * _unpack_lo(w), wk[k] * _unpack_hi(w)
                lo, hi = (lo_k, hi_k) if lo is None else (lo + lo_k, hi + hi_k)
            for c0, routed in ((s * W, lo), (half + s * W, hi)):
                cols = slice(c0, c0 + W)
                o_ref[rows, cols] = x_ref[rows, cols] + gt_ref[:, cols] * (sh_ref[rows, cols].astype(F32) + routed)

    @pl.when(j == 0)
    def _():
        issue(pos_ref, 0, buf_a, 0)

    issue(pos_ref, tm, buf_b, 1)
    drain(buf_a, 0)
    reduce(buf_a, 0)

    @pl.when(j + 1 < pl.num_programs(0))
    def _():
        issue(posn_ref, 0, buf_a, 0)

    drain(buf_b, 1)
    reduce(buf_b, tm)


def _combine(ys, pos, wt, sh, X, mods3, tok, l, which, n_rows):
    D = X.shape[1]
    W = ys.shape[1]
    E = wt.shape[1]
    tm = _tile(n_rows // 2, 128, 16, tok.N // 2)
    tb = 2 * tm
    n_steps = n_rows // tb
    pos3 = pos.reshape(n_steps, 1, tb * TOP_K)

    def gt_idx(j):
        row = tok.mod_row(j * tb)
        return ((l * MOD_ROWS + row) * N_MOD + which, 0, 0)

    kern = functools.partial(_combine_kernel, tm=tm)
    return _call(
        kern, (n_steps,),
        [pl.BlockSpec((None, 1, tb * TOP_K), lambda j: (j, 0, 0), memory_space=pltpu.SMEM),
         pl.BlockSpec((None, 1, tb * TOP_K), lambda j: (jnp.minimum(j + 1, n_steps - 1), 0, 0),
                      memory_space=pltpu.SMEM),
         pl.BlockSpec((tb, E), lambda j: (j, 0)),
         pl.BlockSpec((tb, D), lambda j: (j, 0)),
         pl.BlockSpec((tb, D), lambda j: (j, 0)),
         pl.BlockSpec((None, 1, D), gt_idx),
         pl.BlockSpec(memory_space=pl.ANY)],
        pl.BlockSpec((tb, D), lambda j: (j, 0)),
        jax.ShapeDtypeStruct((n_rows, D), F32), "moe_combine",
        scratch=[pltpu.VMEM((TOP_K * tm * STAGE_PITCH, W), jnp.int32),
                 pltpu.VMEM((TOP_K * tm * STAGE_PITCH, W), jnp.int32),
                 pltpu.SemaphoreType.DMA((2,))])(pos3, pos3, wt, sh, X, mods3, ys)


def _routing_plan(idx, rank, counts, n_tiles):
    E = counts.shape[0]
    padded = (counts + EXPERT_TILE - 1) // EXPERT_TILE * EXPERT_TILE
    ends = jnp.cumsum(padded)
    base = ends - padded
    experts = jnp.arange(E, dtype=jnp.int32)
    pos = jnp.sum(jnp.where(idx[..., None] == experts, base, 0), axis=-1) + rank
    tile_start = jnp.arange(n_tiles, dtype=jnp.int32) * EXPERT_TILE
    tile_expert = jnp.minimum(jnp.sum((ends[None, :] <= tile_start[:, None]).astype(jnp.int32), axis=1), E - 1)
    n_used = (ends[-1] // EXPERT_TILE).reshape(1)
    last_tile = jnp.maximum(ends // EXPERT_TILE - 1, 0)
    return pos.astype(jnp.int32), tile_expert.astype(jnp.int32), n_used.astype(jnp.int32), last_tile.astype(jnp.int32)


def kernel(x, c, ctx, c_ctx, w_mod, b_mod, g_mix, g_ffn, w_in, mla_q_norm_g, mla_kv_norm_g, w_uq, w_ukv,
           mla_q_head_g, mla_k_head_g, gqa_q_g, gqa_k_g, conv_w, w_pa, w_pb, w_pc, w_o, w_router, b_router,
           w_exp_gu, w_exp_down, w_sh_gu, w_sh_down):
    B, N, D = x.shape
    CTX = ctx.shape[1]
    L = w_mod.shape[0]
    QL = mla_q_norm_g.shape[1]
    KVL = mla_kv_norm_g.shape[1]
    HA = w_uq.shape[2] // MLA_QK
    HG = w_pc.shape[1] // GQA_HD
    CW = conv_w.shape[2]
    D_IN = w_in.shape[2]
    HKV = (D_IN - (KVL + MLA_ROPE + QL + HG * GQA_HD + 3 * CW + 3 * D)) // (2 * GQA_HD)
    GKV = HKV * GQA_HD
    E = w_router.shape[2]

    tok = _Tok(B, N, CTX)
    n_lat, n_ctx = tok.n_lat, tok.n_ctx

    KV_COLS = KVL + 2 * GKV + LANE
    col_gk, col_gv, col_kr = KVL, KVL + GKV, KVL + 2 * GKV
    main0 = KVL + MLA_ROPE + 2 * GKV
    col_cq, col_gq = 0, QL
    col_b, col_c, col_u = QL + HG * GQA_HD, QL + HG * GQA_HD + CW, QL + HG * GQA_HD + 2 * CW
    col_gate = col_u + CW

    cvec = jnp.zeros((MOD_ROWS, D), F32).at[:B].set(c).at[B].set(c_ctx)
    mods = _mods(cvec, w_mod, b_mod)
    mods3 = mods.reshape(L * MOD_ROWS * N_MOD, 1, D)

    pad_rows = _tile(n_lat + n_ctx, 512, 16, N)
    cos_a, sin_a = _rope_tables(N, MLA_ROPE, pad_rows)
    cos_g, sin_g = _rope_tables(N, GQA_HD, pad_rows)

    X, Xb = x.reshape(n_lat, D), ctx.reshape(n_ctx, D)

    for l in range(L):
        last = l == L - 1
        n_rows = n_lat if last else n_lat + n_ctx

        wl = w_in[l]
        w_main = wl[:, main0:].astype(BF16)
        w_kv = jnp.concatenate(
            [wl[:, :KVL], wl[:, KVL + MLA_ROPE:main0], wl[:, KVL:KVL + MLA_ROPE],
             jnp.zeros((D, LANE - MLA_ROPE), F32)], axis=1).astype(BF16)
        w_uq_p = jnp.pad(w_uq[l].reshape(QL, HA, MLA_QK),
                         ((0, 0), (0, 0), (0, MLA_QK_PAD - MLA_QK))).reshape(QL, HA * MLA_QK_PAD).astype(BF16)
        w_ukv_r = w_ukv[l].reshape(KVL, HA, MLA_NOPE + MLA_V)
        w_ukv_p = jnp.concatenate([w_ukv_r[:, :, :MLA_NOPE].reshape(KVL, HA * MLA_NOPE),
                                   w_ukv_r[:, :, MLA_NOPE:].reshape(KVL, HA * MLA_V)], axis=1).astype(BF16)
        g_qa = jnp.pad(mla_q_head_g[l] * (1.0 / math.sqrt(MLA_QK)), (0, MLA_QK_PAD - MLA_QK))
        g_ka = jnp.pad(mla_k_head_g[l], (0, MLA_QK_PAD - MLA_QK))
        g_qg = gqa_q_g[l] * (1.0 / math.sqrt(GQA_HD))
        g_kg = gqa_k_g[l]

        h = _norm_mod(X, 0, n_rows, g_mix[l], mods3, tok, l, 0, 0, Xb=None if last else Xb)
        px = _mm(h, w_main, name="in_proj")
        if last:
            c_src, c_off = (X, n_lat) if Xb is None else (Xb, 0)
            hc = _norm_mod(c_src, c_off, n_ctx, g_mix[l], mods3, tok, l, 0, n_lat)
            pkv_x = _mm(h, w_kv, tn_pref=KV_COLS, name="in_proj_kv")
            pkv_c = _mm(hc, w_kv, tn_pref=KV_COLS, name="in_proj_kv")
            kv_parts = [(pkv_x, n_lat, 0), (pkv_c, n_ctx, n_lat)]
        else:
            pkv = _mm(h, w_kv, tn_pref=KV_COLS, name="in_proj_kv")
            kv_parts = [(pkv, n_rows, 0)]

        ka, kvr, kg, pk = [], [], [], []
        for pkv_p, rows_p, tok_off in kv_parts:
            kv_raw = _norm_mm(pkv_p, 0, KVL, mla_kv_norm_g[l], w_ukv_p, "mla_kv_up")
            ka.append(_headprep(
                [(kv_raw, HA * MLA_NOPE, 0), (pkv_p, LANE, col_kr // LANE)],
                [(0, lambda hh: hh * MLA_NOPE), (1, lambda hh: 0)], rows_p, HA, g_ka, cos_a, sin_a,
                1, MLA_QK, MLA_ROPE // 4, tok, tok_off, "mla_k_prep"))
            assert col_gk % GKV == 0
            kg.append(_headprep(
                [(pkv_p, GKV, col_gk // GKV)], [(0, lambda hh: hh * GQA_HD)], rows_p, HKV, g_kg, cos_g, sin_g,
                0, GQA_HD, GQA_HD // 4, tok, tok_off, "gqa_k_prep"))
            kvr.append(kv_raw)
            pk.append(pkv_p)

        q_raw = _norm_mm(px, col_cq, QL, mla_q_norm_g[l], w_uq_p, "mla_q_up")
        qa = _headprep([(q_raw, HA * MLA_QK_PAD, 0)],
                       [(0, lambda hh: hh * MLA_QK_PAD), (0, lambda hh: hh * MLA_QK_PAD + LANE)], n_rows, HA, g_qa,
                       cos_a, sin_a, 1, MLA_QK, MLA_ROPE // 4, tok, 0, "mla_q_prep")
        assert col_gq % (HG * GQA_HD) == 0
        qg = _headprep([(px, HG * GQA_HD, col_gq // (HG * GQA_HD))], [(0, lambda hh: hh * GQA_HD)], n_rows, HG,
                       g_qg, cos_g, sin_g, 0, GQA_HD, GQA_HD // 4, tok, 0, "gqa_q_prep")

        ctx_i = len(kv_parts) - 1
        ctx_off = 0 if last else n_lat
        vA0 = HA * MLA_NOPE // MLA_V
        ka_x, ka_c = (ka[0], 0, N, 0), (ka[ctx_i], ctx_off, CTX, 0)
        va_x, va_c = (kvr[0], 0, N, vA0), (kvr[ctx_i], ctx_off, CTX, vA0)
        kg_x, kg_c = (kg[0], 0, N, 0), (kg[ctx_i], ctx_off, CTX, 0)
        vg_x, vg_c = (pk[0], 0, N, col_gv // GQA_HD), (pk[ctx_i], ctx_off, CTX, col_gv // GQA_HD)

        oa = _attend(qa, 0, N, [ka_c, ka_x], [va_c, va_x], B, HA, 1, MLA_QK_PAD, MLA_V, "mla_attn")
        og = _attend(qg, 0, N, [kg_c, kg_x], [vg_c, vg_x], B, HKV, HG // HKV, GQA_HD, GQA_HD, "gqa_attn")
        ob = _conv(px, 0, B, N, col_b, col_c, col_u, CW, conv_w[l])
        if not last:
            oa_c = _attend(qa, n_lat, CTX, [ka_c], [va_c], B, HA, 1, MLA_QK_PAD, MLA_V, "mla_attn_ctx")
            og_c = _attend(qg, n_lat, CTX, [kg_c], [vg_c], B, HKV, HG // HKV, GQA_HD, GQA_HD, "gqa_attn_ctx")
            ob_c = _conv(px, n_lat, B, CTX, col_b, col_c, col_u, CW, conv_w[l])
            oa = jnp.concatenate([oa, oa_c], axis=0)
            og = jnp.concatenate([og, og_c], axis=0)
            ob = jnp.concatenate([ob, ob_c], axis=0)

        hm = _merge(oa, ob, og, w_pa[l].astype(BF16), w_pb[l].astype(BF16), w_pc[l].astype(BF16),
                    px, col_gate, n_rows)
        X = _mm_res(hm, w_o[l].astype(BF16), X, mods3, tok, l, 2, n_rows, Xb=None if last else Xb)
        Xb = None

        h2, idx_m, wt_m, rank_m, cnt = _norm_router(X, n_rows, g_ffn[l], mods3, tok, l, w_router[l], b_router[l])
        n_tiles = n_rows * TOP_K // EXPERT_TILE + E
        pos, tile_expert, n_used, last_tile = _routing_plan(idx_m[:, :TOP_K], rank_m[:, :TOP_K], cnt[0], n_tiles)
        xs = _dispatch(h2, pos, jnp.concatenate([last_tile, n_used]), n_tiles * EXPERT_TILE)
        ys = _experts(xs, tile_expert, n_used, w_exp_gu[l].astype(BF16), w_exp_down[l].astype(BF16))
        sh = _mm(_swiglu_up(h2, w_sh_gu[l].astype(BF16)), w_sh_down[l].astype(BF16), name="shared_down")
        X = _combine(ys, pos, wt_m, sh, X, mods3, tok, l, 5, n_rows)

    return X.reshape(B, N, D)
```

```python
import functools
import math

import jax
import jax.numpy as jnp
from jax import lax
from jax.experimental import pallas as pl
from jax.experimental.pallas import tpu as pltpu

F32 = jnp.float32
BF16 = jnp.bfloat16

GRID_W = 64
ROPE_BASE = 10000.0
EPS = 1e-6
MLA_NOPE = 128
MLA_ROPE = 64
MLA_QK = MLA_NOPE + MLA_ROPE
MLA_V = 128
GQA_HD = 128
TOP_K = 4
ROUTE_SCALE = 2.5
N_MOD = 6

LANE = 128
MLA_QK_PAD = 2 * LANE
MOD_ROWS = 16
VMEM_LIMIT_BYTES = 56 * 1024 * 1024


def _tile(dim, pref, unit, *also):
    t = min(pref, dim) // unit * unit
    while t > unit:
        if dim % t == 0 and all(a % t == 0 for a in also):
            return t
        t -= unit
    assert dim % unit == 0 and all(a % unit == 0 for a in also), (dim, unit, also)
    return unit


def _call(kernel, grid, in_specs, out_specs, out_shape, name, scratch=()):
    return pl.pallas_call(
        kernel, grid=grid, in_specs=in_specs, out_specs=out_specs, out_shape=out_shape,
        scratch_shapes=list(scratch), name=name,
        compiler_params=pltpu.CompilerParams(
            dimension_semantics=("arbitrary",) * len(grid), vmem_limit_bytes=VMEM_LIMIT_BYTES))


def _silu(v):
    return v * jax.nn.sigmoid(v)


def _mod_kernel(c_ref, w_ref, b_ref, o_ref):
    s = _silu(c_ref[...]).astype(BF16)
    o_ref[0] = jnp.dot(s, w_ref[0].astype(BF16), preferred_element_type=F32) + b_ref[0]


def _mods(cvec, w_mod, b_mod):
    L, D, N6 = w_mod.shape
    tn = _tile(N6, 512, LANE)
    return _call(
        _mod_kernel, (L, N6 // tn),
        [pl.BlockSpec((MOD_ROWS, D), lambda l, j: (0, 0)),
         pl.BlockSpec((1, D, tn), lambda l, j: (l, 0, j)),
         pl.BlockSpec((1, 1, tn), lambda l, j: (l, 0, j))],
        pl.BlockSpec((1, MOD_ROWS, tn), lambda l, j: (l, 0, j)),
        jax.ShapeDtypeStruct((L, MOD_ROWS, N6), F32), "adaln_mod")(cvec, w_mod, b_mod.reshape(L, 1, N6))


class _Tok:
    def __init__(self, B, N, CTX):
        self.B, self.N, self.CTX = B, N, CTX
        self.n_lat = B * N
        self.n_ctx = B * CTX

    def mod_row(self, tok0):
        return jnp.where(tok0 < self.n_lat, tok0 // self.N, self.B)

    def pos_block(self, tok0, tm):
        return jnp.where(tok0 < self.n_lat, (tok0 % self.N) // tm, self.N // tm)


def _mod_spec(tok, l, which, tm, tok_off, D, ngrid):
    def idx(i, *_):
        row = tok.mod_row(i * tm + tok_off)
        return ((l * MOD_ROWS + row) * N_MOD + which, 0, 0)
    return pl.BlockSpec((None, 1, D), idx)


def _two_source_specs(X, Xb, block, row_tile, col_of):
    na = X.shape[0] // row_tile
    return ([pl.BlockSpec(block, lambda i, *r: (jnp.minimum(i, na - 1), col_of(*r))),
             pl.BlockSpec(block, lambda i, *r: (jnp.maximum(i - na, 0), col_of(*r)))], [X, Xb], na)


def _norm_mod_kernel(*refs, n_a):
    x_refs, (g_ref, sh_ref, sc_ref, o_ref) = refs[:-4], refs[-4:]
    x = x_refs[0][...]
    if len(x_refs) == 2:
        x = jnp.where(pl.program_id(0) < n_a, x, x_refs[1][...])
    r = lax.rsqrt(jnp.mean(x * x, axis=-1, keepdims=True) + EPS)
    o_ref[...] = ((x * r * g_ref[...]) * (1.0 + sc_ref[...]) + sh_ref[...]).astype(o_ref.dtype)


def _norm_mod(X, row_off, n_rows, g, mods3, tok, l, which_shift, tok_off, Xb=None):
    D = X.shape[1]
    tm = _tile(n_rows, 256, 16, tok.N, row_off) if row_off else _tile(n_rows, 256, 16, tok.N)
    ro = row_off // tm
    if Xb is None:
        x_specs, x_args, na = [pl.BlockSpec((tm, D), lambda i: (i + ro, 0))], [X], 0
    else:
        assert row_off == 0 and X.shape[0] % tm == 0
        x_specs, x_args, na = _two_source_specs(X, Xb, (tm, D), tm, lambda: 0)
    return _call(
        functools.partial(_norm_mod_kernel, n_a=na), (n_rows // tm,),
        x_specs + [pl.BlockSpec((1, D), lambda i: (0, 0)),
                   _mod_spec(tok, l, which_shift, tm, tok_off, D, 1),
                   _mod_spec(tok, l, which_shift + 1, tm, tok_off, D, 1)],
        pl.BlockSpec((tm, D), lambda i: (i, 0)),
        jax.ShapeDtypeStruct((n_rows, D), BF16), "norm_mod")(*x_args, g.reshape(1, D), mods3, mods3)


def _norm_router_kernel(x_ref, g_ref, sh_ref, sc_ref, wr_ref, br_ref,
                        h_ref, idx_ref, wt_ref, rank_ref, cnt_ref, carry_ref):
    @pl.when(pl.program_id(0) == 0)
    def _():
        carry_ref[...] = jnp.zeros_like(carry_ref)

    x = x_ref[...]
    r = lax.rsqrt(jnp.mean(x * x, axis=-1, keepdims=True) + EPS)
    h = (x * r * g_ref[...]) * (1.0 + sc_ref[...]) + sh_ref[...]
    h_ref[...] = h.astype(h_ref.dtype)
    n_e = br_ref.shape[1]
    h_hi = h.astype(BF16)
    h_lo = (h - h_hi.astype(F32)).astype(BF16)
    w_both = wr_ref[...]
    first = jnp.dot(h_hi, w_both, preferred_element_type=F32)
    logits = first[:, :n_e] + first[:, n_e:] + jnp.dot(h_lo, w_both[:, :n_e], preferred_element_type=F32)
    s = jax.nn.sigmoid(logits)
    sel = s + br_ref[...]
    tm, n_e = sel.shape
    lane = lax.broadcasted_iota(jnp.int32, sel.shape, 1).astype(F32)
    picked = jnp.zeros(sel.shape, F32)
    firsts = []
    for _ in range(TOP_K):
        cur = jnp.where(picked > 0.0, -jnp.inf, sel)
        m = jnp.max(cur, axis=-1, keepdims=True)
        first = jnp.min(jnp.where(cur == m, lane, float(n_e)), axis=-1, keepdims=True)
        picked = jnp.where(lane == first, 1.0, picked)
        firsts.append(first)
    w = picked * s
    gate = w / jnp.sum(w, axis=-1, keepdims=True) * ROUTE_SCALE

    earlier = (lax.broadcasted_iota(jnp.int32, (tm, tm), 1) < lax.broadcasted_iota(jnp.int32, (tm, tm), 0))
    cum = jnp.dot(earlier.astype(BF16), picked.astype(BF16), preferred_element_type=F32) + carry_ref[...]
    idx_m = jnp.zeros(sel.shape, F32)
    wt_m = jnp.zeros(sel.shape, F32)
    rank_m = jnp.zeros(sel.shape, F32)
    for k, first in enumerate(firsts):
        mine = lane == first
        idx_m = jnp.where(lane == float(k), first, idx_m)
        wt_m = jnp.where(lane == float(k), jnp.sum(jnp.where(mine, gate, 0.0), axis=-1, keepdims=True), wt_m)
        rank_m = jnp.where(lane == float(k), jnp.sum(jnp.where(mine, cum, 0.0), axis=-1, keepdims=True), rank_m)
    idx_ref[...] = idx_m.astype(jnp.int32)
    wt_ref[...] = wt_m
    rank_ref[...] = rank_m.astype(jnp.int32)
    carry_ref[...] += jnp.sum(picked, axis=0, keepdims=True)
    cnt_ref[...] = carry_ref[...].astype(jnp.int32)


def _norm_router(X, n_rows, g, mods3, tok, l, w_router, b_router):
    D = X.shape[1]
    E = w_router.shape[1]
    w_hi = w_router.astype(BF16)
    w_both = jnp.concatenate([w_hi, (w_router - w_hi.astype(F32)).astype(BF16)], axis=1)
    tm = _tile(n_rows, 256, 16, tok.N)
    row = lambda i: (i, 0)
    fixed = lambda i: (0, 0)
    return _call(
        _norm_router_kernel, (n_rows // tm,),
        [pl.BlockSpec((tm, D), row), pl.BlockSpec((1, D), fixed),
         _mod_spec(tok, l, 3, tm, 0, D, 1), _mod_spec(tok, l, 4, tm, 0, D, 1),
         pl.BlockSpec((D, 2 * E), fixed), pl.BlockSpec((1, E), fixed)],
        [pl.BlockSpec((tm, D), row), pl.BlockSpec((tm, E), row), pl.BlockSpec((tm, E), row),
         pl.BlockSpec((tm, E), row), pl.BlockSpec((1, E), fixed)],
        [jax.ShapeDtypeStruct((n_rows, D), BF16), jax.ShapeDtypeStruct((n_rows, E), jnp.int32),
         jax.ShapeDtypeStruct((n_rows, E), F32), jax.ShapeDtypeStruct((n_rows, E), jnp.int32),
         jax.ShapeDtypeStruct((1, E), jnp.int32)],
        "norm_router", scratch=[pltpu.VMEM((1, E), F32)])(
            X, g.reshape(1, D), mods3, mods3, w_both, b_router.reshape(1, E))


def _mm_kernel(a_ref, b_ref, o_ref):
    o_ref[...] = jnp.dot(a_ref[...], b_ref[...], preferred_element_type=F32).astype(o_ref.dtype)


def _mm(a, b, tm_pref=1024, tn_pref=1024, name="mm"):
    M, K = a.shape
    N = b.shape[1]
    tm = _tile(M, tm_pref, 16)
    tn = _tile(N, tn_pref, LANE)
    return _call(
        _mm_kernel, (M // tm, N // tn),
        [pl.BlockSpec((tm, K), lambda i, j: (i, 0)), pl.BlockSpec((K, tn), lambda i, j: (0, j))],
        pl.BlockSpec((tm, tn), lambda i, j: (i, j)),
        jax.ShapeDtypeStruct((M, N), BF16), name)(a, b)


def _norm_mm_kernel(a_ref, g_ref, b_ref, o_ref):
    a = a_ref[...].astype(F32)
    r = lax.rsqrt(jnp.mean(a * a, axis=-1, keepdims=True) + EPS)
    an = (a * r * g_ref[...]).astype(BF16)
    o_ref[...] = jnp.dot(an, b_ref[...], preferred_element_type=F32).astype(o_ref.dtype)


def _norm_mm(a, col_off, K, g, b, name):
    M = a.shape[0]
    N = b.shape[1]
    tm = _tile(M, 512, 16)
    cb = col_off // K
    assert col_off % K == 0
    return _call(
        _norm_mm_kernel, (M // tm,),
        [pl.BlockSpec((tm, K), lambda i: (i, cb)),
         pl.BlockSpec((1, K), lambda i: (0, 0)),
         pl.BlockSpec((K, N), lambda i: (0, 0))],
        pl.BlockSpec((tm, N), lambda i: (i, 0)),
        jax.ShapeDtypeStruct((M, N), BF16), name)(a, g.reshape(1, K), b)


def _prep_head(xs, g_ref, rope, rope_chunk, inv_d, shift):
    ones = jnp.ones((LANE, LANE), BF16)
    ssq = sum(jnp.dot((x * x).astype(BF16), ones, preferred_element_type=F32) for x in xs)
    r = lax.rsqrt(ssq * inv_d + EPS)
    out = []
    for c, x in enumerate(xs):
        y = x * r * g_ref[:, c * LANE:(c + 1) * LANE]
        if rope is not None and c == rope_chunk:
            cos, sin = rope
            src = lax.broadcasted_iota(jnp.int32, (LANE, LANE), 0)
            dst = lax.broadcasted_iota(jnp.int32, (LANE, LANE), 1)
            partner = jnp.where((dst % (2 * shift)) < shift, dst + shift, dst - shift)
            swap = (src == partner).astype(BF16)
            y = y * cos + jnp.dot(y.astype(BF16), swap, preferred_element_type=F32) * sin
        out.append(y.astype(BF16))
    return out


def _rope_tables(n_pos, rope_dim):
    nf = rope_dim // 4
    t = jnp.arange(n_pos, dtype=jnp.int32)
    row = (t // GRID_W).astype(F32)
    col = (t % GRID_W).astype(F32)
    inv = ROPE_BASE ** (-jnp.arange(nf, dtype=F32) / nf)
    a_row, a_col = row[:, None] * inv, col[:, None] * inv
    cos = jnp.concatenate([jnp.cos(a_row)] * 2 + [jnp.cos(a_col)] * 2, axis=-1)
    sin = jnp.concatenate([-jnp.sin(a_row), jnp.sin(a_row), -jnp.sin(a_col), jnp.sin(a_col)], axis=-1)
    cos = jnp.pad(cos, ((0, 0), (0, LANE - rope_dim)), constant_values=1.0)
    sin = jnp.pad(sin, ((0, 0), (0, LANE - rope_dim)))
    return cos, sin


def _attn_kernel(*refs, n_chunks, src_cfg, q_rope, keys_once, rope_chunk, inv_d, shift, tq, n_q):
    it = iter(refs)
    q_ref = next(it)
    srcs = [([next(it) for _ in range(n_chunks)], next(it)) for _ in src_cfg]
    gq_ref, gk_ref, cos_ref, sin_ref, o_ref, k_scr = (next(it) for _ in range(6))

    def prep(chunk_refs, rows, g_ref, rope):
        xs = [r[rows, :].astype(F32) for r in chunk_refs]
        tab = (cos_ref[rows, :], sin_ref[rows, :]) if rope else None
        return _prep_head(xs, g_ref, tab, rope_chunk, inv_d, shift)

    def prep_keys():
        off = 0
        for (k_chunks, _), (n_k, rope) in zip(srcs, src_cfg):
            bk = min(n_k, 512)
            for r0 in range(0, n_k, bk):
                for c, y in enumerate(prep(k_chunks, pl.ds(r0, bk), gk_ref, rope)):
                    k_scr[pl.ds(off + r0, bk), c * LANE:(c + 1) * LANE] = y
            off += n_k

    if keys_once:
        pl.when(pl.program_id(2) == 0)(prep_keys)
    else:
        prep_keys()

    for c in range(n_q // tq):
        rows = pl.ds(c * tq, tq)
        q = jnp.concatenate(prep([q_ref.at[:, j * LANE:(j + 1) * LANE] for j in range(n_chunks)],
                                 rows, gq_ref, q_rope), axis=1)
        s = lax.dot_general(q, k_scr[...], (((1,), (1,)), ((), ())), preferred_element_type=F32)
        p = jnp.exp(s - jnp.max(s, axis=-1, keepdims=True))
        l = jnp.sum(p, axis=-1, keepdims=True)
        pb = p.astype(BF16)
        o, off = None, 0
        for (_, v_ref), (n_k, _) in zip(srcs, src_cfg):
            o_src = jnp.dot(pb[:, off:off + n_k], v_ref[...], preferred_element_type=F32)
            o = o_src if o is None else o + o_src
            off += n_k
        o_ref[rows, :] = (o * (1.0 / l)).astype(o_ref.dtype)


def _attend(q_arr, q_row_off, q_cb0, n_q, q_rope, srcs, g_q, g_k, cos, sin, rope_chunk, norm_dim, shift,
            B, n_kv_heads, group, dv, name):
    n_chunks = len(srcs[0][0])
    qw = n_chunks * LANE
    tq = _tile(n_q, 512, 16)
    assert q_row_off % n_q == 0
    qb = q_row_off // n_q
    in_specs = [pl.BlockSpec((n_q, qw), lambda b, hk, g: (qb + b, q_cb0 + hk * group + g))]
    args = [q_arr]
    src_cfg = []
    for k_chunks, (v_arr, v_cb0), row_off, n_k, rope in srcs:
        assert row_off % n_k == 0 and n_k % min(n_k, 512) == 0
        rb = row_off // n_k
        for arr, cb_fn in k_chunks:
            in_specs.append(pl.BlockSpec((n_k, LANE), (lambda b, hk, g, rb=rb, f=cb_fn: (rb + b, f(hk)))))
            args.append(arr)
        in_specs.append(pl.BlockSpec((n_k, dv), (lambda b, hk, g, rb=rb, c0=v_cb0: (rb + b, c0 + hk))))
        args.append(v_arr)
        src_cfg.append((n_k, rope))
    fixed = lambda b, hk, g: (0, 0)
    in_specs += [pl.BlockSpec((1, qw), fixed), pl.BlockSpec((1, qw), fixed),
                 pl.BlockSpec(cos.shape, fixed), pl.BlockSpec(sin.shape, fixed)]
    args += [g_q.reshape(1, qw), g_k.reshape(1, qw), cos, sin]
    assert cos.shape[0] >= max([n_q] + [n_k for n_k, rope in src_cfg if rope])
    kern = functools.partial(_attn_kernel, n_chunks=n_chunks, src_cfg=tuple(src_cfg), q_rope=q_rope,
                             keys_once=group > 1, rope_chunk=rope_chunk, inv_d=1.0 / norm_dim, shift=shift,
                             tq=tq, n_q=n_q)
    return _call(
        kern, (B, n_kv_heads, group), in_specs,
        pl.BlockSpec((n_q, dv), lambda b, hk, g: (b, hk * group + g)),
        jax.ShapeDtypeStruct((B * n_q, n_kv_heads * group * dv), BF16), name,
        scratch=[pltpu.VMEM((sum(n_k for n_k, _ in src_cfg), qw), BF16)])(*args)


def _conv_kernel(b_ref, c_ref, u_ref, w_ref, o_ref):
    v = c_ref[...].astype(F32) * u_ref[...].astype(F32)
    n = v.shape[0]
    row = lax.broadcasted_iota(jnp.int32, v.shape, 0)
    prev = jnp.where(row == 0, 0.0, pltpu.roll(v, 1, 0))
    nxt = jnp.where(row == n - 1, 0.0, pltpu.roll(v, n - 1, 0))
    y = prev * w_ref[0:1, :] + v * w_ref[1:2, :] + nxt * w_ref[2:3, :]
    o_ref[...] = (b_ref[...].astype(F32) * y).astype(o_ref.dtype)


def _conv(px, row_off, n_seq, seq_len, col_b, col_c, col_u, CW, conv_w):
    tc = _tile(CW, 512, LANE, col_b, col_c, col_u)
    rb = row_off // seq_len
    assert row_off % seq_len == 0

    def spec(col):
        return pl.BlockSpec((seq_len, tc), lambda s, j, col=col: (rb + s, col // tc + j))

    return _call(
        _conv_kernel, (n_seq, CW // tc),
        [spec(col_b), spec(col_c), spec(col_u), pl.BlockSpec((3, tc), lambda s, j: (0, j))],
        pl.BlockSpec((seq_len, tc), lambda s, j: (s, j)),
        jax.ShapeDtypeStruct((n_seq * seq_len, CW), BF16), "short_conv")(px, px, px, conv_w)


def _merge_kernel(oa_ref, ob_ref, oc_ref, wa_ref, wb_ref, wc_ref, ga_ref, gb_ref, gc_ref, o_ref):
    def term(o, w, g):
        return jax.nn.sigmoid(g[...].astype(F32)) * jnp.dot(o[...], w[...], preferred_element_type=F32)
    h = term(oa_ref, wa_ref, ga_ref) + term(ob_ref, wb_ref, gb_ref) + term(oc_ref, wc_ref, gc_ref)
    o_ref[...] = h.astype(o_ref.dtype)


def _merge(oa, ob, oc, w_pa, w_pb, w_pc, px, gate_col, n_rows):
    D = w_pa.shape[1]
    tm = _tile(n_rows, 1024, 16)
    tn = _tile(D, 512, LANE, gate_col)
    gb0 = gate_col // tn
    nb = D // tn

    def a_spec(o):
        return pl.BlockSpec((tm, o.shape[1]), lambda i, j: (i, 0))

    def w_spec(w):
        return pl.BlockSpec((w.shape[0], tn), lambda i, j: (0, j))

    def g_spec(k):
        return pl.BlockSpec((tm, tn), lambda i, j, k=k: (i, gb0 + k * nb + j))

    return _call(
        _merge_kernel, (n_rows // tm, nb),
        [a_spec(oa), a_spec(ob), a_spec(oc), w_spec(w_pa), w_spec(w_pb), w_spec(w_pc),
         g_spec(0), g_spec(1), g_spec(2)],
        pl.BlockSpec((tm, tn), lambda i, j: (i, j)),
        jax.ShapeDtypeStruct((n_rows, D), BF16), "merge")(oa, ob, oc, w_pa, w_pb, w_pc, px, px, px)


def _mm_res_kernel(a_ref, b_ref, *refs, n_a):
    x_refs, (gt_ref, o_ref) = refs[:-2], refs[-2:]
    x = x_refs[0][...]
    if len(x_refs) == 2:
        x = jnp.where(pl.program_id(0) < n_a, x, x_refs[1][...])
    acc = jnp.dot(a_ref[...], b_ref[...], preferred_element_type=F32)
    o_ref[...] = x + gt_ref[...] * acc


def _mm_res(a, b, X, mods3, tok, l, which, n_rows, Xb=None):
    K = a.shape[1]
    D = b.shape[1]
    tm = _tile(n_rows, 1024, 16, tok.N)
    tn = _tile(D, 512, LANE)

    def gt_idx(i, j):
        row = tok.mod_row(i * tm)
        return ((l * MOD_ROWS + row) * N_MOD + which, 0, j)

    if Xb is None:
        x_specs, x_args, na = [pl.BlockSpec((tm, tn), lambda i, j: (i, j))], [X], 0
    else:
        assert X.shape[0] % tm == 0
        x_specs, x_args, na = _two_source_specs(X, Xb, (tm, tn), tm, lambda j: j)
    return _call(
        functools.partial(_mm_res_kernel, n_a=na), (n_rows // tm, D // tn),
        [pl.BlockSpec((tm, K), lambda i, j: (i, 0)), pl.BlockSpec((K, tn), lambda i, j: (0, j))]
        + x_specs + [pl.BlockSpec((None, 1, tn), gt_idx)],
        pl.BlockSpec((tm, tn), lambda i, j: (i, j)),
        jax.ShapeDtypeStruct((n_rows, D), F32), "proj_residual")(a, b, *x_args, mods3)


def _swiglu_up_kernel(h_ref, w_ref, o_ref):
    gu = jnp.dot(h_ref[...], w_ref[...], preferred_element_type=F32)
    f = o_ref.shape[1]
    o_ref[...] = (_silu(gu[:, :f]) * gu[:, f:]).astype(o_ref.dtype)


def _swiglu_up(h, w_gu):
    M, D = h.shape
    f = w_gu.shape[1] // 2
    tm = _tile(M, 1024, 16)
    return _call(
        _swiglu_up_kernel, (M // tm,),
        [pl.BlockSpec((tm, D), lambda i: (i, 0)), pl.BlockSpec((D, 2 * f), lambda i: (0, 0))],
        pl.BlockSpec((tm, f), lambda i: (i, 0)),
        jax.ShapeDtypeStruct((M, f), BF16), "shared_up")(h, w_gu)


SLAB_ROWS = 16
STAGE_PITCH = 24
EXPERT_TILE = 256
HI_MASK = -65536


def _pack_pairs(v):
    half = v.shape[1] // 2
    bits = lax.bitcast_convert_type(v.astype(BF16).astype(F32), jnp.int32)
    return lax.shift_right_logical(bits[:, :half], 16) | (bits[:, half:] & HI_MASK)


def _unpack_lo(w):
    return lax.bitcast_convert_type(lax.shift_left(w, 16), F32)


def _unpack_hi(w):
    return lax.bitcast_convert_type(w & HI_MASK, F32)


def _dispatch_kernel(pos_ref, zt_ref, h_ref, xs_hbm, slab_ref, zero_ref, sem, *, tm, n_experts):
    W = slab_ref.shape[1]
    trows = EXPERT_TILE * SLAB_ROWS

    @pl.when(pl.program_id(0) == 0)
    def _():
        zero_ref[...] = jnp.zeros_like(zero_ref)
        def zcopy(t):
            return pltpu.make_async_copy(zero_ref, xs_hbm.at[pl.ds(pl.multiple_of(t * trows, trows), trows), :], sem)
        for e in range(n_experts):
            zcopy(zt_ref[0, e]).start()
        for e in range(n_experts):
            zcopy(zt_ref[0, e]).wait()
        n_used = zt_ref[0, n_experts]
        n_tiles = xs_hbm.shape[0] // trows

        def zstart(t, carry):
            zcopy(t).start()
            return carry

        def zwait(t, carry):
            zcopy(t).wait()
            return carry

        lax.fori_loop(n_used, n_tiles, zstart, 0)
        lax.fori_loop(n_used, n_tiles, zwait, 0)

    words = _pack_pairs(h_ref[...])
    for s in range(SLAB_ROWS):
        slab_ref[pl.ds(s, tm, stride=STAGE_PITCH), :] = words[:, s * W:(s + 1) * W]

    def row_copy(r, k):
        p = pos_ref[0, r * TOP_K + k]
        return pltpu.make_async_copy(
            slab_ref.at[pl.ds(pl.multiple_of(r * STAGE_PITCH, 8), SLAB_ROWS), :],
            xs_hbm.at[pl.ds(pl.multiple_of(p * SLAB_ROWS, SLAB_ROWS), SLAB_ROWS), :], sem)

    def issue(r, carry):
        for k in range(TOP_K):
            row_copy(r, k).start(priority=k % 2)
        return carry

    lax.fori_loop(0, tm, issue, 0)
    for k in range(TOP_K):
        pltpu.make_async_copy(slab_ref.at[pl.ds(0, tm * SLAB_ROWS), :],
                              xs_hbm.at[pl.ds(0, tm * SLAB_ROWS), :], sem).wait()


def _dispatch(h2, pos, zero_tiles, n_slots):
    M, D = h2.shape
    W = D // (2 * SLAB_ROWS)
    assert W == LANE, "strided slab access needs LANE-wide slab rows"
    E = zero_tiles.shape[0] - 1
    tm = _tile(M, 256, 16)
    kern = functools.partial(_dispatch_kernel, tm=tm, n_experts=E)
    return _call(
        kern, (M // tm,),
        [pl.BlockSpec((None, 1, tm * TOP_K), lambda i: (i, 0, 0), memory_space=pltpu.SMEM),
         pl.BlockSpec((1, E + 1), lambda i: (0, 0), memory_space=pltpu.SMEM),
         pl.BlockSpec((tm, D), lambda i: (i, 0))],
        pl.BlockSpec(memory_space=pl.ANY),
        jax.ShapeDtypeStruct((n_slots * SLAB_ROWS, W), jnp.int32), "moe_dispatch",
        scratch=[pltpu.VMEM((tm * STAGE_PITCH, W), jnp.int32),
                 pltpu.VMEM((EXPERT_TILE * SLAB_ROWS, W), jnp.int32),
                 pltpu.SemaphoreType.DMA(())])(
            pos.reshape(M // tm, 1, tm * TOP_K), zero_tiles.reshape(1, E + 1), h2)


def _expert_kernel(te_ref, nu_ref, x_ref, wgu_ref, wdn_ref, o_ref):
    W = x_ref.shape[1]
    f = wdn_ref.shape[0]

    @pl.when(pl.program_id(0) < nu_ref[0])
    def _():
        words = [x_ref[pl.ds(s, EXPERT_TILE, stride=SLAB_ROWS), :] for s in range(SLAB_ROWS)]
        x = jnp.concatenate([_unpack_lo(w).astype(BF16) for w in words]
                            + [_unpack_hi(w).astype(BF16) for w in words], axis=1)
        gu = jnp.dot(x, wgu_ref[...], preferred_element_type=F32)
        act = (_silu(gu[:, :f]) * gu[:, f:]).astype(BF16)
        y = _pack_pairs(jnp.dot(act, wdn_ref[...], preferred_element_type=F32))
        for s in range(SLAB_ROWS):
            o_ref[pl.ds(s, EXPERT_TILE, stride=SLAB_ROWS), :] = y[:, s * W:(s + 1) * W]

    @pl.when(pl.program_id(0) >= nu_ref[0])
    def _():
        o_ref[...] = jnp.zeros_like(o_ref)


def _experts(xs, tile_expert, n_used, w_gu, w_down):
    E, D, F2 = w_gu.shape
    W = xs.shape[1]
    trows = EXPERT_TILE * SLAB_ROWS
    n_tiles = xs.shape[0] // trows
    grid_spec = pltpu.PrefetchScalarGridSpec(
        num_scalar_prefetch=2, grid=(n_tiles,),
        in_specs=[pl.BlockSpec((trows, W), lambda i, te, nu: (jnp.minimum(i, nu[0] - 1), 0)),
                  pl.BlockSpec((None, D, F2), lambda i, te, nu: (te[i], 0, 0)),
                  pl.BlockSpec((None, F2 // 2, D), lambda i, te, nu: (te[i], 0, 0))],
        out_specs=pl.BlockSpec((trows, W), lambda i, te, nu: (i, 0)))
    return pl.pallas_call(
        _expert_kernel, grid_spec=grid_spec, out_shape=jax.ShapeDtypeStruct(xs.shape, jnp.int32),
        name="moe_experts",
        compiler_params=pltpu.CompilerParams(dimension_semantics=("arbitrary",),
                                             vmem_limit_bytes=VMEM_LIMIT_BYTES))(
            tile_expert, n_used, xs, w_gu, w_down)


def _combine_kernel(pos_ref, posn_ref, wt_ref, sh_ref, x_ref, gt_ref, ys_hbm, o_ref, buf_a, buf_b, sem, *, tm):
    j = pl.program_id(0)
    W = buf_a.shape[1]
    half = SLAB_ROWS * W

    def issue(pref, first_tok, buf, s_idx):
        def body(r, carry):
            for k in range(TOP_K):
                p = pref[0, (first_tok + r) * TOP_K + k]
                pltpu.make_async_copy(
                    ys_hbm.at[pl.ds(pl.multiple_of(p * SLAB_ROWS, SLAB_ROWS), SLAB_ROWS), :],
                    buf.at[pl.ds(pl.multiple_of((k * tm + r) * STAGE_PITCH, 8), SLAB_ROWS), :],
                    sem.at[s_idx]).start()
            return carry
        lax.fori_loop(0, tm, body, 0)

    def drain(buf, s_idx):
        for k in range(TOP_K):
            pltpu.make_async_copy(ys_hbm.at[pl.ds(0, tm * SLAB_ROWS), :],
                                  buf.at[pl.ds(0, tm * SLAB_ROWS), :], sem.at[s_idx]).wait()

    def reduce(buf, r0):
        rows = pl.ds(r0, tm)
        wt = wt_ref[rows, :]
        wk = [wt[:, k:k + 1] for k in range(TOP_K)]
        for s in range(SLAB_ROWS):
            lo = hi = None
            for k in range(TOP_K):
                w = buf[pl.ds(k * tm * STAGE_PITCH + s, tm, stride=STAGE_PITCH), :]
                lo_k, hi_k = wk[k] * _unpack_lo(w), wk[k] * _unpack_hi(w)
                lo, hi = (lo_k, hi_k) if lo is None else (lo + lo_k, hi + hi_k)
            for c0, routed in ((s * W, lo), (half + s * W, hi)):
                cols = slice(c0, c0 + W)
                o_ref[rows, cols] = x_ref[rows, cols] + gt_ref[:, cols] * (sh_ref[rows, cols].astype(F32) + routed)

    @pl.when(j == 0)
    def _():
        issue(pos_ref, 0, buf_a, 0)

    issue(pos_ref, tm, buf_b, 1)
    drain(buf_a, 0)
    reduce(buf_a, 0)

    @pl.when(j + 1 < pl.num_programs(0))
    def _():
        issue(posn_ref, 0, buf_a, 0)

    drain(buf_b, 1)
    reduce(buf_b, tm)


def _combine(ys, pos, wt, sh, X, mods3, tok, l, which, n_rows):
    D = X.shape[1]
    W = ys.shape[1]
    E = wt.shape[1]
    tm = _tile(n_rows // 2, 128, 16, tok.N // 2)
    tb = 2 * tm
    n_steps = n_rows // tb
    pos3 = pos.reshape(n_steps, 1, tb * TOP_K)

    def gt_idx(j):
        row = tok.mod_row(j * tb)
        return ((l * MOD_ROWS + row) * N_MOD + which, 0, 0)

    kern = functools.partial(_combine_kernel, tm=tm)
    return _call(
        kern, (n_steps,),
        [pl.BlockSpec((None, 1, tb * TOP_K), lambda j: (j, 0, 0), memory_space=pltpu.SMEM),
         pl.BlockSpec((None, 1, tb * TOP_K), lambda j: (jnp.minimum(j + 1, n_steps - 1), 0, 0),
                      memory_space=pltpu.SMEM),
         pl.BlockSpec((tb, E), lambda j: (j, 0)),
         pl.BlockSpec((tb, D), lambda j: (j, 0)),
         pl.BlockSpec((tb, D), lambda j: (j, 0)),
         pl.BlockSpec((None, 1, D), gt_idx),
         pl.BlockSpec(memory_space=pl.ANY)],
        pl.BlockSpec((tb, D), lambda j: (j, 0)),
        jax.ShapeDtypeStruct((n_rows, D), F32), "moe_combine",
        scratch=[pltpu.VMEM((TOP_K * tm * STAGE_PITCH, W), jnp.int32),
                 pltpu.VMEM((TOP_K * tm * STAGE_PITCH, W), jnp.int32),
                 pltpu.SemaphoreType.DMA((2,))])(pos3, pos3, wt, sh, X, mods3, ys)


def _routing_plan(idx, rank, counts, n_tiles):
    E = counts.shape[0]
    padded = (counts + EXPERT_TILE - 1) // EXPERT_TILE * EXPERT_TILE
    ends = jnp.cumsum(padded)
    base = ends - padded
    experts = jnp.arange(E, dtype=jnp.int32)
    pos = jnp.sum(jnp.where(idx[..., None] == experts, base, 0), axis=-1) + rank
    tile_start = jnp.arange(n_tiles, dtype=jnp.int32) * EXPERT_TILE
    tile_expert = jnp.minimum(jnp.sum((ends[None, :] <= tile_start[:, None]).astype(jnp.int32), axis=1), E - 1)
    n_used = (ends[-1] // EXPERT_TILE).reshape(1)
    last_tile = jnp.maximum(ends // EXPERT_TILE - 1, 0)
    return pos.astype(jnp.int32), tile_expert.astype(jnp.int32), n_used.astype(jnp.int32), last_tile.astype(jnp.int32)


def kernel(x, c, ctx, c_ctx, w_mod, b_mod, g_mix, g_ffn, w_in, mla_q_norm_g, mla_kv_norm_g, w_uq, w_ukv,
           mla_q_head_g, mla_k_head_g, gqa_q_g, gqa_k_g, conv_w, w_pa, w_pb, w_pc, w_o, w_router, b_router,
           w_exp_gu, w_exp_down, w_sh_gu, w_sh_down):
    B, N, D = x.shape
    CTX = ctx.shape[1]
    L = w_mod.shape[0]
    QL = mla_q_norm_g.shape[1]
    KVL = mla_kv_norm_g.shape[1]
    HA = w_uq.shape[2] // MLA_QK
    HG = w_pc.shape[1] // GQA_HD
    CW = conv_w.shape[2]
    D_IN = w_in.shape[2]
    HKV = (D_IN - (KVL + MLA_ROPE + QL + HG * GQA_HD + 3 * CW + 3 * D)) // (2 * GQA_HD)
    GKV = HKV * GQA_HD
    E = w_router.shape[2]

    tok = _Tok(B, N, CTX)
    n_lat, n_ctx = tok.n_lat, tok.n_ctx

    KV_COLS = KVL + 2 * GKV + LANE
    col_gk, col_gv, col_kr = KVL, KVL + GKV, KVL + 2 * GKV
    main0 = KVL + MLA_ROPE + 2 * GKV
    col_cq, col_gq = 0, QL
    col_b, col_c, col_u = QL + HG * GQA_HD, QL + HG * GQA_HD + CW, QL + HG * GQA_HD + 2 * CW
    col_gate = col_u + CW

    cvec = jnp.zeros((MOD_ROWS, D), F32).at[:B].set(c).at[B].set(c_ctx)
    mods = _mods(cvec, w_mod, b_mod)
    mods3 = mods.reshape(L * MOD_ROWS * N_MOD, 1, D)

    cos_a, sin_a = _rope_tables(N, MLA_ROPE)
    cos_g, sin_g = _rope_tables(N, GQA_HD)

    X, Xb = x.reshape(n_lat, D), ctx.reshape(n_ctx, D)

    for l in range(L):
        last = l == L - 1
        n_rows = n_lat if last else n_lat + n_ctx

        w_main = w_in[l, :, main0:].astype(BF16)
        w_kv = jnp.concatenate(
            [w_in[l, :, :KVL].astype(BF16), w_in[l, :, KVL + MLA_ROPE:main0].astype(BF16),
             w_in[l, :, KVL:KVL + MLA_ROPE].astype(BF16), jnp.zeros((D, LANE - MLA_ROPE), BF16)], axis=1)
        w_uq_p = jnp.pad(w_uq[l].reshape(QL, HA, MLA_QK),
                         ((0, 0), (0, 0), (0, MLA_QK_PAD - MLA_QK))).reshape(QL, HA * MLA_QK_PAD).astype(BF16)
        w_ukv_r = w_ukv[l].reshape(KVL, HA, MLA_NOPE + MLA_V)
        w_ukv_p = jnp.concatenate([w_ukv_r[:, :, :MLA_NOPE].reshape(KVL, HA * MLA_NOPE),
                                   w_ukv_r[:, :, MLA_NOPE:].reshape(KVL, HA * MLA_V)], axis=1).astype(BF16)
        g_qa = jnp.pad(mla_q_head_g[l] * (1.0 / math.sqrt(MLA_QK)), (0, MLA_QK_PAD - MLA_QK))
        g_ka = jnp.pad(mla_k_head_g[l], (0, MLA_QK_PAD - MLA_QK))
        g_qg = gqa_q_g[l] * (1.0 / math.sqrt(GQA_HD))
        g_kg = gqa_k_g[l]

        h = _norm_mod(X, 0, n_rows, g_mix[l], mods3, tok, l, 0, 0, Xb=None if last else Xb)
        px = _mm(h, w_main, name="in_proj")
        if last:
            c_src, c_off = (X, n_lat) if Xb is None else (Xb, 0)
            hc = _norm_mod(c_src, c_off, n_ctx, g_mix[l], mods3, tok, l, 0, n_lat)
            pkv_x = _mm(h, w_kv, tn_pref=KV_COLS, name="in_proj_kv")
            pkv_c = _mm(hc, w_kv, tn_pref=KV_COLS, name="in_proj_kv")
            kv_parts = [(pkv_x, n_lat, 0), (pkv_c, n_ctx, n_lat)]
        else:
            pkv = _mm(h, w_kv, tn_pref=KV_COLS, name="in_proj_kv")
            kv_parts = [(pkv, n_rows, 0)]

        kvr = [_norm_mm(p, 0, KVL, mla_kv_norm_g[l], w_ukv_p, "mla_kv_up") for p, _, _ in kv_parts]
        pk = [p for p, _, _ in kv_parts]
        q_raw = _norm_mm(px, col_cq, QL, mla_q_norm_g[l], w_uq_p, "mla_q_up")

        ctx_i = len(kv_parts) - 1
        ctx_off = 0 if last else n_lat
        vA0 = HA * MLA_NOPE // MLA_V

        def mla_src(i, row_off, n_k, rope):
            return ([(kvr[i], lambda hk: hk), (pk[i], lambda hk: col_kr // LANE)], (kvr[i], vA0), row_off, n_k, rope)

        def gqa_src(i, row_off, n_k, rope):
            return ([(pk[i], lambda hk: col_gk // LANE + hk)], (pk[i], col_gv // GQA_HD), row_off, n_k, rope)

        mla_args = (g_qa, g_ka, cos_a, sin_a, 1, MLA_QK, MLA_ROPE // 4, B, HA, 1, MLA_V)
        gqa_args = (g_qg, g_kg, cos_g, sin_g, 0, GQA_HD, GQA_HD // 4, B, HKV, HG // HKV, GQA_HD)
        oa = _attend(q_raw, 0, 0, N, True, [mla_src(ctx_i, ctx_off, CTX, False), mla_src(0, 0, N, True)],
                     *mla_args, "mla_attn")
        og = _attend(px, 0, col_gq // GQA_HD, N, True, [gqa_src(ctx_i, ctx_off, CTX, False), gqa_src(0, 0, N, True)],
                     *gqa_args, "gqa_attn")
        ob = _conv(px, 0, B, N, col_b, col_c, col_u, CW, conv_w[l])
        if not last:
            oa_c = _attend(q_raw, n_lat, 0, CTX, False, [mla_src(0, n_lat, CTX, False)], *mla_args, "mla_attn_ctx")
            og_c = _attend(px, n_lat, col_gq // GQA_HD, CTX, False, [gqa_src(0, n_lat, CTX, False)],
                           *gqa_args, "gqa_attn_ctx")
            ob_c = _conv(px, n_lat, B, CTX, col_b, col_c, col_u, CW, conv_w[l])
            oa = jnp.concatenate([oa, oa_c], axis=0)
            og = jnp.concatenate([og, og_c], axis=0)
            ob = jnp.concatenate([ob, ob_c], axis=0)

        hm = _merge(oa, ob, og, w_pa[l].astype(BF16), w_pb[l].astype(BF16), w_pc[l].astype(BF16),
                    px, col_gate, n_rows)
        X = _mm_res(hm, w_o[l].astype(BF16), X, mods3, tok, l, 2, n_rows, Xb=None if last else Xb)
        Xb = None

        h2, idx_m, wt_m, rank_m, cnt = _norm_router(X, n_rows, g_ffn[l], mods3, tok, l, w_router[l], b_router[l])
        n_tiles = n_rows * TOP_K // EXPERT_TILE + E
        pos, tile_expert, n_used, last_tile = _routing_plan(idx_m[:, :TOP_K], rank_m[:, :TOP_K], cnt[0], n_tiles)
        xs = _dispatch(h2, pos, jnp.concatenate([last_tile, n_used]), n_tiles * EXPERT_TILE)
        ys = _experts(xs, tile_expert, n_used, w_exp_gu[l].astype(BF16), w_exp_down[l].astype(BF16))
        sh = _mm(_swiglu_up(h2, w_sh_gu[l].astype(BF16)), w_sh_down[l].astype(BF16), name="shared_down")
        X = _combine(ys, pos, wt_m, sh, X, mods3, tok, l, 5, n_rows)

    return X.reshape(B, N, D)
```

```python
import functools
import math

import jax
import jax.numpy as jnp
from jax import lax
from jax.experimental import pallas as pl
from jax.experimental.pallas import tpu as pltpu

F32 = jnp.float32
BF16 = jnp.bfloat16

GRID_W = 64
ROPE_BASE = 10000.0
EPS = 1e-6
MLA_NOPE = 128
MLA_ROPE = 64
MLA_QK = MLA_NOPE + MLA_ROPE
MLA_V = 128
GQA_HD = 128
TOP_K = 4
ROUTE_SCALE = 2.5
N_MOD = 6

LANE = 128
MLA_QK_PAD = 2 * LANE
MOD_ROWS = 16
VMEM_LIMIT_BYTES = 56 * 1024 * 1024


def _tile(dim, pref, unit, *also):
    t = min(pref, dim) // unit * unit
    while t > unit:
        if dim % t == 0 and all(a % t == 0 for a in also):
            return t
        t -= unit
    assert dim % unit == 0 and all(a % unit == 0 for a in also), (dim, unit, also)
    return unit


def _call(kernel, grid, in_specs, out_specs, out_shape, name, scratch=()):
    return pl.pallas_call(
        kernel, grid=grid, in_specs=in_specs, out_specs=out_specs, out_shape=out_shape,
        scratch_shapes=list(scratch), name=name,
        compiler_params=pltpu.CompilerParams(
            dimension_semantics=("arbitrary",) * len(grid), vmem_limit_bytes=VMEM_LIMIT_BYTES))


def _silu(v):
    return v * jax.nn.sigmoid(v)


def _layer_spec(l, block, index):
    return pl.BlockSpec((None,) + tuple(block), lambda *g: (l,) + tuple(index(*g)))


def _mod_kernel(c_ref, w_ref, b_ref, o_ref):
    s = _silu(c_ref[...]).astype(BF16)
    o_ref[0] = jnp.dot(s, w_ref[0].astype(BF16), preferred_element_type=F32) + b_ref[0]


def _mods(cvec, w_mod, b_mod):
    L, D, N6 = w_mod.shape
    tn = _tile(N6, 512, LANE)
    return _call(
        _mod_kernel, (L, N6 // tn),
        [pl.BlockSpec((MOD_ROWS, D), lambda l, j: (0, 0)),
         pl.BlockSpec((1, D, tn), lambda l, j: (l, 0, j)),
         pl.BlockSpec((1, 1, tn), lambda l, j: (l, 0, j))],
        pl.BlockSpec((1, MOD_ROWS, tn), lambda l, j: (l, 0, j)),
        jax.ShapeDtypeStruct((L, MOD_ROWS, N6), F32), "adaln_mod")(cvec, w_mod, b_mod.reshape(L, 1, N6))


class _Tok:
    def __init__(self, B, N, CTX):
        self.B, self.N, self.CTX = B, N, CTX
        self.n_lat = B * N
        self.n_ctx = B * CTX

    def mod_row(self, tok0):
        return jnp.where(tok0 < self.n_lat, tok0 // self.N, self.B)

    def pos_block(self, tok0, tm):
        return jnp.where(tok0 < self.n_lat, (tok0 % self.N) // tm, self.N // tm)


def _mod_spec(tok, l, which, tm, tok_off, D, ngrid):
    def idx(i, *_):
        row = tok.mod_row(i * tm + tok_off)
        return ((l * MOD_ROWS + row) * N_MOD + which, 0, 0)
    return pl.BlockSpec((None, 1, D), idx)


def _two_source_specs(X, Xb, block, row_tile, col_of):
    na = X.shape[0] // row_tile
    return ([pl.BlockSpec(block, lambda i, *r: (jnp.minimum(i, na - 1), col_of(*r))),
             pl.BlockSpec(block, lambda i, *r: (jnp.maximum(i - na, 0), col_of(*r)))], [X, Xb], na)


def _norm_mod_kernel(*refs, n_a):
    x_refs, (g_ref, sh_ref, sc_ref, o_ref) = refs[:-4], refs[-4:]
    x = x_refs[0][...]
    if len(x_refs) == 2:
        x = jnp.where(pl.program_id(0) < n_a, x, x_refs[1][...])
    r = lax.rsqrt(jnp.mean(x * x, axis=-1, keepdims=True) + EPS)
    o_ref[...] = ((x * r * g_ref[...]) * (1.0 + sc_ref[...]) + sh_ref[...]).astype(o_ref.dtype)


def _norm_mod(X, row_off, n_rows, g, mods3, tok, l, which_shift, tok_off, Xb=None):
    D = X.shape[1]
    tm = _tile(n_rows, 256, 16, tok.N, row_off) if row_off else _tile(n_rows, 256, 16, tok.N)
    ro = row_off // tm
    if Xb is None:
        x_specs, x_args, na = [pl.BlockSpec((tm, D), lambda i: (i + ro, 0))], [X], 0
    else:
        assert row_off == 0 and X.shape[0] % tm == 0
        x_specs, x_args, na = _two_source_specs(X, Xb, (tm, D), tm, lambda: 0)
    return _call(
        functools.partial(_norm_mod_kernel, n_a=na), (n_rows // tm,),
        x_specs + [pl.BlockSpec((1, D), lambda i: (0, 0)),
                   _mod_spec(tok, l, which_shift, tm, tok_off, D, 1),
                   _mod_spec(tok, l, which_shift + 1, tm, tok_off, D, 1)],
        pl.BlockSpec((tm, D), lambda i: (i, 0)),
        jax.ShapeDtypeStruct((n_rows, D), BF16), "norm_mod")(*x_args, g.reshape(1, D), mods3, mods3)


def _norm_router_kernel(x_ref, g_ref, sh_ref, sc_ref, wr_ref, br_ref,
                        h_ref, idx_ref, wt_ref, rank_ref, cnt_ref, carry_ref):
    @pl.when(pl.program_id(0) == 0)
    def _():
        carry_ref[...] = jnp.zeros_like(carry_ref)

    x = x_ref[...]
    r = lax.rsqrt(jnp.mean(x * x, axis=-1, keepdims=True) + EPS)
    h = (x * r * g_ref[...]) * (1.0 + sc_ref[...]) + sh_ref[...]
    h_ref[...] = h.astype(h_ref.dtype)
    n_e = br_ref.shape[1]
    h_hi = h.astype(BF16)
    h_lo = (h - h_hi.astype(F32)).astype(BF16)
    w_both = wr_ref[...]
    first = jnp.dot(h_hi, w_both, preferred_element_type=F32)
    logits = first[:, :n_e] + first[:, n_e:] + jnp.dot(h_lo, w_both[:, :n_e], preferred_element_type=F32)
    s = jax.nn.sigmoid(logits)
    sel = s + br_ref[...]
    tm, n_e = sel.shape
    lane = lax.broadcasted_iota(jnp.int32, sel.shape, 1).astype(F32)
    picked = jnp.zeros(sel.shape, F32)
    firsts = []
    for _ in range(TOP_K):
        cur = jnp.where(picked > 0.0, -jnp.inf, sel)
        m = jnp.max(cur, axis=-1, keepdims=True)
        first = jnp.min(jnp.where(cur == m, lane, float(n_e)), axis=-1, keepdims=True)
        picked = jnp.where(lane == first, 1.0, picked)
        firsts.append(first)
    w = picked * s
    gate = w / jnp.sum(w, axis=-1, keepdims=True) * ROUTE_SCALE

    earlier = (lax.broadcasted_iota(jnp.int32, (tm, tm), 1) < lax.broadcasted_iota(jnp.int32, (tm, tm), 0))
    cum = jnp.dot(earlier.astype(BF16), picked.astype(BF16), preferred_element_type=F32) + carry_ref[...]
    idx_m = jnp.zeros(sel.shape, F32)
    wt_m = jnp.zeros(sel.shape, F32)
    rank_m = jnp.zeros(sel.shape, F32)
    for k, first in enumerate(firsts):
        mine = lane == first
        idx_m = jnp.where(lane == float(k), first, idx_m)
        wt_m = jnp.where(lane == float(k), jnp.sum(jnp.where(mine, gate, 0.0), axis=-1, keepdims=True), wt_m)
        rank_m = jnp.where(lane == float(k), jnp.sum(jnp.where(mine, cum, 0.0), axis=-1, keepdims=True), rank_m)
    idx_ref[...] = idx_m.astype(jnp.int32)
    wt_ref[...] = wt_m
    rank_ref[...] = rank_m.astype(jnp.int32)
    carry_ref[...] += jnp.sum(picked, axis=0, keepdims=True)
    cnt_ref[...] = carry_ref[...].astype(jnp.int32)


def _norm_router(X, n_rows, g, mods3, tok, l, w_router, b_router):
    D = X.shape[1]
    E = w_router.shape[1]
    w_hi = w_router.astype(BF16)
    w_both = jnp.concatenate([w_hi, (w_router - w_hi.astype(F32)).astype(BF16)], axis=1)
    tm = _tile(n_rows, 256, 16, tok.N)
    row = lambda i: (i, 0)
    fixed = lambda i: (0, 0)
    return _call(
        _norm_router_kernel, (n_rows // tm,),
        [pl.BlockSpec((tm, D), row), pl.BlockSpec((1, D), fixed),
         _mod_spec(tok, l, 3, tm, 0, D, 1), _mod_spec(tok, l, 4, tm, 0, D, 1),
         pl.BlockSpec((D, 2 * E), fixed), pl.BlockSpec((1, E), fixed)],
        [pl.BlockSpec((tm, D), row), pl.BlockSpec((tm, E), row), pl.BlockSpec((tm, E), row),
         pl.BlockSpec((tm, E), row), pl.BlockSpec((1, E), fixed)],
        [jax.ShapeDtypeStruct((n_rows, D), BF16), jax.ShapeDtypeStruct((n_rows, E), jnp.int32),
         jax.ShapeDtypeStruct((n_rows, E), F32), jax.ShapeDtypeStruct((n_rows, E), jnp.int32),
         jax.ShapeDtypeStruct((1, E), jnp.int32)],
        "norm_router", scratch=[pltpu.VMEM((1, E), F32)])(
            X, g.reshape(1, D), mods3, mods3, w_both, b_router.reshape(1, E))


def _mm_kernel(a_ref, b_ref, o_ref):
    o_ref[...] = jnp.dot(a_ref[...], b_ref[...], preferred_element_type=F32).astype(o_ref.dtype)


def _mm(a, b, tm_pref=1024, tn_pref=1024, name="mm", layer=None):
    M, K = a.shape
    N = b.shape[-1]
    tm = _tile(M, tm_pref, 16)
    tn = _tile(N, tn_pref, LANE)
    b_spec = (pl.BlockSpec((K, tn), lambda i, j: (0, j)) if layer is None
              else _layer_spec(layer, (K, tn), lambda i, j: (0, j)))
    return _call(
        _mm_kernel, (M // tm, N // tn),
        [pl.BlockSpec((tm, K), lambda i, j: (i, 0)), b_spec],
        pl.BlockSpec((tm, tn), lambda i, j: (i, j)),
        jax.ShapeDtypeStruct((M, N), BF16), name)(a, b)


def _norm_mm_kernel(a_ref, g_ref, b_ref, o_ref):
    a = a_ref[...].astype(F32)
    r = lax.rsqrt(jnp.mean(a * a, axis=-1, keepdims=True) + EPS)
    an = (a * r * g_ref[...]).astype(BF16)
    o_ref[...] = jnp.dot(an, b_ref[...], preferred_element_type=F32).astype(o_ref.dtype)


def _norm_mm(a, col_off, K, g, b, name):
    M = a.shape[0]
    N = b.shape[1]
    tm = _tile(M, 512, 16)
    cb = col_off // K
    assert col_off % K == 0
    return _call(
        _norm_mm_kernel, (M // tm,),
        [pl.BlockSpec((tm, K), lambda i: (i, cb)),
         pl.BlockSpec((1, K), lambda i: (0, 0)),
         pl.BlockSpec((K, N), lambda i: (0, 0))],
        pl.BlockSpec((tm, N), lambda i: (i, 0)),
        jax.ShapeDtypeStruct((M, N), BF16), name)(a, g.reshape(1, K), b)


def _prep_head(xs, g_ref, rope, rope_chunk, inv_d, shift):
    ones = jnp.ones((LANE, LANE), BF16)
    ssq = sum(jnp.dot((x * x).astype(BF16), ones, preferred_element_type=F32) for x in xs)
    r = lax.rsqrt(ssq * inv_d + EPS)
    out = []
    for c, x in enumerate(xs):
        y = x * r * g_ref[:, c * LANE:(c + 1) * LANE]
        if rope is not None and c == rope_chunk:
            cos, sin = rope
            src = lax.broadcasted_iota(jnp.int32, (LANE, LANE), 0)
            dst = lax.broadcasted_iota(jnp.int32, (LANE, LANE), 1)
            partner = jnp.where((dst % (2 * shift)) < shift, dst + shift, dst - shift)
            swap = (src == partner).astype(BF16)
            y = y * cos + jnp.dot(y.astype(BF16), swap, preferred_element_type=F32) * sin
        out.append(y.astype(BF16))
    return out


def _rope_tables(n_pos, rope_dim):
    nf = rope_dim // 4
    t = jnp.arange(n_pos, dtype=jnp.int32)
    row = (t // GRID_W).astype(F32)
    col = (t % GRID_W).astype(F32)
    inv = ROPE_BASE ** (-jnp.arange(nf, dtype=F32) / nf)
    a_row, a_col = row[:, None] * inv, col[:, None] * inv
    cos = jnp.concatenate([jnp.cos(a_row)] * 2 + [jnp.cos(a_col)] * 2, axis=-1)
    sin = jnp.concatenate([-jnp.sin(a_row), jnp.sin(a_row), -jnp.sin(a_col), jnp.sin(a_col)], axis=-1)
    cos = jnp.pad(cos, ((0, 0), (0, LANE - rope_dim)), constant_values=1.0)
    sin = jnp.pad(sin, ((0, 0), (0, LANE - rope_dim)))
    return cos, sin


def _attn_kernel(*refs, n_chunks, src_cfg, q_rope, keys_once, rope_chunk, inv_d, shift, tq, n_q):
    it = iter(refs)
    q_ref = next(it)
    srcs = [([next(it) for _ in range(n_chunks)], next(it)) for _ in src_cfg]
    gq_ref, gk_ref, cos_ref, sin_ref, o_ref, k_scr = (next(it) for _ in range(6))

    def prep(chunk_refs, rows, g_ref, rope):
        xs = [r[rows, :].astype(F32) for r in chunk_refs]
        tab = (cos_ref[rows, :], sin_ref[rows, :]) if rope else None
        return _prep_head(xs, g_ref, tab, rope_chunk, inv_d, shift)

    def prep_keys():
        off = 0
        for (k_chunks, _), (n_k, rope) in zip(srcs, src_cfg):
            bk = min(n_k, 512)
            for r0 in range(0, n_k, bk):
                for c, y in enumerate(prep(k_chunks, pl.ds(r0, bk), gk_ref, rope)):
                    k_scr[pl.ds(off + r0, bk), c * LANE:(c + 1) * LANE] = y
            off += n_k

    if keys_once:
        pl.when(pl.program_id(2) == 0)(prep_keys)
    else:
        prep_keys()

    for c in range(n_q // tq):
        rows = pl.ds(c * tq, tq)
        q = jnp.concatenate(prep([q_ref.at[:, j * LANE:(j + 1) * LANE] for j in range(n_chunks)],
                                 rows, gq_ref, q_rope), axis=1)
        s = lax.dot_general(q, k_scr[...], (((1,), (1,)), ((), ())), preferred_element_type=F32)
        p = jnp.exp(s - jnp.max(s, axis=-1, keepdims=True))
        l = jnp.sum(p, axis=-1, keepdims=True)
        pb = p.astype(BF16)
        o, off = None, 0
        for (_, v_ref), (n_k, _) in zip(srcs, src_cfg):
            o_src = jnp.dot(pb[:, off:off + n_k], v_ref[...], preferred_element_type=F32)
            o = o_src if o is None else o + o_src
            off += n_k
        o_ref[rows, :] = (o * (1.0 / l)).astype(o_ref.dtype)


def _attend(q_arr, q_row_off, q_cb0, n_q, q_rope, srcs, g_q, g_k, cos, sin, rope_chunk, norm_dim, shift,
            B, n_kv_heads, group, dv, name):
    n_chunks = len(srcs[0][0])
    qw = n_chunks * LANE
    tq = _tile(n_q, 512, 16)
    assert q_row_off % n_q == 0
    qb = q_row_off // n_q
    in_specs = [pl.BlockSpec((n_q, qw), lambda b, hk, g: (qb + b, q_cb0 + hk * group + g))]
    args = [q_arr]
    src_cfg = []
    for k_chunks, (v_arr, v_cb0), row_off, n_k, rope in srcs:
        assert row_off % n_k == 0 and n_k % min(n_k, 512) == 0
        rb = row_off // n_k
        for arr, cb_fn in k_chunks:
            in_specs.append(pl.BlockSpec((n_k, LANE), (lambda b, hk, g, rb=rb, f=cb_fn: (rb + b, f(hk)))))
            args.append(arr)
        in_specs.append(pl.BlockSpec((n_k, dv), (lambda b, hk, g, rb=rb, c0=v_cb0: (rb + b, c0 + hk))))
        args.append(v_arr)
        src_cfg.append((n_k, rope))
    fixed = lambda b, hk, g: (0, 0)
    in_specs += [pl.BlockSpec((1, qw), fixed), pl.BlockSpec((1, qw), fixed),
                 pl.BlockSpec(cos.shape, fixed), pl.BlockSpec(sin.shape, fixed)]
    args += [g_q.reshape(1, qw), g_k.reshape(1, qw), cos, sin]
    assert cos.shape[0] >= max([n_q] + [n_k for n_k, rope in src_cfg if rope])
    n_all = sum(n_k for n_k, _ in src_cfg)
    kern = functools.partial(_attn_kernel, n_chunks=n_chunks, src_cfg=tuple(src_cfg), q_rope=q_rope,
                             keys_once=group > 1, rope_chunk=rope_chunk, inv_d=1.0 / norm_dim, shift=shift,
                             tq=tq, n_q=n_q)
    return _call(
        kern, (B, n_kv_heads, group), in_specs,
        pl.BlockSpec((n_q, dv), lambda b, hk, g: (b, hk * group + g)),
        jax.ShapeDtypeStruct((B * n_q, n_kv_heads * group * dv), BF16), name,
        scratch=[pltpu.VMEM((n_all, qw), BF16)])(*args)


def _conv_kernel(b_ref, c_ref, u_ref, w_ref, o_ref):
    v = c_ref[...].astype(F32) * u_ref[...].astype(F32)
    n = v.shape[0]
    row = lax.broadcasted_iota(jnp.int32, v.shape, 0)
    prev = jnp.where(row == 0, 0.0, pltpu.roll(v, 1, 0))
    nxt = jnp.where(row == n - 1, 0.0, pltpu.roll(v, n - 1, 0))
    y = prev * w_ref[0:1, :] + v * w_ref[1:2, :] + nxt * w_ref[2:3, :]
    o_ref[...] = (b_ref[...].astype(F32) * y).astype(o_ref.dtype)


def _conv(px, row_off, n_seq, seq_len, col_b, col_c, col_u, CW, conv_w):
    tc = _tile(CW, 512, LANE, col_b, col_c, col_u)
    rb = row_off // seq_len
    assert row_off % seq_len == 0

    def spec(col):
        return pl.BlockSpec((seq_len, tc), lambda s, j, col=col: (rb + s, col // tc + j))

    return _call(
        _conv_kernel, (n_seq, CW // tc),
        [spec(col_b), spec(col_c), spec(col_u), pl.BlockSpec((3, tc), lambda s, j: (0, j))],
        pl.BlockSpec((seq_len, tc), lambda s, j: (s, j)),
        jax.ShapeDtypeStruct((n_seq * seq_len, CW), BF16), "short_conv")(px, px, px, conv_w)


def _merge_kernel(*refs, n_a, n_src):
    o_refs, (wa_ref, wb_ref, wc_ref, ga_ref, gb_ref, gc_ref, o_ref) = refs[:3 * n_src], refs[3 * n_src:]

    def branch(k):
        o = o_refs[k * n_src][...]
        if n_src == 2:
            o = jnp.where(pl.program_id(0) < n_a, o, o_refs[k * n_src + 1][...])
        return o

    def term(o, w, g):
        return jax.nn.sigmoid(g[...].astype(F32)) * jnp.dot(o, w[...], preferred_element_type=F32)
    h = term(branch(0), wa_ref, ga_ref) + term(branch(1), wb_ref, gb_ref) + term(branch(2), wc_ref, gc_ref)
    o_ref[...] = h.astype(o_ref.dtype)


def _merge(oa, ob, oc, w_pa, w_pb, w_pc, l, px, gate_col, n_rows):
    D = w_pa.shape[2]
    tm = _tile(n_rows, 1024, 16, oa[0].shape[0])
    tn = _tile(D, 512, LANE, gate_col)
    gb0 = gate_col // tn
    nb = D // tn

    n_src = 1 if oa[1] is None else 2
    a_specs, a_args, na = [], [], 0
    for lat, ctx in (oa, ob, oc):
        if ctx is None:
            a_specs.append(pl.BlockSpec((tm, lat.shape[1]), lambda i, j: (i, 0)))
            a_args.append(lat)
        else:
            assert lat.shape[0] % tm == 0
            specs, args, na = _two_source_specs(lat, ctx, (tm, lat.shape[1]), tm, lambda j: 0)
            a_specs += specs
            a_args += args

    def w_spec(w):
        return _layer_spec(l, (w.shape[1], tn), lambda i, j: (0, j))

    def g_spec(k):
        return pl.BlockSpec((tm, tn), lambda i, j, k=k: (i, gb0 + k * nb + j))

    return _call(
        functools.partial(_merge_kernel, n_a=na, n_src=n_src), (n_rows // tm, nb),
        a_specs + [w_spec(w_pa), w_spec(w_pb), w_spec(w_pc), g_spec(0), g_spec(1), g_spec(2)],
        pl.BlockSpec((tm, tn), lambda i, j: (i, j)),
        jax.ShapeDtypeStruct((n_rows, D), BF16), "merge")(*a_args, w_pa, w_pb, w_pc, px, px, px)


def _mm_res_kernel(a_ref, b_ref, *refs, n_a):
    x_refs, (gt_ref, o_ref) = refs[:-2], refs[-2:]
    x = x_refs[0][...]
    if len(x_refs) == 2:
        x = jnp.where(pl.program_id(0) < n_a, x, x_refs[1][...])
    acc = jnp.dot(a_ref[...], b_ref[...], preferred_element_type=F32)
    o_ref[...] = x + gt_ref[...] * acc


def _mm_res(a, b, X, mods3, tok, l, which, n_rows, Xb=None):
    K = a.shape[1]
    D = b.shape[2]
    tm = _tile(n_rows, 1024, 16, tok.N)
    tn = _tile(D, 512, LANE)

    def gt_idx(i, j):
        row = tok.mod_row(i * tm)
        return ((l * MOD_ROWS + row) * N_MOD + which, 0, j)

    if Xb is None:
        x_specs, x_args, na = [pl.BlockSpec((tm, tn), lambda i, j: (i, j))], [X], 0
    else:
        assert X.shape[0] % tm == 0
        x_specs, x_args, na = _two_source_specs(X, Xb, (tm, tn), tm, lambda j: j)
    return _call(
        functools.partial(_mm_res_kernel, n_a=na), (n_rows // tm, D // tn),
        [pl.BlockSpec((tm, K), lambda i, j: (i, 0)), _layer_spec(l, (K, tn), lambda i, j: (0, j))]
        + x_specs + [pl.BlockSpec((None, 1, tn), gt_idx)],
        pl.BlockSpec((tm, tn), lambda i, j: (i, j)),
        jax.ShapeDtypeStruct((n_rows, D), F32), "proj_residual")(a, b, *x_args, mods3)


def _swiglu_up_kernel(h_ref, w_ref, o_ref):
    gu = jnp.dot(h_ref[...], w_ref[...], preferred_element_type=F32)
    f = o_ref.shape[1]
    o_ref[...] = (_silu(gu[:, :f]) * gu[:, f:]).astype(o_ref.dtype)


def _swiglu_up(h, w_gu, l):
    M, D = h.shape
    f = w_gu.shape[2] // 2
    tm = _tile(M, 1024, 16)
    return _call(
        _swiglu_up_kernel, (M // tm,),
        [pl.BlockSpec((tm, D), lambda i: (i, 0)), _layer_spec(l, (D, 2 * f), lambda i: (0, 0))],
        pl.BlockSpec((tm, f), lambda i: (i, 0)),
        jax.ShapeDtypeStruct((M, f), BF16), "shared_up")(h, w_gu)


SLAB_ROWS = 16
STAGE_PITCH = 24
EXPERT_TILE = 256
HI_MASK = -65536


def _pack_pairs(v):
    half = v.shape[1] // 2
    bits = lax.bitcast_convert_type(v.astype(BF16).astype(F32), jnp.int32)
    return lax.shift_right_logical(bits[:, :half], 16) | (bits[:, half:] & HI_MASK)


def _unpack_lo(w):
    return lax.bitcast_convert_type(lax.shift_left(w, 16), F32)


def _unpack_hi(w):
    return lax.bitcast_convert_type(w & HI_MASK, F32)


def _dispatch_kernel(pos_ref, zt_ref, h_ref, xs_hbm, slab_ref, zero_ref, sem, *, tm, n_experts):
    W = slab_ref.shape[1]
    trows = EXPERT_TILE * SLAB_ROWS

    @pl.when(pl.program_id(0) == 0)
    def _():
        zero_ref[...] = jnp.zeros_like(zero_ref)
        def zcopy(t):
            return pltpu.make_async_copy(zero_ref, xs_hbm.at[pl.ds(pl.multiple_of(t * trows, trows), trows), :], sem)
        for e in range(n_experts):
            zcopy(zt_ref[0, e]).start()
        for e in range(n_experts):
            zcopy(zt_ref[0, e]).wait()
        n_used = zt_ref[0, n_experts]
        n_tiles = xs_hbm.shape[0] // trows

        def zstart(t, carry):
            zcopy(t).start()
            return carry

        def zwait(t, carry):
            zcopy(t).wait()
            return carry

        lax.fori_loop(n_used, n_tiles, zstart, 0)
        lax.fori_loop(n_used, n_tiles, zwait, 0)

    words = _pack_pairs(h_ref[...])
    for s in range(SLAB_ROWS):
        slab_ref[pl.ds(s, tm, stride=STAGE_PITCH), :] = words[:, s * W:(s + 1) * W]

    def row_copy(r, k):
        p = pos_ref[0, r * TOP_K + k]
        return pltpu.make_async_copy(
            slab_ref.at[pl.ds(pl.multiple_of(r * STAGE_PITCH, 8), SLAB_ROWS), :],
            xs_hbm.at[pl.ds(pl.multiple_of(p * SLAB_ROWS, SLAB_ROWS), SLAB_ROWS), :], sem)

    def issue(r, carry):
        for k in range(TOP_K):
            row_copy(r, k).start(priority=k % 2)
        return carry

    lax.fori_loop(0, tm, issue, 0)
    for k in range(TOP_K):
        pltpu.make_async_copy(slab_ref.at[pl.ds(0, tm * SLAB_ROWS), :],
                              xs_hbm.at[pl.ds(0, tm * SLAB_ROWS), :], sem).wait()


def _dispatch(h2, pos, zero_tiles, n_slots):
    M, D = h2.shape
    W = D // (2 * SLAB_ROWS)
    assert W == LANE, "strided slab access needs LANE-wide slab rows"
    E = zero_tiles.shape[0] - 1
    tm = _tile(M, 256, 16)
    kern = functools.partial(_dispatch_kernel, tm=tm, n_experts=E)
    return _call(
        kern, (M // tm,),
        [pl.BlockSpec((None, 1, tm * TOP_K), lambda i: (i, 0, 0), memory_space=pltpu.SMEM),
         pl.BlockSpec((1, E + 1), lambda i: (0, 0), memory_space=pltpu.SMEM),
         pl.BlockSpec((tm, D), lambda i: (i, 0))],
        pl.BlockSpec(memory_space=pl.ANY),
        jax.ShapeDtypeStruct((n_slots * SLAB_ROWS, W), jnp.int32), "moe_dispatch",
        scratch=[pltpu.VMEM((tm * STAGE_PITCH, W), jnp.int32),
                 pltpu.VMEM((EXPERT_TILE * SLAB_ROWS, W), jnp.int32),
                 pltpu.SemaphoreType.DMA(())])(
            pos.reshape(M // tm, 1, tm * TOP_K), zero_tiles.reshape(1, E + 1), h2)


def _expert_kernel(te_ref, nu_ref, x_ref, wgu_ref, wdn_ref, o_ref):
    W = x_ref.shape[1]
    f = wdn_ref.shape[0]

    @pl.when(pl.program_id(0) < nu_ref[0])
    def _():
        words = [x_ref[pl.ds(s, EXPERT_TILE, stride=SLAB_ROWS), :] for s in range(SLAB_ROWS)]
        x = jnp.concatenate([_unpack_lo(w).astype(BF16) for w in words]
                            + [_unpack_hi(w).astype(BF16) for w in words], axis=1)
        gu = jnp.dot(x, wgu_ref[...], preferred_element_type=F32)
        act = (_silu(gu[:, :f]) * gu[:, f:]).astype(BF16)
        y = _pack_pairs(jnp.dot(act, wdn_ref[...], preferred_element_type=F32))
        for s in range(SLAB_ROWS):
            o_ref[pl.ds(s, EXPERT_TILE, stride=SLAB_ROWS), :] = y[:, s * W:(s + 1) * W]

    @pl.when(pl.program_id(0) >= nu_ref[0])
    def _():
        o_ref[...] = jnp.zeros_like(o_ref)


def _experts(xs, tile_expert, n_used, w_gu, w_down, l):
    _, E, D, F2 = w_gu.shape
    W = xs.shape[1]
    trows = EXPERT_TILE * SLAB_ROWS
    n_tiles = xs.shape[0] // trows
    grid_spec = pltpu.PrefetchScalarGridSpec(
        num_scalar_prefetch=2, grid=(n_tiles,),
        in_specs=[pl.BlockSpec((trows, W), lambda i, te, nu: (jnp.minimum(i, nu[0] - 1), 0)),
                  pl.BlockSpec((None, None, D, F2), lambda i, te, nu: (l, te[i], 0, 0)),
                  pl.BlockSpec((None, None, F2 // 2, D), lambda i, te, nu: (l, te[i], 0, 0))],
        out_specs=pl.BlockSpec((trows, W), lambda i, te, nu: (i, 0)))
    return pl.pallas_call(
        _expert_kernel, grid_spec=grid_spec, out_shape=jax.ShapeDtypeStruct(xs.shape, jnp.int32),
        name="moe_experts",
        compiler_params=pltpu.CompilerParams(dimension_semantics=("arbitrary",),
                                             vmem_limit_bytes=VMEM_LIMIT_BYTES))(
            tile_expert, n_used, xs, w_gu, w_down)


def _combine_kernel(pos_ref, posn_ref, wt_ref, sh_ref, x_ref, gt_ref, ys_hbm, o_ref, buf_a, buf_b, sem, *, tm):
    j = pl.program_id(0)
    W = buf_a.shape[1]
    half = SLAB_ROWS * W

    def issue(pref, first_tok, buf, s_idx):
        def body(r, carry):
            for k in range(TOP_K):
                p = pref[0, (first_tok + r) * TOP_K + k]
                pltpu.make_async_copy(
                    ys_hbm.at[pl.ds(pl.multiple_of(p * SLAB_ROWS, SLAB_ROWS), SLAB_ROWS), :],
                    buf.at[pl.ds(pl.multiple_of((k * tm + r) * STAGE_PITCH, 8), SLAB_ROWS), :],
                    sem.at[s_idx]).start()
            return carry
        lax.fori_loop(0, tm, body, 0)

    def drain(buf, s_idx):
        for k in range(TOP_K):
            pltpu.make_async_copy(ys_hbm.at[pl.ds(0, tm * SLAB_ROWS), :],
                                  buf.at[pl.ds(0, tm * SLAB_ROWS), :], sem.at[s_idx]).wait()

    def reduce(buf, r0):
        rows = pl.ds(r0, tm)
        wt = wt_ref[rows, :]
        wk = [wt[:, k:k + 1] for k in range(TOP_K)]
        for s in range(SLAB_ROWS):
            lo = hi = None
            for k in range(TOP_K):
                w = buf[pl.ds(k * tm * STAGE_PITCH + s, tm, stride=STAGE_PITCH), :]
                lo_k, hi_k = wk[k] * _unpack_lo(w), wk[k] * _unpack_hi(w)
                lo, hi = (lo_k, hi_k) if lo is None else (lo + lo_k, hi + hi_k)
            for c0, routed in ((s * W, lo), (half + s * W, hi)):
                cols = slice(c0, c0 + W)
                o_ref[rows, cols] = x_ref[rows, cols] + gt_ref[:, cols] * (sh_ref[rows, cols].astype(F32) + routed)

    @pl.when(j == 0)
    def _():
        issue(pos_ref, 0, buf_a, 0)

    issue(pos_ref, tm, buf_b, 1)
    drain(buf_a, 0)
    reduce(buf_a, 0)

    @pl.when(j + 1 < pl.num_programs(0))
    def _():
        issue(posn_ref, 0, buf_a, 0)

    drain(buf_b, 1)
    reduce(buf_b, tm)


def _combine(ys, pos, wt, sh, X, mods3, tok, l, which, n_rows):
    D = X.shape[1]
    W = ys.shape[1]
    E = wt.shape[1]
    tm = _tile(n_rows // 2, 128, 16, tok.N // 2)
    tb = 2 * tm
    n_steps = n_rows // tb
    pos3 = pos.reshape(n_steps, 1, tb * TOP_K)

    def gt_idx(j):
        row = tok.mod_row(j * tb)
        return ((l * MOD_ROWS + row) * N_MOD + which, 0, 0)

    kern = functools.partial(_combine_kernel, tm=tm)
    return _call(
        kern, (n_steps,),
        [pl.BlockSpec((None, 1, tb * TOP_K), lambda j: (j, 0, 0), memory_space=pltpu.SMEM),
         pl.BlockSpec((None, 1, tb * TOP_K), lambda j: (jnp.minimum(j + 1, n_steps - 1), 0, 0),
                      memory_space=pltpu.SMEM),
         pl.BlockSpec((tb, E), lambda j: (j, 0)),
         pl.BlockSpec((tb, D), lambda j: (j, 0)),
         pl.BlockSpec((tb, D), lambda j: (j, 0)),
         pl.BlockSpec((None, 1, D), gt_idx),
         pl.BlockSpec(memory_space=pl.ANY)],
        pl.BlockSpec((tb, D), lambda j: (j, 0)),
        jax.ShapeDtypeStruct((n_rows, D), F32), "moe_combine",
        scratch=[pltpu.VMEM((TOP_K * tm * STAGE_PITCH, W), jnp.int32),
                 pltpu.VMEM((TOP_K * tm * STAGE_PITCH, W), jnp.int32),
                 pltpu.SemaphoreType.DMA((2,))])(pos3, pos3, wt, sh, X, mods3, ys)


def _routing_plan(idx, rank, counts, n_tiles):
    E = counts.shape[0]
    padded = (counts + EXPERT_TILE - 1) // EXPERT_TILE * EXPERT_TILE
    ends = jnp.cumsum(padded)
    base = ends - padded
    experts = jnp.arange(E, dtype=jnp.int32)
    pos = jnp.sum(jnp.where(idx[..., None] == experts, base, 0), axis=-1) + rank
    tile_start = jnp.arange(n_tiles, dtype=jnp.int32) * EXPERT_TILE
    tile_expert = jnp.minimum(jnp.sum((ends[None, :] <= tile_start[:, None]).astype(jnp.int32), axis=1), E - 1)
    n_used = (ends[-1] // EXPERT_TILE).reshape(1)
    last_tile = jnp.maximum(ends // EXPERT_TILE - 1, 0)
    return pos.astype(jnp.int32), tile_expert.astype(jnp.int32), n_used.astype(jnp.int32), last_tile.astype(jnp.int32)


def kernel(x, c, ctx, c_ctx, w_mod, b_mod, g_mix, g_ffn, w_in, mla_q_norm_g, mla_kv_norm_g, w_uq, w_ukv,
           mla_q_head_g, mla_k_head_g, gqa_q_g, gqa_k_g, conv_w, w_pa, w_pb, w_pc, w_o, w_router, b_router,
           w_exp_gu, w_exp_down, w_sh_gu, w_sh_down):
    B, N, D = x.shape
    CTX = ctx.shape[1]
    L = w_mod.shape[0]
    QL = mla_q_norm_g.shape[1]
    KVL = mla_kv_norm_g.shape[1]
    HA = w_uq.shape[2] // MLA_QK
    HG = w_pc.shape[1] // GQA_HD
    CW = conv_w.shape[2]
    D_IN = w_in.shape[2]
    HKV = (D_IN - (KVL + MLA_ROPE + QL + HG * GQA_HD + 3 * CW + 3 * D)) // (2 * GQA_HD)
    GKV = HKV * GQA_HD
    E = w_router.shape[2]

    tok = _Tok(B, N, CTX)
    n_lat, n_ctx = tok.n_lat, tok.n_ctx

    KV_COLS = KVL + 2 * GKV + LANE
    col_gk, col_gv, col_kr = KVL, KVL + GKV, KVL + 2 * GKV
    main0 = KVL + MLA_ROPE + 2 * GKV
    col_cq, col_gq = 0, QL
    col_b, col_c, col_u = QL + HG * GQA_HD, QL + HG * GQA_HD + CW, QL + HG * GQA_HD + 2 * CW
    col_gate = col_u + CW

    cvec = jnp.zeros((MOD_ROWS, D), F32).at[:B].set(c).at[B].set(c_ctx)
    mods = _mods(cvec, w_mod, b_mod)
    mods3 = mods.reshape(L * MOD_ROWS * N_MOD, 1, D)

    cos_a, sin_a = _rope_tables(N, MLA_ROPE)
    cos_g, sin_g = _rope_tables(N, GQA_HD)

    w_pa_b, w_pb_b, w_pc_b, w_o_b = (w.astype(BF16) for w in (w_pa, w_pb, w_pc, w_o))
    w_exp_gu_b, w_exp_down_b = w_exp_gu.astype(BF16), w_exp_down.astype(BF16)
    w_sh_gu_b, w_sh_down_b = w_sh_gu.astype(BF16), w_sh_down.astype(BF16)
    w_main_b = w_in[:, :, main0:].astype(BF16)

    X, Xb = x.reshape(n_lat, D), ctx.reshape(n_ctx, D)

    for l in range(L):
        last = l == L - 1
        n_rows = n_lat if last else n_lat + n_ctx

        w_kv = jnp.concatenate(
            [w_in[l, :, :KVL].astype(BF16), w_in[l, :, KVL + MLA_ROPE:main0].astype(BF16),
             w_in[l, :, KVL:KVL + MLA_ROPE].astype(BF16), jnp.zeros((D, LANE - MLA_ROPE), BF16)], axis=1)
        w_uq_p = jnp.pad(w_uq[l].reshape(QL, HA, MLA_QK),
                         ((0, 0), (0, 0), (0, MLA_QK_PAD - MLA_QK))).reshape(QL, HA * MLA_QK_PAD).astype(BF16)
        w_ukv_r = w_ukv[l].reshape(KVL, HA, MLA_NOPE + MLA_V)
        w_ukv_p = jnp.concatenate([w_ukv_r[:, :, :MLA_NOPE].reshape(KVL, HA * MLA_NOPE),
                                   w_ukv_r[:, :, MLA_NOPE:].reshape(KVL, HA * MLA_V)], axis=1).astype(BF16)
        g_qa = jnp.pad(mla_q_head_g[l] * (1.0 / math.sqrt(MLA_QK)), (0, MLA_QK_PAD - MLA_QK))
        g_ka = jnp.pad(mla_k_head_g[l], (0, MLA_QK_PAD - MLA_QK))
        g_qg = gqa_q_g[l] * (1.0 / math.sqrt(GQA_HD))
        g_kg = gqa_k_g[l]

        h = _norm_mod(X, 0, n_rows, g_mix[l], mods3, tok, l, 0, 0, Xb=None if last else Xb)
        px = _mm(h, w_main_b, name="in_proj", layer=l)
        if last:
            c_src, c_off = (X, n_lat) if Xb is None else (Xb, 0)
            hc = _norm_mod(c_src, c_off, n_ctx, g_mix[l], mods3, tok, l, 0, n_lat)
            pkv_x = _mm(h, w_kv, tn_pref=KV_COLS, name="in_proj_kv")
            pkv_c = _mm(hc, w_kv, tn_pref=KV_COLS, name="in_proj_kv")
            kv_parts = [(pkv_x, n_lat, 0), (pkv_c, n_ctx, n_lat)]
        else:
            pkv = _mm(h, w_kv, tn_pref=KV_COLS, name="in_proj_kv")
            kv_parts = [(pkv, n_rows, 0)]

        kvr = [_norm_mm(p, 0, KVL, mla_kv_norm_g[l], w_ukv_p, "mla_kv_up") for p, _, _ in kv_parts]
        pk = [p for p, _, _ in kv_parts]
        q_raw = _norm_mm(px, col_cq, QL, mla_q_norm_g[l], w_uq_p, "mla_q_up")

        ctx_i = len(kv_parts) - 1
        ctx_off = 0 if last else n_lat
        vA0 = HA * MLA_NOPE // MLA_V

        def mla_src(i, row_off, n_k, rope):
            return ([(kvr[i], lambda hk: hk), (pk[i], lambda hk: col_kr // LANE)], (kvr[i], vA0), row_off, n_k, rope)

        def gqa_src(i, row_off, n_k, rope):
            return ([(pk[i], lambda hk: col_gk // LANE + hk)], (pk[i], col_gv // GQA_HD), row_off, n_k, rope)

        mla_args = (g_qa, g_ka, cos_a, sin_a, 1, MLA_QK, MLA_ROPE // 4, B, HA, 1, MLA_V)
        gqa_args = (g_qg, g_kg, cos_g, sin_g, 0, GQA_HD, GQA_HD // 4, B, HKV, HG // HKV, GQA_HD)
        oa = _attend(q_raw, 0, 0, N, True, [mla_src(ctx_i, ctx_off, CTX, False), mla_src(0, 0, N, True)],
                     *mla_args, "mla_attn")
        og = _attend(px, 0, col_gq // GQA_HD, N, True, [gqa_src(ctx_i, ctx_off, CTX, False), gqa_src(0, 0, N, True)],
                     *gqa_args, "gqa_attn")
        ob = _conv(px, 0, B, N, col_b, col_c, col_u, CW, conv_w[l])
        oa_c = og_c = ob_c = None
        if not last:
            oa_c = _attend(q_raw, n_lat, 0, CTX, False, [mla_src(0, n_lat, CTX, False)], *mla_args, "mla_attn_ctx")
            og_c = _attend(px, n_lat, col_gq // GQA_HD, CTX, False, [gqa_src(0, n_lat, CTX, False)],
                           *gqa_args, "gqa_attn_ctx")
            ob_c = _conv(px, n_lat, B, CTX, col_b, col_c, col_u, CW, conv_w[l])

        hm = _merge((oa, oa_c), (ob, ob_c), (og, og_c), w_pa_b, w_pb_b, w_pc_b, l, px, col_gate, n_rows)
        X = _mm_res(hm, w_o_b, X, mods3, tok, l, 2, n_rows, Xb=None if last else Xb)
        Xb = None

        h2, idx_m, wt_m, rank_m, cnt = _norm_router(X, n_rows, g_ffn[l], mods3, tok, l, w_router[l], b_router[l])
        n_tiles = n_rows * TOP_K // EXPERT_TILE + E
        pos, tile_expert, n_used, last_tile = _routing_plan(idx_m[:, :TOP_K], rank_m[:, :TOP_K], cnt[0], n_tiles)
        xs = _dispatch(h2, pos, jnp.concatenate([last_tile, n_used]), n_tiles * EXPERT_TILE)
        ys = _experts(xs, tile_expert, n_used, w_exp_gu_b, w_exp_down_b, l)
        sh = _mm(_swiglu_up(h2, w_sh_gu_b, l), w_sh_down_b, name="shared_down", layer=l)
        X = _combine(ys, pos, wt_m, sh, X, mods3, tok, l, 5, n_rows)

    return X.reshape(B, N, D)
```

```python
import functools
import math

import jax
import jax.numpy as jnp
from jax import lax
from jax.experimental import pallas as pl
from jax.experimental.pallas import tpu as pltpu

F32 = jnp.float32
BF16 = jnp.bfloat16

GRID_W = 64
ROPE_BASE = 10000.0
EPS = 1e-6
MLA_NOPE = 128
MLA_ROPE = 64
MLA_QK = MLA_NOPE + MLA_ROPE
MLA_V = 128
GQA_HD = 128
TOP_K = 4
ROUTE_SCALE = 2.5
N_MOD = 6

LANE = 128
MLA_QK_PAD = 2 * LANE
MOD_ROWS = 16
VMEM_LIMIT_BYTES = 56 * 1024 * 1024


def _tile(dim, pref, unit, *also):
    t = min(pref, dim) // unit * unit
    while t > unit:
        if dim % t == 0 and all(a % t == 0 for a in also):
            return t
        t -= unit
    assert dim % unit == 0 and all(a % unit == 0 for a in also), (dim, unit, also)
    return unit


def _call(kernel, grid, in_specs, out_specs, out_shape, name, scratch=()):
    return pl.pallas_call(
        kernel, grid=grid, in_specs=in_specs, out_specs=out_specs, out_shape=out_shape,
        scratch_shapes=list(scratch), name=name,
        compiler_params=pltpu.CompilerParams(
            dimension_semantics=("arbitrary",) * len(grid), vmem_limit_bytes=VMEM_LIMIT_BYTES))


def _silu(v):
    return v * jax.nn.sigmoid(v)


def _layer_spec(l, block, index):
    return pl.BlockSpec((None,) + tuple(block), lambda *g: (l,) + tuple(index(*g)))


def _mod_kernel(c_ref, w_ref, b_ref, o_ref):
    s = _silu(c_ref[...]).astype(BF16)
    o_ref[0] = jnp.dot(s, w_ref[0].astype(BF16), preferred_element_type=F32) + b_ref[0]


def _mods(cvec, w_mod, b_mod):
    L, D, N6 = w_mod.shape
    tn = _tile(N6, 512, LANE)
    return _call(
        _mod_kernel, (L, N6 // tn),
        [pl.BlockSpec((MOD_ROWS, D), lambda l, j: (0, 0)),
         pl.BlockSpec((1, D, tn), lambda l, j: (l, 0, j)),
         pl.BlockSpec((1, 1, tn), lambda l, j: (l, 0, j))],
        pl.BlockSpec((1, MOD_ROWS, tn), lambda l, j: (l, 0, j)),
        jax.ShapeDtypeStruct((L, MOD_ROWS, N6), F32), "adaln_mod")(cvec, w_mod, b_mod.reshape(L, 1, N6))


class _Tok:
    def __init__(self, B, N, CTX):
        self.B, self.N, self.CTX = B, N, CTX
        self.n_lat = B * N
        self.n_ctx = B * CTX

    def mod_row(self, tok0):
        return jnp.where(tok0 < self.n_lat, tok0 // self.N, self.B)

    def pos_block(self, tok0, tm):
        return jnp.where(tok0 < self.n_lat, (tok0 % self.N) // tm, self.N // tm)


def _mod_spec(tok, l, which, tm, tok_off, D, ngrid):
    def idx(i, *_):
        row = tok.mod_row(i * tm + tok_off)
        return ((l * MOD_ROWS + row) * N_MOD + which, 0, 0)
    return pl.BlockSpec((None, 1, D), idx)


def _two_source_specs(X, Xb, block, row_tile, col_of):
    na = X.shape[0] // row_tile
    return ([pl.BlockSpec(block, lambda i, *r: (jnp.minimum(i, na - 1), col_of(*r))),
             pl.BlockSpec(block, lambda i, *r: (jnp.maximum(i - na, 0), col_of(*r)))], [X, Xb], na)


def _norm_mod_kernel(*refs, n_a):
    x_refs, (g_ref, sh_ref, sc_ref, o_ref) = refs[:-4], refs[-4:]
    x = x_refs[0][...]
    if len(x_refs) == 2:
        x = jnp.where(pl.program_id(0) < n_a, x, x_refs[1][...])
    r = lax.rsqrt(jnp.mean(x * x, axis=-1, keepdims=True) + EPS)
    o_ref[...] = ((x * r * g_ref[...]) * (1.0 + sc_ref[...]) + sh_ref[...]).astype(o_ref.dtype)


def _norm_mod(X, row_off, n_rows, g, mods3, tok, l, which_shift, tok_off, Xb=None):
    D = X.shape[1]
    tm = _tile(n_rows, 256, 16, tok.N, row_off) if row_off else _tile(n_rows, 256, 16, tok.N)
    ro = row_off // tm
    if Xb is None:
        x_specs, x_args, na = [pl.BlockSpec((tm, D), lambda i: (i + ro, 0))], [X], 0
    else:
        assert row_off == 0 and X.shape[0] % tm == 0
        x_specs, x_args, na = _two_source_specs(X, Xb, (tm, D), tm, lambda: 0)
    return _call(
        functools.partial(_norm_mod_kernel, n_a=na), (n_rows // tm,),
        x_specs + [pl.BlockSpec((1, D), lambda i: (0, 0)),
                   _mod_spec(tok, l, which_shift, tm, tok_off, D, 1),
                   _mod_spec(tok, l, which_shift + 1, tm, tok_off, D, 1)],
        pl.BlockSpec((tm, D), lambda i: (i, 0)),
        jax.ShapeDtypeStruct((n_rows, D), BF16), "norm_mod")(*x_args, g.reshape(1, D), mods3, mods3)


def _norm_router_kernel(x_ref, g_ref, sh_ref, sc_ref, wr_ref, br_ref,
                        h_ref, idx_ref, wt_ref, rank_ref, cnt_ref, carry_ref):
    @pl.when(pl.program_id(0) == 0)
    def _():
        carry_ref[...] = jnp.zeros_like(carry_ref)

    x = x_ref[...]
    r = lax.rsqrt(jnp.mean(x * x, axis=-1, keepdims=True) + EPS)
    h = (x * r * g_ref[...]) * (1.0 + sc_ref[...]) + sh_ref[...]
    h_ref[...] = h.astype(h_ref.dtype)
    n_e = br_ref.shape[1]
    h_hi = h.astype(BF16)
    h_lo = (h - h_hi.astype(F32)).astype(BF16)
    w_both = wr_ref[...]
    first = jnp.dot(h_hi, w_both, preferred_element_type=F32)
    logits = first[:, :n_e] + first[:, n_e:] + jnp.dot(h_lo, w_both[:, :n_e], preferred_element_type=F32)
    s = jax.nn.sigmoid(logits)
    sel = s + br_ref[...]
    tm, n_e = sel.shape
    lane = lax.broadcasted_iota(jnp.int32, sel.shape, 1).astype(F32)
    picked = jnp.zeros(sel.shape, F32)
    firsts = []
    for _ in range(TOP_K):
        cur = jnp.where(picked > 0.0, -jnp.inf, sel)
        m = jnp.max(cur, axis=-1, keepdims=True)
        first = jnp.min(jnp.where(cur == m, lane, float(n_e)), axis=-1, keepdims=True)
        picked = jnp.where(lane == first, 1.0, picked)
        firsts.append(first)
    w = picked * s
    gate = w / jnp.sum(w, axis=-1, keepdims=True) * ROUTE_SCALE

    earlier = (lax.broadcasted_iota(jnp.int32, (tm, tm), 1) < lax.broadcasted_iota(jnp.int32, (tm, tm), 0))
    cum = jnp.dot(earlier.astype(BF16), picked.astype(BF16), preferred_element_type=F32) + carry_ref[...]
    idx_m = jnp.zeros(sel.shape, F32)
    wt_m = jnp.zeros(sel.shape, F32)
    rank_m = jnp.zeros(sel.shape, F32)
    for k, first in enumerate(firsts):
        mine = lane == first
        idx_m = jnp.where(lane == float(k), first, idx_m)
        wt_m = jnp.where(lane == float(k), jnp.sum(jnp.where(mine, gate, 0.0), axis=-1, keepdims=True), wt_m)
        rank_m = jnp.where(lane == float(k), jnp.sum(jnp.where(mine, cum, 0.0), axis=-1, keepdims=True), rank_m)
    idx_ref[...] = idx_m.astype(jnp.int32)
    wt_ref[...] = wt_m
    rank_ref[...] = rank_m.astype(jnp.int32)
    carry_ref[...] += jnp.sum(picked, axis=0, keepdims=True)
    cnt_ref[...] = carry_ref[...].astype(jnp.int32)


def _norm_router(X, n_rows, g, mods3, tok, l, w_router, b_router):
    D = X.shape[1]
    E = w_router.shape[1]
    w_hi = w_router.astype(BF16)
    w_both = jnp.concatenate([w_hi, (w_router - w_hi.astype(F32)).astype(BF16)], axis=1)
    tm = _tile(n_rows, 256, 16, tok.N)
    row = lambda i: (i, 0)
    fixed = lambda i: (0, 0)
    return _call(
        _norm_router_kernel, (n_rows // tm,),
        [pl.BlockSpec((tm, D), row), pl.BlockSpec((1, D), fixed),
         _mod_spec(tok, l, 3, tm, 0, D, 1), _mod_spec(tok, l, 4, tm, 0, D, 1),
         pl.BlockSpec((D, 2 * E), fixed), pl.BlockSpec((1, E), fixed)],
        [pl.BlockSpec((tm, D), row), pl.BlockSpec((tm, E), row), pl.BlockSpec((tm, E), row),
         pl.BlockSpec((tm, E), row), pl.BlockSpec((1, E), fixed)],
        [jax.ShapeDtypeStruct((n_rows, D), BF16), jax.ShapeDtypeStruct((n_rows, E), jnp.int32),
         jax.ShapeDtypeStruct((n_rows, E), F32), jax.ShapeDtypeStruct((n_rows, E), jnp.int32),
         jax.ShapeDtypeStruct((1, E), jnp.int32)],
        "norm_router", scratch=[pltpu.VMEM((1, E), F32)])(
            X, g.reshape(1, D), mods3, mods3, w_both, b_router.reshape(1, E))


def _col_cast_kernel(w_ref, o_ref, *, col0):
    o_ref[...] = w_ref[:, col0:col0 + o_ref.shape[1]].astype(o_ref.dtype)


def _col_cast(w, col0, n_cols):
    L, K, N = w.shape
    tr = _tile(K, 128, 16)
    return _call(
        functools.partial(_col_cast_kernel, col0=col0), (L, K // tr),
        [pl.BlockSpec((None, tr, N), lambda l, i: (l, i, 0))],
        pl.BlockSpec((None, tr, n_cols), lambda l, i: (l, i, 0)),
        jax.ShapeDtypeStruct((L, K, n_cols), BF16), "weight_cols")(w)


def _mm_kernel(a_ref, b_ref, o_ref):
    o_ref[...] = jnp.dot(a_ref[...], b_ref[...], preferred_element_type=F32).astype(o_ref.dtype)


def _mm(a, b, tm_pref=1024, tn_pref=1024, name="mm", layer=None):
    M, K = a.shape
    N = b.shape[-1]
    tm = _tile(M, tm_pref, 16)
    tn = _tile(N, tn_pref, LANE)
    b_spec = (pl.BlockSpec((K, tn), lambda i, j: (0, j)) if layer is None
              else _layer_spec(layer, (K, tn), lambda i, j: (0, j)))
    return _call(
        _mm_kernel, (M // tm, N // tn),
        [pl.BlockSpec((tm, K), lambda i, j: (i, 0)), b_spec],
        pl.BlockSpec((tm, tn), lambda i, j: (i, j)),
        jax.ShapeDtypeStruct((M, N), BF16), name)(a, b)


def _norm_mm_kernel(a_ref, g_ref, b_ref, o_ref):
    a = a_ref[...].astype(F32)
    r = lax.rsqrt(jnp.mean(a * a, axis=-1, keepdims=True) + EPS)
    an = (a * r * g_ref[...]).astype(BF16)
    o_ref[...] = jnp.dot(an, b_ref[...], preferred_element_type=F32).astype(o_ref.dtype)


def _norm_mm(a, col_off, K, g, b, name):
    M = a.shape[0]
    N = b.shape[1]
    tm = _tile(M, 512, 16)
    cb = col_off // K
    assert col_off % K == 0
    return _call(
        _norm_mm_kernel, (M // tm,),
        [pl.BlockSpec((tm, K), lambda i: (i, cb)),
         pl.BlockSpec((1, K), lambda i: (0, 0)),
         pl.BlockSpec((K, N), lambda i: (0, 0))],
        pl.BlockSpec((tm, N), lambda i: (i, 0)),
        jax.ShapeDtypeStruct((M, N), BF16), name)(a, g.reshape(1, K), b)


def _prep_head(xs, g_ref, rope, rope_chunk, inv_d, shift):
    ones = jnp.ones((LANE, LANE), BF16)
    ssq = sum(jnp.dot((x * x).astype(BF16), ones, preferred_element_type=F32) for x in xs)
    r = lax.rsqrt(ssq * inv_d + EPS)
    out = []
    for c, x in enumerate(xs):
        y = x * r * g_ref[:, c * LANE:(c + 1) * LANE]
        if rope is not None and c == rope_chunk:
            cos, sin = rope
            src = lax.broadcasted_iota(jnp.int32, (LANE, LANE), 0)
            dst = lax.broadcasted_iota(jnp.int32, (LANE, LANE), 1)
            partner = jnp.where((dst % (2 * shift)) < shift, dst + shift, dst - shift)
            swap = (src == partner).astype(BF16)
            y = y * cos + jnp.dot(y.astype(BF16), swap, preferred_element_type=F32) * sin
        out.append(y.astype(BF16))
    return out


def _rope_tables(n_pos, rope_dim):
    nf = rope_dim // 4
    t = jnp.arange(n_pos, dtype=jnp.int32)
    row = (t // GRID_W).astype(F32)
    col = (t % GRID_W).astype(F32)
    inv = ROPE_BASE ** (-jnp.arange(nf, dtype=F32) / nf)
    a_row, a_col = row[:, None] * inv, col[:, None] * inv
    cos = jnp.concatenate([jnp.cos(a_row)] * 2 + [jnp.cos(a_col)] * 2, axis=-1)
    sin = jnp.concatenate([-jnp.sin(a_row), jnp.sin(a_row), -jnp.sin(a_col), jnp.sin(a_col)], axis=-1)
    cos = jnp.pad(cos, ((0, 0), (0, LANE - rope_dim)), constant_values=1.0)
    sin = jnp.pad(sin, ((0, 0), (0, LANE - rope_dim)))
    return cos, sin


def _attn_kernel(*refs, n_chunks, src_cfg, q_rope, keys_once, rope_chunk, inv_d, shift, tq, n_q):
    it = iter(refs)
    q_ref = next(it)
    srcs = [([next(it) for _ in range(n_chunks)], next(it)) for _ in src_cfg]
    gq_ref, gk_ref, cos_ref, sin_ref, o_ref, k_scr = (next(it) for _ in range(6))

    def prep(chunk_refs, rows, g_ref, rope):
        xs = [r[rows, :].astype(F32) for r in chunk_refs]
        tab = (cos_ref[rows, :], sin_ref[rows, :]) if rope else None
        return _prep_head(xs, g_ref, tab, rope_chunk, inv_d, shift)

    def prep_keys():
        off = 0
        for (k_chunks, _), (n_k, rope) in zip(srcs, src_cfg):
            bk = min(n_k, 512)
            for r0 in range(0, n_k, bk):
                for c, y in enumerate(prep(k_chunks, pl.ds(r0, bk), gk_ref, rope)):
                    k_scr[pl.ds(off + r0, bk), c * LANE:(c + 1) * LANE] = y
            off += n_k

    if keys_once:
        pl.when(pl.program_id(2) == 0)(prep_keys)
    else:
        prep_keys()

    def scores(c):
        q = jnp.concatenate(prep([q_ref.at[:, j * LANE:(j + 1) * LANE] for j in range(n_chunks)],
                                 pl.ds(c * tq, tq), gq_ref, q_rope), axis=1)
        return lax.dot_general(q, k_scr[...], (((1,), (1,)), ((), ())), preferred_element_type=F32)

    n_blocks = n_q // tq
    ahead = 1
    pending = [scores(c) for c in range(min(ahead, n_blocks))]
    for c in range(n_blocks):
        rows = pl.ds(c * tq, tq)
        s = pending.pop(0)
        if c + ahead < n_blocks:
            pending.append(scores(c + ahead))
        p = jnp.exp(s - jnp.max(s, axis=-1, keepdims=True))
        l = jnp.sum(p, axis=-1, keepdims=True)
        pb = p.astype(BF16)
        o, off = None, 0
        for (_, v_ref), (n_k, _) in zip(srcs, src_cfg):
            o_src = jnp.dot(pb[:, off:off + n_k], v_ref[...], preferred_element_type=F32)
            o = o_src if o is None else o + o_src
            off += n_k
        o_ref[rows, :] = (o * (1.0 / l)).astype(o_ref.dtype)


def _attend(q_arr, q_row_off, q_cb0, n_q, q_rope, srcs, g_q, g_k, cos, sin, rope_chunk, norm_dim, shift,
            B, n_kv_heads, group, dv, name):
    n_chunks = len(srcs[0][0])
    qw = n_chunks * LANE
    tq = _tile(n_q, 512, 16)
    assert q_row_off % n_q == 0
    qb = q_row_off // n_q
    in_specs = [pl.BlockSpec((n_q, qw), lambda b, hk, g: (qb + b, q_cb0 + hk * group + g))]
    args = [q_arr]
    src_cfg = []
    for k_chunks, (v_arr, v_cb0), row_off, n_k, rope in srcs:
        assert row_off % n_k == 0 and n_k % min(n_k, 512) == 0
        rb = row_off // n_k
        for arr, cb_fn in k_chunks:
            in_specs.append(pl.BlockSpec((n_k, LANE), (lambda b, hk, g, rb=rb, f=cb_fn: (rb + b, f(hk)))))
            args.append(arr)
        in_specs.append(pl.BlockSpec((n_k, dv), (lambda b, hk, g, rb=rb, c0=v_cb0: (rb + b, c0 + hk))))
        args.append(v_arr)
        src_cfg.append((n_k, rope))
    fixed = lambda b, hk, g: (0, 0)
    in_specs += [pl.BlockSpec((1, qw), fixed), pl.BlockSpec((1, qw), fixed),
                 pl.BlockSpec(cos.shape, fixed), pl.BlockSpec(sin.shape, fixed)]
    args += [g_q.reshape(1, qw), g_k.reshape(1, qw), cos, sin]
    assert cos.shape[0] >= max([n_q] + [n_k for n_k, rope in src_cfg if rope])
    n_all = sum(n_k for n_k, _ in src_cfg)
    kern = functools.partial(_attn_kernel, n_chunks=n_chunks, src_cfg=tuple(src_cfg), q_rope=q_rope,
                             keys_once=group > 1, rope_chunk=rope_chunk, inv_d=1.0 / norm_dim, shift=shift,
                             tq=tq, n_q=n_q)
    return _call(
        kern, (B, n_kv_heads, group), in_specs,
        pl.BlockSpec((n_q, dv), lambda b, hk, g: (b, hk * group + g)),
        jax.ShapeDtypeStruct((B * n_q, n_kv_heads * group * dv), BF16), name,
        scratch=[pltpu.VMEM((n_all, qw), BF16)])(*args)


def _conv_kernel(b_ref, c_ref, u_ref, w_ref, o_ref):
    v = c_ref[...].astype(F32) * u_ref[...].astype(F32)
    n = v.shape[0]
    row = lax.broadcasted_iota(jnp.int32, v.shape, 0)
    prev = jnp.where(row == 0, 0.0, pltpu.roll(v, 1, 0))
    nxt = jnp.where(row == n - 1, 0.0, pltpu.roll(v, n - 1, 0))
    y = prev * w_ref[0:1, :] + v * w_ref[1:2, :] + nxt * w_ref[2:3, :]
    o_ref[...] = (b_ref[...].astype(F32) * y).astype(o_ref.dtype)


def _conv(px, row_off, n_seq, seq_len, col_b, col_c, col_u, CW, conv_w):
    tc = _tile(CW, 512, LANE, col_b, col_c, col_u)
    rb = row_off // seq_len
    assert row_off % seq_len == 0

    def spec(col):
        return pl.BlockSpec((seq_len, tc), lambda s, j, col=col: (rb + s, col // tc + j))

    return _call(
        _conv_kernel, (n_seq, CW // tc),
        [spec(col_b), spec(col_c), spec(col_u), pl.BlockSpec((3, tc), lambda s, j: (0, j))],
        pl.BlockSpec((seq_len, tc), lambda s, j: (s, j)),
        jax.ShapeDtypeStruct((n_seq * seq_len, CW), BF16), "short_conv")(px, px, px, conv_w)


def _merge_kernel(*refs, n_a, n_src):
    o_refs, (wa_ref, wb_ref, wc_ref, ga_ref, gb_ref, gc_ref, o_ref) = refs[:3 * n_src], refs[3 * n_src:]

    def branch(k):
        o = o_refs[k * n_src][...]
        if n_src == 2:
            o = jnp.where(pl.program_id(0) < n_a, o, o_refs[k * n_src + 1][...])
        return o

    def term(o, w, g):
        return jax.nn.sigmoid(g[...].astype(F32)) * jnp.dot(o, w[...], preferred_element_type=F32)
    h = term(branch(0), wa_ref, ga_ref) + term(branch(1), wb_ref, gb_ref) + term(branch(2), wc_ref, gc_ref)
    o_ref[...] = h.astype(o_ref.dtype)


def _merge(oa, ob, oc, w_pa, w_pb, w_pc, l, px, gate_col, n_rows):
    D = w_pa.shape[2]
    tm = _tile(n_rows, 1024, 16, oa[0].shape[0])
    tn = _tile(D, 512, LANE, gate_col)
    gb0 = gate_col // tn
    nb = D // tn

    n_src = 1 if oa[1] is None else 2
    a_specs, a_args, na = [], [], 0
    for lat, ctx in (oa, ob, oc):
        if ctx is None:
            a_specs.append(pl.BlockSpec((tm, lat.shape[1]), lambda i, j: (i, 0)))
            a_args.append(lat)
        else:
            assert lat.shape[0] % tm == 0
            specs, args, na = _two_source_specs(lat, ctx, (tm, lat.shape[1]), tm, lambda j: 0)
            a_specs += specs
            a_args += args

    def w_spec(w):
        return _layer_spec(l, (w.shape[1], tn), lambda i, j: (0, j))

    def g_spec(k):
        return pl.BlockSpec((tm, tn), lambda i, j, k=k: (i, gb0 + k * nb + j))

    return _call(
        functools.partial(_merge_kernel, n_a=na, n_src=n_src), (n_rows // tm, nb),
        a_specs + [w_spec(w_pa), w_spec(w_pb), w_spec(w_pc), g_spec(0), g_spec(1), g_spec(2)],
        pl.BlockSpec((tm, tn), lambda i, j: (i, j)),
        jax.ShapeDtypeStruct((n_rows, D), BF16), "merge")(*a_args, w_pa, w_pb, w_pc, px, px, px)


def _mm_res_kernel(a_ref, b_ref, *refs, n_a):
    x_refs, (gt_ref, o_ref) = refs[:-2], refs[-2:]
    x = x_refs[0][...]
    if len(x_refs) == 2:
        x = jnp.where(pl.program_id(0) < n_a, x, x_refs[1][...])
    acc = jnp.dot(a_ref[...], b_ref[...], preferred_element_type=F32)
    o_ref[...] = x + gt_ref[...] * acc


def _mm_res(a, b, X, mods3, tok, l, which, n_rows, Xb=None):
    K = a.shape[1]
    D = b.shape[2]
    tm = _tile(n_rows, 1024, 16, tok.N)
    tn = _tile(D, 512, LANE)

    def gt_idx(i, j):
        row = tok.mod_row(i * tm)
        return ((l * MOD_ROWS + row) * N_MOD + which, 0, j)

    if Xb is None:
        x_specs, x_args, na = [pl.BlockSpec((tm, tn), lambda i, j: (i, j))], [X], 0
    else:
        assert X.shape[0] % tm == 0
        x_specs, x_args, na = _two_source_specs(X, Xb, (tm, tn), tm, lambda j: j)
    return _call(
        functools.partial(_mm_res_kernel, n_a=na), (n_rows // tm, D // tn),
        [pl.BlockSpec((tm, K), lambda i, j: (i, 0)), _layer_spec(l, (K, tn), lambda i, j: (0, j))]
        + x_specs + [pl.BlockSpec((None, 1, tn), gt_idx)],
        pl.BlockSpec((tm, tn), lambda i, j: (i, j)),
        jax.ShapeDtypeStruct((n_rows, D), F32), "proj_residual")(a, b, *x_args, mods3)


def _swiglu_up_kernel(h_ref, w_ref, o_ref):
    gu = jnp.dot(h_ref[...], w_ref[...], preferred_element_type=F32)
    f = o_ref.shape[1]
    o_ref[...] = (_silu(gu[:, :f]) * gu[:, f:]).astype(o_ref.dtype)


def _swiglu_up(h, w_gu, l):
    M, D = h.shape
    f = w_gu.shape[2] // 2
    tm = _tile(M, 1024, 16)
    return _call(
        _swiglu_up_kernel, (M // tm,),
        [pl.BlockSpec((tm, D), lambda i: (i, 0)), _layer_spec(l, (D, 2 * f), lambda i: (0, 0))],
        pl.BlockSpec((tm, f), lambda i: (i, 0)),
        jax.ShapeDtypeStruct((M, f), BF16), "shared_up")(h, w_gu)


SLAB_ROWS = 16
STAGE_PITCH = 24
EXPERT_TILE = 256
HI_MASK = -65536


def _pack_pairs(v):
    half = v.shape[1] // 2
    bits = lax.bitcast_convert_type(v.astype(BF16).astype(F32), jnp.int32)
    return lax.shift_right_logical(bits[:, :half], 16) | (bits[:, half:] & HI_MASK)


def _unpack_lo(w):
    return lax.bitcast_convert_type(lax.shift_left(w, 16), F32)


def _unpack_hi(w):
    return lax.bitcast_convert_type(w & HI_MASK, F32)


def _dispatch_kernel(pos_ref, zt_ref, h_ref, xs_hbm, slab_ref, zero_ref, sem, *, tm, n_experts):
    W = slab_ref.shape[1]
    trows = EXPERT_TILE * SLAB_ROWS

    @pl.when(pl.program_id(0) == 0)
    def _():
        zero_ref[...] = jnp.zeros_like(zero_ref)
        def zcopy(t):
            return pltpu.make_async_copy(zero_ref, xs_hbm.at[pl.ds(pl.multiple_of(t * trows, trows), trows), :], sem)
        for e in range(n_experts):
            zcopy(zt_ref[0, e]).start()
        for e in range(n_experts):
            zcopy(zt_ref[0, e]).wait()
        n_used = zt_ref[0, n_experts]
        n_tiles = xs_hbm.shape[0] // trows

        def zstart(t, carry):
            zcopy(t).start()
            return carry

        def zwait(t, carry):
            zcopy(t).wait()
            return carry

        lax.fori_loop(n_used, n_tiles, zstart, 0)
        lax.fori_loop(n_used, n_tiles, zwait, 0)

    words = _pack_pairs(h_ref[...])
    for s in range(SLAB_ROWS):
        slab_ref[pl.ds(s, tm, stride=STAGE_PITCH), :] = words[:, s * W:(s + 1) * W]

    def row_copy(r, k):
        p = pos_ref[0, r * TOP_K + k]
        return pltpu.make_async_copy(
            slab_ref.at[pl.ds(pl.multiple_of(r * STAGE_PITCH, 8), SLAB_ROWS), :],
            xs_hbm.at[pl.ds(pl.multiple_of(p * SLAB_ROWS, SLAB_ROWS), SLAB_ROWS), :], sem)

    def issue(r, carry):
        for k in range(TOP_K):
            row_copy(r, k).start(priority=k % 2)
        return carry

    lax.fori_loop(0, tm, issue, 0)
    for k in range(TOP_K):
        pltpu.make_async_copy(slab_ref.at[pl.ds(0, tm * SLAB_ROWS), :],
                              xs_hbm.at[pl.ds(0, tm * SLAB_ROWS), :], sem).wait()


def _dispatch(h2, pos, zero_tiles, n_slots):
    M, D = h2.shape
    W = D // (2 * SLAB_ROWS)
    assert W == LANE, "strided slab access needs LANE-wide slab rows"
    E = zero_tiles.shape[0] - 1
    tm = _tile(M, 256, 16)
    kern = functools.partial(_dispatch_kernel, tm=tm, n_experts=E)
    return _call(
        kern, (M // tm,),
        [pl.BlockSpec((None, 1, tm * TOP_K), lambda i: (i, 0, 0), memory_space=pltpu.SMEM),
         pl.BlockSpec((1, E + 1), lambda i: (0, 0), memory_space=pltpu.SMEM),
         pl.BlockSpec((tm, D), lambda i: (i, 0))],
        pl.BlockSpec(memory_space=pl.ANY),
        jax.ShapeDtypeStruct((n_slots * SLAB_ROWS, W), jnp.int32), "moe_dispatch",
        scratch=[pltpu.VMEM((tm * STAGE_PITCH, W), jnp.int32),
                 pltpu.VMEM((EXPERT_TILE * SLAB_ROWS, W), jnp.int32),
                 pltpu.SemaphoreType.DMA(())])(
            pos.reshape(M // tm, 1, tm * TOP_K), zero_tiles.reshape(1, E + 1), h2)


def _expert_kernel(te_ref, nu_ref, x_ref, wgu_ref, wdn_ref, o_ref):
    W = x_ref.shape[1]
    f = wdn_ref.shape[0]

    @pl.when(pl.program_id(0) < nu_ref[0])
    def _():
        words = [x_ref[pl.ds(s, EXPERT_TILE, stride=SLAB_ROWS), :] for s in range(SLAB_ROWS)]
        x = jnp.concatenate([_unpack_lo(w).astype(BF16) for w in words]
                            + [_unpack_hi(w).astype(BF16) for w in words], axis=1)
        gu = jnp.dot(x, wgu_ref[...], preferred_element_type=F32)
        act = (_silu(gu[:, :f]) * gu[:, f:]).astype(BF16)
        y = _pack_pairs(jnp.dot(act, wdn_ref[...], preferred_element_type=F32))
        for s in range(SLAB_ROWS):
            o_ref[pl.ds(s, EXPERT_TILE, stride=SLAB_ROWS), :] = y[:, s * W:(s + 1) * W]

    @pl.when(pl.program_id(0) >= nu_ref[0])
    def _():
        o_ref[...] = jnp.zeros_like(o_ref)


def _experts(xs, tile_expert, n_used, w_gu, w_down, l):
    _, E, D, F2 = w_gu.shape
    W = xs.shape[1]
    trows = EXPERT_TILE * SLAB_ROWS
    n_tiles = xs.shape[0] // trows
    grid_spec = pltpu.PrefetchScalarGridSpec(
        num_scalar_prefetch=2, grid=(n_tiles,),
        in_specs=[pl.BlockSpec((trows, W), lambda i, te, nu: (jnp.minimum(i, nu[0] - 1), 0)),
                  pl.BlockSpec((None, None, D, F2), lambda i, te, nu: (l, te[i], 0, 0)),
                  pl.BlockSpec((None, None, F2 // 2, D), lambda i, te, nu: (l, te[i], 0, 0))],
        out_specs=pl.BlockSpec((trows, W), lambda i, te, nu: (i, 0)))
    return pl.pallas_call(
        _expert_kernel, grid_spec=grid_spec, out_shape=jax.ShapeDtypeStruct(xs.shape, jnp.int32),
        name="moe_experts",
        compiler_params=pltpu.CompilerParams(dimension_semantics=("arbitrary",),
                                             vmem_limit_bytes=VMEM_LIMIT_BYTES))(
            tile_expert, n_used, xs, w_gu, w_down)


def _combine_kernel(pos_ref, posn_ref, wt_ref, sh_ref, x_ref, gt_ref, ys_hbm, o_ref, buf_a, buf_b, sem, *, tm):
    j = pl.program_id(0)
    W = buf_a.shape[1]
    half = SLAB_ROWS * W

    def issue(pref, first_tok, buf, s_idx):
        def body(r, carry):
            for k in range(TOP_K):
                p = pref[0, (first_tok + r) * TOP_K + k]
                pltpu.make_async_copy(
                    ys_hbm.at[pl.ds(pl.multiple_of(p * SLAB_ROWS, SLAB_ROWS), SLAB_ROWS), :],
                    buf.at[pl.ds(pl.multiple_of((k * tm + r) * STAGE_PITCH, 8), SLAB_ROWS), :],
                    sem.at[s_idx]).start()
            return carry
        lax.fori_loop(0, tm, body, 0)

    def drain(buf, s_idx):
        for k in range(TOP_K):
            pltpu.make_async_copy(ys_hbm.at[pl.ds(0, tm * SLAB_ROWS), :],
                                  buf.at[pl.ds(0, tm * SLAB_ROWS), :], sem.at[s_idx]).wait()

    def reduce(buf, r0):
        rows = pl.ds(r0, tm)
        wt = wt_ref[rows, :]
        wk = [wt[:, k:k + 1] for k in range(TOP_K)]
        for s in range(SLAB_ROWS):
            lo = hi = None
            for k in range(TOP_K):
                w = buf[pl.ds(k * tm * STAGE_PITCH + s, tm, stride=STAGE_PITCH), :]
                lo_k, hi_k = wk[k] * _unpack_lo(w), wk[k] * _unpack_hi(w)
                lo, hi = (lo_k, hi_k) if lo is None else (lo + lo_k, hi + hi_k)
            for c0, routed in ((s * W, lo), (half + s * W, hi)):
                cols = slice(c0, c0 + W)
                o_ref[rows, cols] = x_ref[rows, cols] + gt_ref[:, cols] * (sh_ref[rows, cols].astype(F32) + routed)

    @pl.when(j == 0)
    def _():
        issue(pos_ref, 0, buf_a, 0)

    issue(pos_ref, tm, buf_b, 1)
    drain(buf_a, 0)
    reduce(buf_a, 0)

    @pl.when(j + 1 < pl.num_programs(0))
    def _():
        issue(posn_ref, 0, buf_a, 0)

    drain(buf_b, 1)
    reduce(buf_b, tm)


def _combine(ys, pos, wt, sh, X, mods3, tok, l, which, n_rows):
    D = X.shape[1]
    W = ys.shape[1]
    E = wt.shape[1]
    tm = _tile(n_rows // 2, 128, 16, tok.N // 2)
    tb = 2 * tm
    n_steps = n_rows // tb
    pos3 = pos.reshape(n_steps, 1, tb * TOP_K)

    def gt_idx(j):
        row = tok.mod_row(j * tb)
        return ((l * MOD_ROWS + row) * N_MOD + which, 0, 0)

    kern = functools.partial(_combine_kernel, tm=tm)
    return _call(
        kern, (n_steps,),
        [pl.BlockSpec((None, 1, tb * TOP_K), lambda j: (j, 0, 0), memory_space=pltpu.SMEM),
         pl.BlockSpec((None, 1, tb * TOP_K), lambda j: (jnp.minimum(j + 1, n_steps - 1), 0, 0),
                      memory_space=pltpu.SMEM),
         pl.BlockSpec((tb, E), lambda j: (j, 0)),
         pl.BlockSpec((tb, D), lambda j: (j, 0)),
         pl.BlockSpec((tb, D), lambda j: (j, 0)),
         pl.BlockSpec((None, 1, D), gt_idx),
         pl.BlockSpec(memory_space=pl.ANY)],
        pl.BlockSpec((tb, D), lambda j: (j, 0)),
        jax.ShapeDtypeStruct((n_rows, D), F32), "moe_combine",
        scratch=[pltpu.VMEM((TOP_K * tm * STAGE_PITCH, W), jnp.int32),
                 pltpu.VMEM((TOP_K * tm * STAGE_PITCH, W), jnp.int32),
                 pltpu.SemaphoreType.DMA((2,))])(pos3, pos3, wt, sh, X, mods3, ys)


def _routing_plan(idx, rank, counts, n_tiles):
    E = counts.shape[0]
    padded = (counts + EXPERT_TILE - 1) // EXPERT_TILE * EXPERT_TILE
    ends = jnp.cumsum(padded)
    base = ends - padded
    experts = jnp.arange(E, dtype=jnp.int32)
    pos = jnp.sum(jnp.where(idx[..., None] == experts, base, 0), axis=-1) + rank
    tile_start = jnp.arange(n_tiles, dtype=jnp.int32) * EXPERT_TILE
    tile_expert = jnp.minimum(jnp.sum((ends[None, :] <= tile_start[:, None]).astype(jnp.int32), axis=1), E - 1)
    n_used = (ends[-1] // EXPERT_TILE).reshape(1)
    last_tile = jnp.maximum(ends // EXPERT_TILE - 1, 0)
    return pos.astype(jnp.int32), tile_expert.astype(jnp.int32), n_used.astype(jnp.int32), last_tile.astype(jnp.int32)


def kernel(x, c, ctx, c_ctx, w_mod, b_mod, g_mix, g_ffn, w_in, mla_q_norm_g, mla_kv_norm_g, w_uq, w_ukv,
           mla_q_head_g, mla_k_head_g, gqa_q_g, gqa_k_g, conv_w, w_pa, w_pb, w_pc, w_o, w_router, b_router,
           w_exp_gu, w_exp_down, w_sh_gu, w_sh_down):
    B, N, D = x.shape
    CTX = ctx.shape[1]
    L = w_mod.shape[0]
    QL = mla_q_norm_g.shape[1]
    KVL = mla_kv_norm_g.shape[1]
    HA = w_uq.shape[2] // MLA_QK
    HG = w_pc.shape[1] // GQA_HD
    CW = conv_w.shape[2]
    D_IN = w_in.shape[2]
    HKV = (D_IN - (KVL + MLA_ROPE + QL + HG * GQA_HD + 3 * CW + 3 * D)) // (2 * GQA_HD)
    GKV = HKV * GQA_HD
    E = w_router.shape[2]

    tok = _Tok(B, N, CTX)
    n_lat, n_ctx = tok.n_lat, tok.n_ctx

    KV_COLS = KVL + 2 * GKV + LANE
    col_gk, col_gv, col_kr = KVL, KVL + GKV, KVL + 2 * GKV
    main0 = KVL + MLA_ROPE + 2 * GKV
    col_cq, col_gq = 0, QL
    col_b, col_c, col_u = QL + HG * GQA_HD, QL + HG * GQA_HD + CW, QL + HG * GQA_HD + 2 * CW
    col_gate = col_u + CW

    cvec = jnp.zeros((MOD_ROWS, D), F32).at[:B].set(c).at[B].set(c_ctx)
    mods = _mods(cvec, w_mod, b_mod)
    mods3 = mods.reshape(L * MOD_ROWS * N_MOD, 1, D)

    cos_a, sin_a = _rope_tables(N, MLA_ROPE)
    cos_g, sin_g = _rope_tables(N, GQA_HD)

    w_pa_b, w_pb_b, w_pc_b, w_o_b = (w.astype(BF16) for w in (w_pa, w_pb, w_pc, w_o))
    w_exp_gu_b, w_exp_down_b = w_exp_gu.astype(BF16), w_exp_down.astype(BF16)
    w_sh_gu_b, w_sh_down_b = w_sh_gu.astype(BF16), w_sh_down.astype(BF16)
    w_main_b = _col_cast(w_in, main0, D_IN - main0)

    X, Xb = x.reshape(n_lat, D), ctx.reshape(n_ctx, D)

    for l in range(L):
        last = l == L - 1
        n_rows = n_lat if last else n_lat + n_ctx

        w_kv = jnp.concatenate(
            [w_in[l, :, :KVL].astype(BF16), w_in[l, :, KVL + MLA_ROPE:main0].astype(BF16),
             w_in[l, :, KVL:KVL + MLA_ROPE].astype(BF16), jnp.zeros((D, LANE - MLA_ROPE), BF16)], axis=1)
        w_uq_p = jnp.pad(w_uq[l].reshape(QL, HA, MLA_QK),
                         ((0, 0), (0, 0), (0, MLA_QK_PAD - MLA_QK))).reshape(QL, HA * MLA_QK_PAD).astype(BF16)
        w_ukv_r = w_ukv[l].reshape(KVL, HA, MLA_NOPE + MLA_V)
        w_ukv_p = jnp.concatenate([w_ukv_r[:, :, :MLA_NOPE].reshape(KVL, HA * MLA_NOPE),
                                   w_ukv_r[:, :, MLA_NOPE:].reshape(KVL, HA * MLA_V)], axis=1).astype(BF16)
        g_qa = jnp.pad(mla_q_head_g[l] * (1.0 / math.sqrt(MLA_QK)), (0, MLA_QK_PAD - MLA_QK))
        g_ka = jnp.pad(mla_k_head_g[l], (0, MLA_QK_PAD - MLA_QK))
        g_qg = gqa_q_g[l] * (1.0 / math.sqrt(GQA_HD))
        g_kg = gqa_k_g[l]

        h = _norm_mod(X, 0, n_rows, g_mix[l], mods3, tok, l, 0, 0, Xb=None if last else Xb)
        px = _mm(h, w_main_b, name="in_proj", layer=l)
        if last:
            c_src, c_off = (X, n_lat) if Xb is None else (Xb, 0)
            hc = _norm_mod(c_src, c_off, n_ctx, g_mix[l], mods3, tok, l, 0, n_lat)
            pkv_x = _mm(h, w_kv, tn_pref=KV_COLS, name="in_proj_kv")
            pkv_c = _mm(hc, w_kv, tn_pref=KV_COLS, name="in_proj_kv")
            kv_parts = [(pkv_x, n_lat, 0), (pkv_c, n_ctx, n_lat)]
        else:
            pkv = _mm(h, w_kv, tn_pref=KV_COLS, name="in_proj_kv")
            kv_parts = [(pkv, n_rows, 0)]

        kvr = [_norm_mm(p, 0, KVL, mla_kv_norm_g[l], w_ukv_p, "mla_kv_up") for p, _, _ in kv_parts]
        pk = [p for p, _, _ in kv_parts]
        q_raw = _norm_mm(px, col_cq, QL, mla_q_norm_g[l], w_uq_p, "mla_q_up")

        ctx_i = len(kv_parts) - 1
        ctx_off = 0 if last else n_lat
        vA0 = HA * MLA_NOPE // MLA_V

        def mla_src(i, row_off, n_k, rope):
            return ([(kvr[i], lambda hk: hk), (pk[i], lambda hk: col_kr // LANE)], (kvr[i], vA0), row_off, n_k, rope)

        def gqa_src(i, row_off, n_k, rope):
            return ([(pk[i], lambda hk: col_gk // LANE + hk)], (pk[i], col_gv // GQA_HD), row_off, n_k, rope)

        mla_args = (g_qa, g_ka, cos_a, sin_a, 1, MLA_QK, MLA_ROPE // 4, B, HA, 1, MLA_V)
        gqa_args = (g_qg, g_kg, cos_g, sin_g, 0, GQA_HD, GQA_HD // 4, B, HKV, HG // HKV, GQA_HD)
        oa = _attend(q_raw, 0, 0, N, True, [mla_src(ctx_i, ctx_off, CTX, False), mla_src(0, 0, N, True)],
                     *mla_args, "mla_attn")
        og = _attend(px, 0, col_gq // GQA_HD, N, True, [gqa_src(ctx_i, ctx_off, CTX, False), gqa_src(0, 0, N, True)],
                     *gqa_args, "gqa_attn")
        ob = _conv(px, 0, B, N, col_b, col_c, col_u, CW, conv_w[l])
        oa_c = og_c = ob_c = None
        if not last:
            oa_c = _attend(q_raw, n_lat, 0, CTX, False, [mla_src(0, n_lat, CTX, False)], *mla_args, "mla_attn_ctx")
            og_c = _attend(px, n_lat, col_gq // GQA_HD, CTX, False, [gqa_src(0, n_lat, CTX, False)],
                           *gqa_args, "gqa_attn_ctx")
            ob_c = _conv(px, n_lat, B, CTX, col_b, col_c, col_u, CW, conv_w[l])

        hm = _merge((oa, oa_c), (ob, ob_c), (og, og_c), w_pa_b, w_pb_b, w_pc_b, l, px, col_gate, n_rows)
        X = _mm_res(hm, w_o_b, X, mods3, tok, l, 2, n_rows, Xb=None if last else Xb)
        Xb = None

        h2, idx_m, wt_m, rank_m, cnt = _norm_router(X, n_rows, g_ffn[l], mods3, tok, l, w_router[l], b_router[l])
        n_tiles = n_rows * TOP_K // EXPERT_TILE + E
        pos, tile_expert, n_used, last_tile = _routing_plan(idx_m[:, :TOP_K], rank_m[:, :TOP_K], cnt[0], n_tiles)
        xs = _dispatch(h2, pos, jnp.concatenate([last_tile, n_used]), n_tiles * EXPERT_TILE)
        ys = _experts(xs, tile_expert, n_used, w_exp_gu_b, w_exp_down_b, l)
        sh = _mm(_swiglu_up(h2, w_sh_gu_b, l), w_sh_down_b, name="shared_down", layer=l)
        X = _combine(ys, pos, wt_m, sh, X, mods3, tok, l, 5, n_rows)

    return X.reshape(B, N, D)
```

```python
import functools
import math

import jax
import jax.numpy as jnp
from jax import lax
from jax.experimental import pallas as pl
from jax.experimental.pallas import tpu as pltpu

F32 = jnp.float32
BF16 = jnp.bfloat16

GRID_W = 64
ROPE_BASE = 10000.0
EPS = 1e-6
MLA_NOPE = 128
MLA_ROPE = 64
MLA_QK = MLA_NOPE + MLA_ROPE
MLA_V = 128
GQA_HD = 128
TOP_K = 4
ROUTE_SCALE = 2.5
N_MOD = 6

LANE = 128
MLA_QK_PAD = 2 * LANE
MOD_ROWS = 16
VMEM_LIMIT_BYTES = 56 * 1024 * 1024


def _tile(dim, pref, unit, *also):
    t = min(pref, dim) // unit * unit
    while t > unit:
        if dim % t == 0 and all(a % t == 0 for a in also):
            return t
        t -= unit
    assert dim % unit == 0 and all(a % unit == 0 for a in also), (dim, unit, also)
    return unit


def _call(kernel, grid, in_specs, out_specs, out_shape, name, scratch=()):
    return pl.pallas_call(
        kernel, grid=grid, in_specs=in_specs, out_specs=out_specs, out_shape=out_shape,
        scratch_shapes=list(scratch), name=name,
        compiler_params=pltpu.CompilerParams(
            dimension_semantics=("arbitrary",) * len(grid), vmem_limit_bytes=VMEM_LIMIT_BYTES))


def _silu(v):
    return v * jax.nn.sigmoid(v)


def _layer_spec(l, block, index):
    return pl.BlockSpec((None,) + tuple(block), lambda *g: (l,) + tuple(index(*g)))


def _mod_kernel(c_ref, w_ref, b_ref, o_ref):
    s = _silu(c_ref[...]).astype(BF16)
    o_ref[0] = jnp.dot(s, w_ref[0].astype(BF16), preferred_element_type=F32) + b_ref[0]


def _mods(cvec, w_mod, b_mod):
    L, D, N6 = w_mod.shape
    tn = _tile(N6, 512, LANE)
    return _call(
        _mod_kernel, (L, N6 // tn),
        [pl.BlockSpec((MOD_ROWS, D), lambda l, j: (0, 0)),
         pl.BlockSpec((1, D, tn), lambda l, j: (l, 0, j)),
         pl.BlockSpec((1, 1, tn), lambda l, j: (l, 0, j))],
        pl.BlockSpec((1, MOD_ROWS, tn), lambda l, j: (l, 0, j)),
        jax.ShapeDtypeStruct((L, MOD_ROWS, N6), F32), "adaln_mod")(cvec, w_mod, b_mod.reshape(L, 1, N6))


class _Tok:
    def __init__(self, B, N, CTX):
        self.B, self.N, self.CTX = B, N, CTX
        self.n_lat = B * N
        self.n_ctx = B * CTX

    def mod_row(self, tok0):
        return jnp.where(tok0 < self.n_lat, tok0 // self.N, self.B)

    def pos_block(self, tok0, tm):
        return jnp.where(tok0 < self.n_lat, (tok0 % self.N) // tm, self.N // tm)


def _mod_spec(tok, l, which, tm, tok_off, D, ngrid):
    def idx(i, *_):
        row = tok.mod_row(i * tm + tok_off)
        return ((l * MOD_ROWS + row) * N_MOD + which, 0, 0)
    return pl.BlockSpec((None, 1, D), idx)


def _two_source_specs(X, Xb, block, row_tile, col_of):
    na = X.shape[0] // row_tile
    return ([pl.BlockSpec(block, lambda i, *r: (jnp.minimum(i, na - 1), col_of(*r))),
             pl.BlockSpec(block, lambda i, *r: (jnp.maximum(i - na, 0), col_of(*r)))], [X, Xb], na)


def _norm_mod_kernel(*refs, n_a):
    x_refs, (g_ref, sh_ref, sc_ref, o_ref) = refs[:-4], refs[-4:]
    x = x_refs[0][...]
    if len(x_refs) == 2:
        x = jnp.where(pl.program_id(0) < n_a, x, x_refs[1][...])
    r = lax.rsqrt(jnp.mean(x * x, axis=-1, keepdims=True) + EPS)
    o_ref[...] = ((x * r * g_ref[...]) * (1.0 + sc_ref[...]) + sh_ref[...]).astype(o_ref.dtype)


def _norm_mod(X, row_off, n_rows, g, mods3, tok, l, which_shift, tok_off, Xb=None):
    D = X.shape[1]
    tm = _tile(n_rows, 256, 16, tok.N, row_off) if row_off else _tile(n_rows, 256, 16, tok.N)
    ro = row_off // tm
    if Xb is None:
        x_specs, x_args, na = [pl.BlockSpec((tm, D), lambda i: (i + ro, 0))], [X], 0
    else:
        assert row_off == 0 and X.shape[0] % tm == 0
        x_specs, x_args, na = _two_source_specs(X, Xb, (tm, D), tm, lambda: 0)
    return _call(
        functools.partial(_norm_mod_kernel, n_a=na), (n_rows // tm,),
        x_specs + [pl.BlockSpec((1, D), lambda i: (0, 0)),
                   _mod_spec(tok, l, which_shift, tm, tok_off, D, 1),
                   _mod_spec(tok, l, which_shift + 1, tm, tok_off, D, 1)],
        pl.BlockSpec((tm, D), lambda i: (i, 0)),
        jax.ShapeDtypeStruct((n_rows, D), BF16), "norm_mod")(*x_args, g.reshape(1, D), mods3, mods3)


def _norm_router_kernel(x_ref, g_ref, sh_ref, sc_ref, wr_ref, br_ref,
                        h_ref, idx_ref, wt_ref, rank_ref, cnt_ref, carry_ref):
    @pl.when(pl.program_id(0) == 0)
    def _():
        carry_ref[...] = jnp.zeros_like(carry_ref)

    x = x_ref[...]
    r = lax.rsqrt(jnp.mean(x * x, axis=-1, keepdims=True) + EPS)
    h = (x * r * g_ref[...]) * (1.0 + sc_ref[...]) + sh_ref[...]
    h_ref[...] = h.astype(h_ref.dtype)
    n_e = br_ref.shape[1]
    h_hi = h.astype(BF16)
    h_lo = (h - h_hi.astype(F32)).astype(BF16)
    w_both = wr_ref[...]
    first = jnp.dot(h_hi, w_both, preferred_element_type=F32)
    logits = first[:, :n_e] + first[:, n_e:] + jnp.dot(h_lo, w_both[:, :n_e], preferred_element_type=F32)
    s = jax.nn.sigmoid(logits)
    sel = s + br_ref[...]
    tm, n_e = sel.shape
    lane = lax.broadcasted_iota(jnp.int32, sel.shape, 1).astype(F32)
    picked = jnp.zeros(sel.shape, F32)
    firsts = []
    for _ in range(TOP_K):
        cur = jnp.where(picked > 0.0, -jnp.inf, sel)
        m = jnp.max(cur, axis=-1, keepdims=True)
        first = jnp.min(jnp.where(cur == m, lane, float(n_e)), axis=-1, keepdims=True)
        picked = jnp.where(lane == first, 1.0, picked)
        firsts.append(first)
    w = picked * s
    gate = w / jnp.sum(w, axis=-1, keepdims=True) * ROUTE_SCALE

    earlier = (lax.broadcasted_iota(jnp.int32, (tm, tm), 1) < lax.broadcasted_iota(jnp.int32, (tm, tm), 0))
    cum = jnp.dot(earlier.astype(BF16), picked.astype(BF16), preferred_element_type=F32) + carry_ref[...]
    idx_m = jnp.zeros(sel.shape, F32)
    wt_m = jnp.zeros(sel.shape, F32)
    rank_m = jnp.zeros(sel.shape, F32)
    for k, first in enumerate(firsts):
        mine = lane == first
        idx_m = jnp.where(lane == float(k), first, idx_m)
        wt_m = jnp.where(lane == float(k), jnp.sum(jnp.where(mine, gate, 0.0), axis=-1, keepdims=True), wt_m)
        rank_m = jnp.where(lane == float(k), jnp.sum(jnp.where(mine, cum, 0.0), axis=-1, keepdims=True), rank_m)
    idx_ref[...] = idx_m.astype(jnp.int32)
    wt_ref[...] = wt_m
    rank_ref[...] = rank_m.astype(jnp.int32)
    carry_ref[...] += jnp.sum(picked, axis=0, keepdims=True)
    cnt_ref[...] = carry_ref[...].astype(jnp.int32)


def _norm_router(X, n_rows, g, mods3, tok, l, w_router, b_router):
    D = X.shape[1]
    E = w_router.shape[1]
    w_hi = w_router.astype(BF16)
    w_both = jnp.concatenate([w_hi, (w_router - w_hi.astype(F32)).astype(BF16)], axis=1)
    tm = _tile(n_rows, 256, 16, tok.N)
    row = lambda i: (i, 0)
    fixed = lambda i: (0, 0)
    return _call(
        _norm_router_kernel, (n_rows // tm,),
        [pl.BlockSpec((tm, D), row), pl.BlockSpec((1, D), fixed),
         _mod_spec(tok, l, 3, tm, 0, D, 1), _mod_spec(tok, l, 4, tm, 0, D, 1),
         pl.BlockSpec((D, 2 * E), fixed), pl.BlockSpec((1, E), fixed)],
        [pl.BlockSpec((tm, D), row), pl.BlockSpec((tm, E), row), pl.BlockSpec((tm, E), row),
         pl.BlockSpec((tm, E), row), pl.BlockSpec((1, E), fixed)],
        [jax.ShapeDtypeStruct((n_rows, D), BF16), jax.ShapeDtypeStruct((n_rows, E), jnp.int32),
         jax.ShapeDtypeStruct((n_rows, E), F32), jax.ShapeDtypeStruct((n_rows, E), jnp.int32),
         jax.ShapeDtypeStruct((1, E), jnp.int32)],
        "norm_router", scratch=[pltpu.VMEM((1, E), F32)])(
            X, g.reshape(1, D), mods3, mods3, w_both, b_router.reshape(1, E))


def _mm_kernel(a_ref, b_ref, o_ref):
    o_ref[...] = jnp.dot(a_ref[...], b_ref[...], preferred_element_type=F32).astype(o_ref.dtype)


def _mm(a, b, tm_pref=1024, tn_pref=1024, name="mm", layer=None):
    M, K = a.shape
    N = b.shape[-1]
    tm = _tile(M, tm_pref, 16)
    tn = _tile(N, tn_pref, LANE)
    b_spec = (pl.BlockSpec((K, tn), lambda i, j: (0, j)) if layer is None
              else _layer_spec(layer, (K, tn), lambda i, j: (0, j)))
    return _call(
        _mm_kernel, (M // tm, N // tn),
        [pl.BlockSpec((tm, K), lambda i, j: (i, 0)), b_spec],
        pl.BlockSpec((tm, tn), lambda i, j: (i, j)),
        jax.ShapeDtypeStruct((M, N), BF16), name)(a, b)


def _norm_mm_kernel(a_ref, g_ref, b_ref, o_ref):
    a = a_ref[...].astype(F32)
    r = lax.rsqrt(jnp.mean(a * a, axis=-1, keepdims=True) + EPS)
    an = (a * r * g_ref[...]).astype(BF16)
    o_ref[...] = jnp.dot(an, b_ref[...], preferred_element_type=F32).astype(o_ref.dtype)


def _norm_mm(a, col_off, K, g, b, name):
    M = a.shape[0]
    N = b.shape[1]
    tm = _tile(M, 512, 16)
    cb = col_off // K
    assert col_off % K == 0
    return _call(
        _norm_mm_kernel, (M // tm,),
        [pl.BlockSpec((tm, K), lambda i: (i, cb)),
         pl.BlockSpec((1, K), lambda i: (0, 0)),
         pl.BlockSpec((K, N), lambda i: (0, 0))],
        pl.BlockSpec((tm, N), lambda i: (i, 0)),
        jax.ShapeDtypeStruct((M, N), BF16), name)(a, g.reshape(1, K), b)


def _prep_head(xs, g_ref, rope, rope_chunk, inv_d, shift):
    ones = jnp.ones((len(xs) * LANE, LANE), BF16)
    ssq = jnp.dot(jnp.concatenate([(x * x).astype(BF16) for x in xs], axis=1), ones, preferred_element_type=F32)
    r = lax.rsqrt(ssq * inv_d + EPS)
    out = []
    for c, x in enumerate(xs):
        y = x * r * g_ref[:, c * LANE:(c + 1) * LANE]
        if rope is not None and c == rope_chunk:
            cos, sin = rope
            src = lax.broadcasted_iota(jnp.int32, (LANE, LANE), 0)
            dst = lax.broadcasted_iota(jnp.int32, (LANE, LANE), 1)
            partner = jnp.where((dst % (2 * shift)) < shift, dst + shift, dst - shift)
            swap = (src == partner).astype(BF16)
            y = y * cos + jnp.dot(y.astype(BF16), swap, preferred_element_type=F32) * sin
        out.append(y.astype(BF16))
    return out


def _rope_tables(n_pos, rope_dim):
    nf = rope_dim // 4
    t = jnp.arange(n_pos, dtype=jnp.int32)
    row = (t // GRID_W).astype(F32)
    col = (t % GRID_W).astype(F32)
    inv = ROPE_BASE ** (-jnp.arange(nf, dtype=F32) / nf)
    a_row, a_col = row[:, None] * inv, col[:, None] * inv
    cos = jnp.concatenate([jnp.cos(a_row)] * 2 + [jnp.cos(a_col)] * 2, axis=-1)
    sin = jnp.concatenate([-jnp.sin(a_row), jnp.sin(a_row), -jnp.sin(a_col), jnp.sin(a_col)], axis=-1)
    cos = jnp.pad(cos, ((0, 0), (0, LANE - rope_dim)), constant_values=1.0)
    sin = jnp.pad(sin, ((0, 0), (0, LANE - rope_dim)))
    return cos, sin


def _attn_kernel(*refs, n_chunks, src_cfg, q_rope, keys_once, rope_chunk, inv_d, shift, tq, n_q):
    it = iter(refs)
    q_ref = next(it)
    srcs = [([next(it) for _ in range(n_chunks)], next(it)) for _ in src_cfg]
    gq_ref, gk_ref, cos_ref, sin_ref, o_ref, k_scr = (next(it) for _ in range(6))

    def prep(chunk_refs, rows, g_ref, rope):
        xs = [r[rows, :].astype(F32) for r in chunk_refs]
        tab = (cos_ref[rows, :], sin_ref[rows, :]) if rope else None
        return _prep_head(xs, g_ref, tab, rope_chunk, inv_d, shift)

    def prep_keys():
        off = 0
        for (k_chunks, _), (n_k, rope) in zip(srcs, src_cfg):
            bk = min(n_k, 512)
            for r0 in range(0, n_k, bk):
                for c, y in enumerate(prep(k_chunks, pl.ds(r0, bk), gk_ref, rope)):
                    k_scr[pl.ds(off + r0, bk), c * LANE:(c + 1) * LANE] = y
            off += n_k

    if keys_once:
        pl.when(pl.program_id(2) == 0)(prep_keys)
    else:
        prep_keys()

    def scores(c):
        q = jnp.concatenate(prep([q_ref.at[:, j * LANE:(j + 1) * LANE] for j in range(n_chunks)],
                                 pl.ds(c * tq, tq), gq_ref, q_rope), axis=1)
        return lax.dot_general(q, k_scr[...], (((1,), (1,)), ((), ())), preferred_element_type=F32)

    n_blocks = n_q // tq
    ahead = 1
    pending = [scores(c) for c in range(min(ahead, n_blocks))]
    for c in range(n_blocks):
        rows = pl.ds(c * tq, tq)
        s = pending.pop(0)
        if c + ahead < n_blocks:
            pending.append(scores(c + ahead))
        p = jnp.exp(s - jnp.max(s, axis=-1, keepdims=True))
        l = jnp.sum(p, axis=-1, keepdims=True)
        pb = p.astype(BF16)
        o, off = None, 0
        for (_, v_ref), (n_k, _) in zip(srcs, src_cfg):
            o_src = jnp.dot(pb[:, off:off + n_k], v_ref[...], preferred_element_type=F32)
            o = o_src if o is None else o + o_src
            off += n_k
        o_ref[rows, :] = (o * (1.0 / l)).astype(o_ref.dtype)


def _attend(q_arr, q_row_off, q_cb0, n_q, q_rope, srcs, g_q, g_k, cos, sin, rope_chunk, norm_dim, shift,
            B, n_kv_heads, group, dv, name):
    n_chunks = len(srcs[0][0])
    qw = n_chunks * LANE
    tq = _tile(n_q, 512, 16)
    assert q_row_off % n_q == 0
    qb = q_row_off // n_q
    in_specs = [pl.BlockSpec((n_q, qw), lambda b, hk, g: (qb + b, q_cb0 + hk * group + g))]
    args = [q_arr]
    src_cfg = []
    for k_chunks, (v_arr, v_cb0), row_off, n_k, rope in srcs:
        assert row_off % n_k == 0 and n_k % min(n_k, 512) == 0
        rb = row_off // n_k
        for arr, cb_fn in k_chunks:
            in_specs.append(pl.BlockSpec((n_k, LANE), (lambda b, hk, g, rb=rb, f=cb_fn: (rb + b, f(hk)))))
            args.append(arr)
        in_specs.append(pl.BlockSpec((n_k, dv), (lambda b, hk, g, rb=rb, c0=v_cb0: (rb + b, c0 + hk))))
        args.append(v_arr)
        src_cfg.append((n_k, rope))
    fixed = lambda b, hk, g: (0, 0)
    in_specs += [pl.BlockSpec((1, qw), fixed), pl.BlockSpec((1, qw), fixed),
                 pl.BlockSpec(cos.shape, fixed), pl.BlockSpec(sin.shape, fixed)]
    args += [g_q.reshape(1, qw), g_k.reshape(1, qw), cos, sin]
    assert cos.shape[0] >= max([n_q] + [n_k for n_k, rope in src_cfg if rope])
    n_all = sum(n_k for n_k, _ in src_cfg)
    kern = functools.partial(_attn_kernel, n_chunks=n_chunks, src_cfg=tuple(src_cfg), q_rope=q_rope,
                             keys_once=group > 1, rope_chunk=rope_chunk, inv_d=1.0 / norm_dim, shift=shift,
                             tq=tq, n_q=n_q)
    return _call(
        kern, (B, n_kv_heads, group), in_specs,
        pl.BlockSpec((n_q, dv), lambda b, hk, g: (b, hk * group + g)),
        jax.ShapeDtypeStruct((B * n_q, n_kv_heads * group * dv), BF16), name,
        scratch=[pltpu.VMEM((n_all, qw), BF16)])(*args)


def _conv_kernel(b_ref, c_ref, u_ref, w_ref, o_ref):
    v = c_ref[...].astype(F32) * u_ref[...].astype(F32)
    n = v.shape[0]
    row = lax.broadcasted_iota(jnp.int32, v.shape, 0)
    prev = jnp.where(row == 0, 0.0, pltpu.roll(v, 1, 0))
    nxt = jnp.where(row == n - 1, 0.0, pltpu.roll(v, n - 1, 0))
    y = prev * w_ref[0:1, :] + v * w_ref[1:2, :] + nxt * w_ref[2:3, :]
    o_ref[...] = (b_ref[...].astype(F32) * y).astype(o_ref.dtype)


def _conv(px, row_off, n_seq, seq_len, col_b, col_c, col_u, CW, conv_w):
    tc = _tile(CW, 512, LANE, col_b, col_c, col_u)
    rb = row_off // seq_len
    assert row_off % seq_len == 0

    def spec(col):
        return pl.BlockSpec((seq_len, tc), lambda s, j, col=col: (rb + s, col // tc + j))

    return _call(
        _conv_kernel, (n_seq, CW // tc),
        [spec(col_b), spec(col_c), spec(col_u), pl.BlockSpec((3, tc), lambda s, j: (0, j))],
        pl.BlockSpec((seq_len, tc), lambda s, j: (s, j)),
        jax.ShapeDtypeStruct((n_seq * seq_len, CW), BF16), "short_conv")(px, px, px, conv_w)


def _merge_kernel(*refs, n_a, n_src):
    o_refs, (wa_ref, wb_ref, wc_ref, ga_ref, gb_ref, gc_ref, o_ref) = refs[:3 * n_src], refs[3 * n_src:]

    def branch(k):
        o = o_refs[k * n_src][...]
        if n_src == 2:
            o = jnp.where(pl.program_id(0) < n_a, o, o_refs[k * n_src + 1][...])
        return o

    def term(o, w, g):
        return jax.nn.sigmoid(g[...].astype(F32)) * jnp.dot(o, w[...], preferred_element_type=F32)
    h = term(branch(0), wa_ref, ga_ref) + term(branch(1), wb_ref, gb_ref) + term(branch(2), wc_ref, gc_ref)
    o_ref[...] = h.astype(o_ref.dtype)


def _merge(oa, ob, oc, w_pa, w_pb, w_pc, l, px, gate_col, n_rows):
    D = w_pa.shape[2]
    tm = _tile(n_rows, 1024, 16, oa[0].shape[0])
    tn = _tile(D, 512, LANE, gate_col)
    gb0 = gate_col // tn
    nb = D // tn

    n_src = 1 if oa[1] is None else 2
    a_specs, a_args, na = [], [], 0
    for lat, ctx in (oa, ob, oc):
        if ctx is None:
            a_specs.append(pl.BlockSpec((tm, lat.shape[1]), lambda i, j: (i, 0)))
            a_args.append(lat)
        else:
            assert lat.shape[0] % tm == 0
            specs, args, na = _two_source_specs(lat, ctx, (tm, lat.shape[1]), tm, lambda j: 0)
            a_specs += specs
            a_args += args

    def w_spec(w):
        return _layer_spec(l, (w.shape[1], tn), lambda i, j: (0, j))

    def g_spec(k):
        return pl.BlockSpec((tm, tn), lambda i, j, k=k: (i, gb0 + k * nb + j))

    return _call(
        functools.partial(_merge_kernel, n_a=na, n_src=n_src), (n_rows // tm, nb),
        a_specs + [w_spec(w_pa), w_spec(w_pb), w_spec(w_pc), g_spec(0), g_spec(1), g_spec(2)],
        pl.BlockSpec((tm, tn), lambda i, j: (i, j)),
        jax.ShapeDtypeStruct((n_rows, D), BF16), "merge")(*a_args, w_pa, w_pb, w_pc, px, px, px)


def _mm_res_kernel(a_ref, b_ref, *refs, n_a):
    x_refs, (gt_ref, o_ref) = refs[:-2], refs[-2:]
    x = x_refs[0][...]
    if len(x_refs) == 2:
        x = jnp.where(pl.program_id(0) < n_a, x, x_refs[1][...])
    acc = jnp.dot(a_ref[...], b_ref[...], preferred_element_type=F32)
    o_ref[...] = x + gt_ref[...] * acc


def _mm_res(a, b, X, mods3, tok, l, which, n_rows, Xb=None):
    K = a.shape[1]
    D = b.shape[2]
    tm = _tile(n_rows, 1024, 16, tok.N)
    tn = _tile(D, 512, LANE)

    def gt_idx(i, j):
        row = tok.mod_row(i * tm)
        return ((l * MOD_ROWS + row) * N_MOD + which, 0, j)

    if Xb is None:
        x_specs, x_args, na = [pl.BlockSpec((tm, tn), lambda i, j: (i, j))], [X], 0
    else:
        assert X.shape[0] % tm == 0
        x_specs, x_args, na = _two_source_specs(X, Xb, (tm, tn), tm, lambda j: j)
    return _call(
        functools.partial(_mm_res_kernel, n_a=na), (n_rows // tm, D // tn),
        [pl.BlockSpec((tm, K), lambda i, j: (i, 0)), _layer_spec(l, (K, tn), lambda i, j: (0, j))]
        + x_specs + [pl.BlockSpec((None, 1, tn), gt_idx)],
        pl.BlockSpec((tm, tn), lambda i, j: (i, j)),
        jax.ShapeDtypeStruct((n_rows, D), F32), "proj_residual")(a, b, *x_args, mods3)


def _swiglu_up_kernel(h_ref, w_ref, o_ref):
    gu = jnp.dot(h_ref[...], w_ref[...], preferred_element_type=F32)
    f = o_ref.shape[1]
    o_ref[...] = (_silu(gu[:, :f]) * gu[:, f:]).astype(o_ref.dtype)


def _swiglu_up(h, w_gu, l):
    M, D = h.shape
    f = w_gu.shape[2] // 2
    tm = _tile(M, 1024, 16)
    return _call(
        _swiglu_up_kernel, (M // tm,),
        [pl.BlockSpec((tm, D), lambda i: (i, 0)), _layer_spec(l, (D, 2 * f), lambda i: (0, 0))],
        pl.BlockSpec((tm, f), lambda i: (i, 0)),
        jax.ShapeDtypeStruct((M, f), BF16), "shared_up")(h, w_gu)


SLAB_ROWS = 16
STAGE_PITCH = 24
EXPERT_TILE = 256
HI_MASK = -65536


def _pack_pairs(v):
    half = v.shape[1] // 2
    bits = lax.bitcast_convert_type(v.astype(BF16).astype(F32), jnp.int32)
    return lax.shift_right_logical(bits[:, :half], 16) | (bits[:, half:] & HI_MASK)


def _unpack_lo(w):
    return lax.bitcast_convert_type(lax.shift_left(w, 16), F32)


def _unpack_hi(w):
    return lax.bitcast_convert_type(w & HI_MASK, F32)


def _dispatch_kernel(pos_ref, zt_ref, h_ref, xs_hbm, slab_ref, zero_ref, sem, *, tm, n_experts):
    W = slab_ref.shape[1]
    trows = EXPERT_TILE * SLAB_ROWS

    @pl.when(pl.program_id(0) == 0)
    def _():
        zero_ref[...] = jnp.zeros_like(zero_ref)
        def zcopy(t):
            return pltpu.make_async_copy(zero_ref, xs_hbm.at[pl.ds(pl.multiple_of(t * trows, trows), trows), :], sem)
        for e in range(n_experts):
            zcopy(zt_ref[0, e]).start()
        for e in range(n_experts):
            zcopy(zt_ref[0, e]).wait()
        n_used = zt_ref[0, n_experts]
        n_tiles = xs_hbm.shape[0] // trows

        def zstart(t, carry):
            zcopy(t).start()
            return carry

        def zwait(t, carry):
            zcopy(t).wait()
            return carry

        lax.fori_loop(n_used, n_tiles, zstart, 0)
        lax.fori_loop(n_used, n_tiles, zwait, 0)

    words = _pack_pairs(h_ref[...])
    for s in range(SLAB_ROWS):
        slab_ref[pl.ds(s, tm, stride=STAGE_PITCH), :] = words[:, s * W:(s + 1) * W]

    def row_copy(r, k):
        p = pos_ref[0, r * TOP_K + k]
        return pltpu.make_async_copy(
            slab_ref.at[pl.ds(pl.multiple_of(r * STAGE_PITCH, 8), SLAB_ROWS), :],
            xs_hbm.at[pl.ds(pl.multiple_of(p * SLAB_ROWS, SLAB_ROWS), SLAB_ROWS), :], sem)

    def issue(r, carry):
        for k in range(TOP_K):
            row_copy(r, k).start(priority=k % 2)
        return carry

    lax.fori_loop(0, tm, issue, 0)
    for k in range(TOP_K):
        pltpu.make_async_copy(slab_ref.at[pl.ds(0, tm * SLAB_ROWS), :],
                              xs_hbm.at[pl.ds(0, tm * SLAB_ROWS), :], sem).wait()


def _dispatch(h2, pos, zero_tiles, n_slots):
    M, D = h2.shape
    W = D // (2 * SLAB_ROWS)
    assert W == LANE, "strided slab access needs LANE-wide slab rows"
    E = zero_tiles.shape[0] - 1
    tm = _tile(M, 256, 16)
    kern = functools.partial(_dispatch_kernel, tm=tm, n_experts=E)
    return _call(
        kern, (M // tm,),
        [pl.BlockSpec((None, 1, tm * TOP_K), lambda i: (i, 0, 0), memory_space=pltpu.SMEM),
         pl.BlockSpec((1, E + 1), lambda i: (0, 0), memory_space=pltpu.SMEM),
         pl.BlockSpec((tm, D), lambda i: (i, 0))],
        pl.BlockSpec(memory_space=pl.ANY),
        jax.ShapeDtypeStruct((n_slots * SLAB_ROWS, W), jnp.int32), "moe_dispatch",
        scratch=[pltpu.VMEM((tm * STAGE_PITCH, W), jnp.int32),
                 pltpu.VMEM((EXPERT_TILE * SLAB_ROWS, W), jnp.int32),
                 pltpu.SemaphoreType.DMA(())])(
            pos.reshape(M // tm, 1, tm * TOP_K), zero_tiles.reshape(1, E + 1), h2)


def _expert_kernel(te_ref, nu_ref, x_ref, wgu_ref, wdn_ref, o_ref):
    W = x_ref.shape[1]
    f = wdn_ref.shape[0]

    @pl.when(pl.program_id(0) < nu_ref[0])
    def _():
        words = [x_ref[pl.ds(s, EXPERT_TILE, stride=SLAB_ROWS), :] for s in range(SLAB_ROWS)]
        x = jnp.concatenate([_unpack_lo(w).astype(BF16) for w in words]
                            + [_unpack_hi(w).astype(BF16) for w in words], axis=1)
        gu = jnp.dot(x, wgu_ref[...], preferred_element_type=F32)
        act = (_silu(gu[:, :f]) * gu[:, f:]).astype(BF16)
        y = _pack_pairs(jnp.dot(act, wdn_ref[...], preferred_element_type=F32))
        for s in range(SLAB_ROWS):
            o_ref[pl.ds(s, EXPERT_TILE, stride=SLAB_ROWS), :] = y[:, s * W:(s + 1) * W]

    @pl.when(pl.program_id(0) >= nu_ref[0])
    def _():
        o_ref[...] = jnp.zeros_like(o_ref)


def _experts(xs, tile_expert, n_used, w_gu, w_down, l):
    _, E, D, F2 = w_gu.shape
    W = xs.shape[1]
    trows = EXPERT_TILE * SLAB_ROWS
    n_tiles = xs.shape[0] // trows
    grid_spec = pltpu.PrefetchScalarGridSpec(
        num_scalar_prefetch=2, grid=(n_tiles,),
        in_specs=[pl.BlockSpec((trows, W), lambda i, te, nu: (jnp.minimum(i, nu[0] - 1), 0)),
                  pl.BlockSpec((None, None, D, F2), lambda i, te, nu: (l, te[i], 0, 0)),
                  pl.BlockSpec((None, None, F2 // 2, D), lambda i, te, nu: (l, te[i], 0, 0))],
        out_specs=pl.BlockSpec((trows, W), lambda i, te, nu: (i, 0)))
    return pl.pallas_call(
        _expert_kernel, grid_spec=grid_spec, out_shape=jax.ShapeDtypeStruct(xs.shape, jnp.int32),
        name="moe_experts",
        compiler_params=pltpu.CompilerParams(dimension_semantics=("arbitrary",),
                                             vmem_limit_bytes=VMEM_LIMIT_BYTES))(
            tile_expert, n_used, xs, w_gu, w_down)


def _combine_kernel(pos_ref, posn_ref, wt_ref, sh_ref, x_ref, gt_ref, ys_hbm, o_ref, buf_a, buf_b, sem, *, tm):
    j = pl.program_id(0)
    W = buf_a.shape[1]
    half = SLAB_ROWS * W

    def issue(pref, first_tok, buf, s_idx):
        def body(r, carry):
            for k in range(TOP_K):
                p = pref[0, (first_tok + r) * TOP_K + k]
                pltpu.make_async_copy(
                    ys_hbm.at[pl.ds(pl.multiple_of(p * SLAB_ROWS, SLAB_ROWS), SLAB_ROWS), :],
                    buf.at[pl.ds(pl.multiple_of((k * tm + r) * STAGE_PITCH, 8), SLAB_ROWS), :],
                    sem.at[s_idx]).start()
            return carry
        lax.fori_loop(0, tm, body, 0)

    def drain(buf, s_idx):
        for k in range(TOP_K):
            pltpu.make_async_copy(ys_hbm.at[pl.ds(0, tm * SLAB_ROWS), :],
                                  buf.at[pl.ds(0, tm * SLAB_ROWS), :], sem.at[s_idx]).wait()

    def reduce(buf, r0):
        rows = pl.ds(r0, tm)
        wt = wt_ref[rows, :]
        wk = [wt[:, k:k + 1] for k in range(TOP_K)]
        for s in range(SLAB_ROWS):
            lo = hi = None
            for k in range(TOP_K):
                w = buf[pl.ds(k * tm * STAGE_PITCH + s, tm, stride=STAGE_PITCH), :]
                lo_k, hi_k = wk[k] * _unpack_lo(w), wk[k] * _unpack_hi(w)
                lo, hi = (lo_k, hi_k) if lo is None else (lo + lo_k, hi + hi_k)
            for c0, routed in ((s * W, lo), (half + s * W, hi)):
                cols = slice(c0, c0 + W)
                o_ref[rows, cols] = x_ref[rows, cols] + gt_ref[:, cols] * (sh_ref[rows, cols].astype(F32) + routed)

    @pl.when(j == 0)
    def _():
        issue(pos_ref, 0, buf_a, 0)

    issue(pos_ref, tm, buf_b, 1)
    drain(buf_a, 0)
    reduce(buf_a, 0)

    @pl.when(j + 1 < pl.num_programs(0))
    def _():
        issue(posn_ref, 0, buf_a, 0)

    drain(buf_b, 1)
    reduce(buf_b, tm)


def _combine(ys, pos, wt, sh, X, mods3, tok, l, which, n_rows):
    D = X.shape[1]
    W = ys.shape[1]
    E = wt.shape[1]
    tm = _tile(n_rows // 2, 128, 16, tok.N // 2)
    tb = 2 * tm
    n_steps = n_rows // tb
    pos3 = pos.reshape(n_steps, 1, tb * TOP_K)

    def gt_idx(j):
        row = tok.mod_row(j * tb)
        return ((l * MOD_ROWS + row) * N_MOD + which, 0, 0)

    kern = functools.partial(_combine_kernel, tm=tm)
    return _call(
        kern, (n_steps,),
        [pl.BlockSpec((None, 1, tb * TOP_K), lambda j: (j, 0, 0), memory_space=pltpu.SMEM),
         pl.BlockSpec((None, 1, tb * TOP_K), lambda j: (jnp.minimum(j + 1, n_steps - 1), 0, 0),
                      memory_space=pltpu.SMEM),
         pl.BlockSpec((tb, E), lambda j: (j, 0)),
         pl.BlockSpec((tb, D), lambda j: (j, 0)),
         pl.BlockSpec((tb, D), lambda j: (j, 0)),
         pl.BlockSpec((None, 1, D), gt_idx),
         pl.BlockSpec(memory_space=pl.ANY)],
        pl.BlockSpec((tb, D), lambda j: (j, 0)),
        jax.ShapeDtypeStruct((n_rows, D), F32), "moe_combine",
        scratch=[pltpu.VMEM((TOP_K * tm * STAGE_PITCH, W), jnp.int32),
                 pltpu.VMEM((TOP_K * tm * STAGE_PITCH, W), jnp.int32),
                 pltpu.SemaphoreType.DMA((2,))])(pos3, pos3, wt, sh, X, mods3, ys)


def _routing_plan(idx, rank, counts, n_tiles):
    E = counts.shape[0]
    padded = (counts + EXPERT_TILE - 1) // EXPERT_TILE * EXPERT_TILE
    ends = jnp.cumsum(padded)
    base = ends - padded
    experts = jnp.arange(E, dtype=jnp.int32)
    pos = jnp.sum(jnp.where(idx[..., None] == experts, base, 0), axis=-1) + rank
    tile_start = jnp.arange(n_tiles, dtype=jnp.int32) * EXPERT_TILE
    tile_expert = jnp.minimum(jnp.sum((ends[None, :] <= tile_start[:, None]).astype(jnp.int32), axis=1), E - 1)
    n_used = (ends[-1] // EXPERT_TILE).reshape(1)
    last_tile = jnp.maximum(ends // EXPERT_TILE - 1, 0)
    return pos.astype(jnp.int32), tile_expert.astype(jnp.int32), n_used.astype(jnp.int32), last_tile.astype(jnp.int32)


def kernel(x, c, ctx, c_ctx, w_mod, b_mod, g_mix, g_ffn, w_in, mla_q_norm_g, mla_kv_norm_g, w_uq, w_ukv,
           mla_q_head_g, mla_k_head_g, gqa_q_g, gqa_k_g, conv_w, w_pa, w_pb, w_pc, w_o, w_router, b_router,
           w_exp_gu, w_exp_down, w_sh_gu, w_sh_down):
    B, N, D = x.shape
    CTX = ctx.shape[1]
    L = w_mod.shape[0]
    QL = mla_q_norm_g.shape[1]
    KVL = mla_kv_norm_g.shape[1]
    HA = w_uq.shape[2] // MLA_QK
    HG = w_pc.shape[1] // GQA_HD
    CW = conv_w.shape[2]
    D_IN = w_in.shape[2]
    HKV = (D_IN - (KVL + MLA_ROPE + QL + HG * GQA_HD + 3 * CW + 3 * D)) // (2 * GQA_HD)
    GKV = HKV * GQA_HD
    E = w_router.shape[2]

    tok = _Tok(B, N, CTX)
    n_lat, n_ctx = tok.n_lat, tok.n_ctx

    KV_COLS = KVL + 2 * GKV + LANE
    col_gk, col_gv, col_kr = KVL, KVL + GKV, KVL + 2 * GKV
    main0 = KVL + MLA_ROPE + 2 * GKV
    col_cq, col_gq = 0, QL
    col_b, col_c, col_u = QL + HG * GQA_HD, QL + HG * GQA_HD + CW, QL + HG * GQA_HD + 2 * CW
    col_gate = col_u + CW

    cvec = jnp.zeros((MOD_ROWS, D), F32).at[:B].set(c).at[B].set(c_ctx)
    mods = _mods(cvec, w_mod, b_mod)
    mods3 = mods.reshape(L * MOD_ROWS * N_MOD, 1, D)

    cos_a, sin_a = _rope_tables(N, MLA_ROPE)
    cos_g, sin_g = _rope_tables(N, GQA_HD)

    w_pa_b, w_pb_b, w_pc_b, w_o_b = (w.astype(BF16) for w in (w_pa, w_pb, w_pc, w_o))
    w_exp_gu_b, w_exp_down_b = w_exp_gu.astype(BF16), w_exp_down.astype(BF16)
    w_sh_gu_b, w_sh_down_b = w_sh_gu.astype(BF16), w_sh_down.astype(BF16)
    w_main_b = w_in[:, :, main0:].astype(BF16)

    X, Xb = x.reshape(n_lat, D), ctx.reshape(n_ctx, D)

    for l in range(L):
        last = l == L - 1
        n_rows = n_lat if last else n_lat + n_ctx

        w_kv = jnp.concatenate(
            [w_in[l, :, :KVL].astype(BF16), w_in[l, :, KVL + MLA_ROPE:main0].astype(BF16),
             w_in[l, :, KVL:KVL + MLA_ROPE].astype(BF16), jnp.zeros((D, LANE - MLA_ROPE), BF16)], axis=1)
        w_uq_p = jnp.pad(w_uq[l].reshape(QL, HA, MLA_QK),
                         ((0, 0), (0, 0), (0, MLA_QK_PAD - MLA_QK))).reshape(QL, HA * MLA_QK_PAD).astype(BF16)
        w_ukv_r = w_ukv[l].reshape(KVL, HA, MLA_NOPE + MLA_V)
        w_ukv_p = jnp.concatenate([w_ukv_r[:, :, :MLA_NOPE].reshape(KVL, HA * MLA_NOPE),
                                   w_ukv_r[:, :, MLA_NOPE:].reshape(KVL, HA * MLA_V)], axis=1).astype(BF16)
        g_qa = jnp.pad(mla_q_head_g[l] * (1.0 / math.sqrt(MLA_QK)), (0, MLA_QK_PAD - MLA_QK))
        g_ka = jnp.pad(mla_k_head_g[l], (0, MLA_QK_PAD - MLA_QK))
        g_qg = gqa_q_g[l] * (1.0 / math.sqrt(GQA_HD))
        g_kg = gqa_k_g[l]

        h = _norm_mod(X, 0, n_rows, g_mix[l], mods3, tok, l, 0, 0, Xb=None if last else Xb)
        px = _mm(h, w_main_b, name="in_proj", layer=l)
        if last:
            c_src, c_off = (X, n_lat) if Xb is None else (Xb, 0)
            hc = _norm_mod(c_src, c_off, n_ctx, g_mix[l], mods3, tok, l, 0, n_lat)
            pkv_x = _mm(h, w_kv, tn_pref=KV_COLS, name="in_proj_kv")
            pkv_c = _mm(hc, w_kv, tn_pref=KV_COLS, name="in_proj_kv")
            kv_parts = [(pkv_x, n_lat, 0), (pkv_c, n_ctx, n_lat)]
        else:
            pkv = _mm(h, w_kv, tn_pref=KV_COLS, name="in_proj_kv")
            kv_parts = [(pkv, n_rows, 0)]

        kvr = [_norm_mm(p, 0, KVL, mla_kv_norm_g[l], w_ukv_p, "mla_kv_up") for p, _, _ in kv_parts]
        pk = [p for p, _, _ in kv_parts]
        q_raw = _norm_mm(px, col_cq, QL, mla_q_norm_g[l], w_uq_p, "mla_q_up")

        ctx_i = len(kv_parts) - 1
        ctx_off = 0 if last else n_lat
        vA0 = HA * MLA_NOPE // MLA_V

        def mla_src(i, row_off, n_k, rope):
            return ([(kvr[i], lambda hk: hk), (pk[i], lambda hk: col_kr // LANE)], (kvr[i], vA0), row_off, n_k, rope)

        def gqa_src(i, row_off, n_k, rope):
            return ([(pk[i], lambda hk: col_gk // LANE + hk)], (pk[i], col_gv // GQA_HD), row_off, n_k, rope)

        mla_args = (g_qa, g_ka, cos_a, sin_a, 1, MLA_QK, MLA_ROPE // 4, B, HA, 1, MLA_V)
        gqa_args = (g_qg, g_kg, cos_g, sin_g, 0, GQA_HD, GQA_HD // 4, B, HKV, HG // HKV, GQA_HD)
        oa = _attend(q_raw, 0, 0, N, True, [mla_src(ctx_i, ctx_off, CTX, False), mla_src(0, 0, N, True)],
                     *mla_args, "mla_attn")
        og = _attend(px, 0, col_gq // GQA_HD, N, True, [gqa_src(ctx_i, ctx_off, CTX, False), gqa_src(0, 0, N, True)],
                     *gqa_args, "gqa_attn")
        ob = _conv(px, 0, B, N, col_b, col_c, col_u, CW, conv_w[l])
        oa_c = og_c = ob_c = None
        if not last:
            oa_c = _attend(q_raw, n_lat, 0, CTX, False, [mla_src(0, n_lat, CTX, False)], *mla_args, "mla_attn_ctx")
            og_c = _attend(px, n_lat, col_gq // GQA_HD, CTX, False, [gqa_src(0, n_lat, CTX, False)],
                           *gqa_args, "gqa_attn_ctx")
            ob_c = _conv(px, n_lat, B, CTX, col_b, col_c, col_u, CW, conv_w[l])

        hm = _merge((oa, oa_c), (ob, ob_c), (og, og_c), w_pa_b, w_pb_b, w_pc_b, l, px, col_gate, n_rows)
        X = _mm_res(hm, w_o_b, X, mods3, tok, l, 2, n_rows, Xb=None if last else Xb)
        Xb = None

        h2, idx_m, wt_m, rank_m, cnt = _norm_router(X, n_rows, g_ffn[l], mods3, tok, l, w_router[l], b_router[l])
        n_tiles = n_rows * TOP_K // EXPERT_TILE + E
        pos, tile_expert, n_used, last_tile = _routing_plan(idx_m[:, :TOP_K], rank_m[:, :TOP_K], cnt[0], n_tiles)
        xs = _dispatch(h2, pos, jnp.concatenate([last_tile, n_used]), n_tiles * EXPERT_TILE)
        ys = _experts(xs, tile_expert, n_used, w_exp_gu_b, w_exp_down_b, l)
        sh = _mm(_swiglu_up(h2, w_sh_gu_b, l), w_sh_down_b, name="shared_down", layer=l)
        X = _combine(ys, pos, wt_m, sh, X, mods3, tok, l, 5, n_rows)

    return X.reshape(B, N, D)
```

```python
import functools
import math

import jax
import jax.numpy as jnp
from jax import lax
from jax.experimental import pallas as pl
from jax.experimental.pallas import tpu as pltpu

F32 = jnp.float32
BF16 = jnp.bfloat16

GRID_W = 64
ROPE_BASE = 10000.0
EPS = 1e-6
MLA_NOPE = 128
MLA_ROPE = 64
MLA_QK = MLA_NOPE + MLA_ROPE
MLA_V = 128
GQA_HD = 128
TOP_K = 4
ROUTE_SCALE = 2.5
N_MOD = 6

LANE = 128
MLA_QK_PAD = 2 * LANE
MOD_ROWS = 16
VMEM_LIMIT_BYTES = 56 * 1024 * 1024


def _tile(dim, pref, unit, *also):
    t = min(pref, dim) // unit * unit
    while t > unit:
        if dim % t == 0 and all(a % t == 0 for a in also):
            return t
        t -= unit
    assert dim % unit == 0 and all(a % unit == 0 for a in also), (dim, unit, also)
    return unit


def _call(kernel, grid, in_specs, out_specs, out_shape, name, scratch=()):
    return pl.pallas_call(
        kernel, grid=grid, in_specs=in_specs, out_specs=out_specs, out_shape=out_shape,
        scratch_shapes=list(scratch), name=name,
        compiler_params=pltpu.CompilerParams(
            dimension_semantics=("arbitrary",) * len(grid), vmem_limit_bytes=VMEM_LIMIT_BYTES))


def _silu(v):
    return v * jax.nn.sigmoid(v)


def _layer_spec(l, block, index):
    return pl.BlockSpec((None,) + tuple(block), lambda *g: (l,) + tuple(index(*g)))


def _mod_kernel(c_ref, w_ref, b_ref, o_ref):
    s = _silu(c_ref[...]).astype(BF16)
    o_ref[0] = jnp.dot(s, w_ref[0].astype(BF16), preferred_element_type=F32) + b_ref[0]


def _mods(cvec, w_mod, b_mod):
    L, D, N6 = w_mod.shape
    tn = _tile(N6, 512, LANE)
    return _call(
        _mod_kernel, (L, N6 // tn),
        [pl.BlockSpec((MOD_ROWS, D), lambda l, j: (0, 0)),
         pl.BlockSpec((1, D, tn), lambda l, j: (l, 0, j)),
         pl.BlockSpec((1, 1, tn), lambda l, j: (l, 0, j))],
        pl.BlockSpec((1, MOD_ROWS, tn), lambda l, j: (l, 0, j)),
        jax.ShapeDtypeStruct((L, MOD_ROWS, N6), F32), "adaln_mod")(cvec, w_mod, b_mod.reshape(L, 1, N6))


class _Tok:
    def __init__(self, B, N, CTX):
        self.B, self.N, self.CTX = B, N, CTX
        self.n_lat = B * N
        self.n_ctx = B * CTX

    def mod_row(self, tok0):
        return jnp.where(tok0 < self.n_lat, tok0 // self.N, self.B)


def _mod_spec(tok, l, which, tm, tok_off, D):
    def idx(i, *_):
        row = tok.mod_row(i * tm + tok_off)
        return ((l * MOD_ROWS + row) * N_MOD + which, 0, 0)
    return pl.BlockSpec((None, 1, D), idx)


def _two_source_specs(X, Xb, block, row_tile, col_of):
    na = X.shape[0] // row_tile
    return ([pl.BlockSpec(block, lambda i, *r: (jnp.minimum(i, na - 1), col_of(*r))),
             pl.BlockSpec(block, lambda i, *r: (jnp.maximum(i - na, 0), col_of(*r)))], [X, Xb], na)


def _norm_mod_kernel(*refs, n_a):
    x_refs, (g_ref, sh_ref, sc_ref, o_ref) = refs[:-4], refs[-4:]
    x = x_refs[0][...]
    if len(x_refs) == 2:
        x = jnp.where(pl.program_id(0) < n_a, x, x_refs[1][...])
    r = lax.rsqrt(jnp.mean(x * x, axis=-1, keepdims=True) + EPS)
    o_ref[...] = ((x * r * g_ref[...]) * (1.0 + sc_ref[...]) + sh_ref[...]).astype(o_ref.dtype)


def _norm_mod(X, row_off, n_rows, g, mods3, tok, l, which_shift, tok_off, Xb=None):
    D = X.shape[1]
    tm = _tile(n_rows, 256, 16, tok.N, row_off) if row_off else _tile(n_rows, 256, 16, tok.N)
    ro = row_off // tm
    if Xb is None:
        x_specs, x_args, na = [pl.BlockSpec((tm, D), lambda i: (i + ro, 0))], [X], 0
    else:
        assert row_off == 0 and X.shape[0] % tm == 0
        x_specs, x_args, na = _two_source_specs(X, Xb, (tm, D), tm, lambda: 0)
    return _call(
        functools.partial(_norm_mod_kernel, n_a=na), (n_rows // tm,),
        x_specs + [pl.BlockSpec((1, D), lambda i: (0, 0)),
                   _mod_spec(tok, l, which_shift, tm, tok_off, D),
                   _mod_spec(tok, l, which_shift + 1, tm, tok_off, D)],
        pl.BlockSpec((tm, D), lambda i: (i, 0)),
        jax.ShapeDtypeStruct((n_rows, D), BF16), "norm_mod")(*x_args, g.reshape(1, D), mods3, mods3)


def _norm_router_kernel(x_ref, g_ref, sh_ref, sc_ref, wr_ref, br_ref,
                        h_ref, idx_ref, wt_ref, rank_ref, cnt_ref, carry_ref):
    @pl.when(pl.program_id(0) == 0)
    def _():
        carry_ref[...] = jnp.zeros_like(carry_ref)

    x = x_ref[...]
    r = lax.rsqrt(jnp.mean(x * x, axis=-1, keepdims=True) + EPS)
    h = (x * r * g_ref[...]) * (1.0 + sc_ref[...]) + sh_ref[...]
    h_ref[...] = h.astype(h_ref.dtype)
    n_e = br_ref.shape[1]
    h_hi = h.astype(BF16)
    h_lo = (h - h_hi.astype(F32)).astype(BF16)
    w_both = wr_ref[...]
    first = jnp.dot(h_hi, w_both, preferred_element_type=F32)
    logits = first[:, :n_e] + first[:, n_e:] + jnp.dot(h_lo, w_both[:, :n_e], preferred_element_type=F32)
    s = jax.nn.sigmoid(logits)
    sel = s + br_ref[...]
    tm, n_e = sel.shape
    lane = lax.broadcasted_iota(jnp.int32, sel.shape, 1).astype(F32)
    picked = jnp.zeros(sel.shape, F32)
    firsts = []
    for _ in range(TOP_K):
        cur = jnp.where(picked > 0.0, -jnp.inf, sel)
        m = jnp.max(cur, axis=-1, keepdims=True)
        first = jnp.min(jnp.where(cur == m, lane, float(n_e)), axis=-1, keepdims=True)
        picked = jnp.where(lane == first, 1.0, picked)
        firsts.append(first)
    w = picked * s
    gate = w / jnp.sum(w, axis=-1, keepdims=True) * ROUTE_SCALE

    earlier = (lax.broadcasted_iota(jnp.int32, (tm, tm), 1) < lax.broadcasted_iota(jnp.int32, (tm, tm), 0))
    cum = jnp.dot(earlier.astype(BF16), picked.astype(BF16), preferred_element_type=F32) + carry_ref[...]
    idx_m = jnp.zeros(sel.shape, F32)
    wt_m = jnp.zeros(sel.shape, F32)
    rank_m = jnp.zeros(sel.shape, F32)
    for k, first in enumerate(firsts):
        mine = lane == first
        idx_m = jnp.where(lane == float(k), first, idx_m)
        wt_m = jnp.where(lane == float(k), jnp.sum(jnp.where(mine, gate, 0.0), axis=-1, keepdims=True), wt_m)
        rank_m = jnp.where(lane == float(k), jnp.sum(jnp.where(mine, cum, 0.0), axis=-1, keepdims=True), rank_m)
    idx_ref[...] = idx_m.astype(jnp.int32)
    wt_ref[...] = wt_m
    rank_ref[...] = rank_m.astype(jnp.int32)
    carry_ref[...] += jnp.sum(picked, axis=0, keepdims=True)
    cnt_ref[...] = carry_ref[...].astype(jnp.int32)


def _norm_router(X, n_rows, g, mods3, tok, l, w_router, b_router):
    D = X.shape[1]
    E = w_router.shape[1]
    w_hi = w_router.astype(BF16)
    w_both = jnp.concatenate([w_hi, (w_router - w_hi.astype(F32)).astype(BF16)], axis=1)
    tm = _tile(n_rows, 256, 16, tok.N)
    row = lambda i: (i, 0)
    fixed = lambda i: (0, 0)
    return _call(
        _norm_router_kernel, (n_rows // tm,),
        [pl.BlockSpec((tm, D), row), pl.BlockSpec((1, D), fixed),
         _mod_spec(tok, l, 3, tm, 0, D), _mod_spec(tok, l, 4, tm, 0, D),
         pl.BlockSpec((D, 2 * E), fixed), pl.BlockSpec((1, E), fixed)],
        [pl.BlockSpec((tm, D), row), pl.BlockSpec((tm, E), row), pl.BlockSpec((tm, E), row),
         pl.BlockSpec((tm, E), row), pl.BlockSpec((1, E), fixed)],
        [jax.ShapeDtypeStruct((n_rows, D), BF16), jax.ShapeDtypeStruct((n_rows, E), jnp.int32),
         jax.ShapeDtypeStruct((n_rows, E), F32), jax.ShapeDtypeStruct((n_rows, E), jnp.int32),
         jax.ShapeDtypeStruct((1, E), jnp.int32)],
        "norm_router", scratch=[pltpu.VMEM((1, E), F32)])(
            X, g.reshape(1, D), mods3, mods3, w_both, b_router.reshape(1, E))


def _mm_kernel(a_ref, b_ref, o_ref):
    o_ref[...] = jnp.dot(a_ref[...], b_ref[...], preferred_element_type=F32).astype(o_ref.dtype)


def _mm(a, b, tm_pref=1024, tn_pref=1024, name="mm", layer=None):
    M, K = a.shape
    N = b.shape[-1]
    tm = _tile(M, tm_pref, 16)
    tn = _tile(N, tn_pref, LANE)
    b_spec = (pl.BlockSpec((K, tn), lambda i, j: (0, j)) if layer is None
              else _layer_spec(layer, (K, tn), lambda i, j: (0, j)))
    return _call(
        _mm_kernel, (M // tm, N // tn),
        [pl.BlockSpec((tm, K), lambda i, j: (i, 0)), b_spec],
        pl.BlockSpec((tm, tn), lambda i, j: (i, j)),
        jax.ShapeDtypeStruct((M, N), BF16), name)(a, b)


def _norm_mm_kernel(a_ref, g_ref, b_ref, o_ref):
    a = a_ref[...].astype(F32)
    r = lax.rsqrt(jnp.mean(a * a, axis=-1, keepdims=True) + EPS)
    an = (a * r * g_ref[...]).astype(BF16)
    o_ref[...] = jnp.dot(an, b_ref[...], preferred_element_type=F32).astype(o_ref.dtype)


def _norm_mm(a, col_off, K, g, b, name):
    M = a.shape[0]
    N = b.shape[1]
    tm = _tile(M, 512, 16)
    cb = col_off // K
    assert col_off % K == 0
    return _call(
        _norm_mm_kernel, (M // tm,),
        [pl.BlockSpec((tm, K), lambda i: (i, cb)),
         pl.BlockSpec((1, K), lambda i: (0, 0)),
         pl.BlockSpec((K, N), lambda i: (0, 0))],
        pl.BlockSpec((tm, N), lambda i: (i, 0)),
        jax.ShapeDtypeStruct((M, N), BF16), name)(a, g.reshape(1, K), b)


def _prep_head(xs, g_ref, rope, rope_chunk, inv_d, shift):
    ones = jnp.ones((len(xs) * LANE, LANE), BF16)
    ssq = jnp.dot(jnp.concatenate([(x * x).astype(BF16) for x in xs], axis=1), ones, preferred_element_type=F32)
    r = lax.rsqrt(ssq * inv_d + EPS)
    out = []
    for c, x in enumerate(xs):
        y = x * r * g_ref[:, c * LANE:(c + 1) * LANE]
        if rope is not None and c == rope_chunk:
            cos, sin = rope
            src = lax.broadcasted_iota(jnp.int32, (LANE, LANE), 0)
            dst = lax.broadcasted_iota(jnp.int32, (LANE, LANE), 1)
            partner = jnp.where((dst % (2 * shift)) < shift, dst + shift, dst - shift)
            swap = (src == partner).astype(BF16)
            y = y * cos + jnp.dot(y.astype(BF16), swap, preferred_element_type=F32) * sin
        out.append(y.astype(BF16))
    return out


def _rope_tables(n_pos, rope_dim):
    nf = rope_dim // 4
    t = jnp.arange(n_pos, dtype=jnp.int32)
    row = (t // GRID_W).astype(F32)
    col = (t % GRID_W).astype(F32)
    inv = ROPE_BASE ** (-jnp.arange(nf, dtype=F32) / nf)
    a_row, a_col = row[:, None] * inv, col[:, None] * inv
    cos = jnp.concatenate([jnp.cos(a_row)] * 2 + [jnp.cos(a_col)] * 2, axis=-1)
    sin = jnp.concatenate([-jnp.sin(a_row), jnp.sin(a_row), -jnp.sin(a_col), jnp.sin(a_col)], axis=-1)
    cos = jnp.pad(cos, ((0, 0), (0, LANE - rope_dim)), constant_values=1.0)
    sin = jnp.pad(sin, ((0, 0), (0, LANE - rope_dim)))
    return cos, sin


def _attn_kernel(*refs, n_chunks, src_cfg, q_rope, keys_once, rope_chunk, inv_d, shift, tq, n_q):
    it = iter(refs)
    q_ref = next(it)
    srcs = [([next(it) for _ in range(n_chunks)], next(it)) for _ in src_cfg]
    gq_ref, gk_ref, cos_ref, sin_ref, o_ref, k_scr = (next(it) for _ in range(6))

    def prep(chunk_refs, rows, g_ref, rope):
        xs = [r[rows, :].astype(F32) for r in chunk_refs]
        tab = (cos_ref[rows, :], sin_ref[rows, :]) if rope else None
        return _prep_head(xs, g_ref, tab, rope_chunk, inv_d, shift)

    def prep_keys():
        off = 0
        for (k_chunks, _), (n_k, rope) in zip(srcs, src_cfg):
            bk = min(n_k, 512)
            for r0 in range(0, n_k, bk):
                for c, y in enumerate(prep(k_chunks, pl.ds(r0, bk), gk_ref, rope)):
                    k_scr[pl.ds(off + r0, bk), c * LANE:(c + 1) * LANE] = y
            off += n_k

    if keys_once:
        pl.when(pl.program_id(2) == 0)(prep_keys)
    else:
        prep_keys()

    def scores(c):
        q = jnp.concatenate(prep([q_ref.at[:, j * LANE:(j + 1) * LANE] for j in range(n_chunks)],
                                 pl.ds(c * tq, tq), gq_ref, q_rope), axis=1)
        return lax.dot_general(q, k_scr[...], (((1,), (1,)), ((), ())), preferred_element_type=F32)

    n_blocks = n_q // tq
    ahead = 1
    pending = [scores(c) for c in range(min(ahead, n_blocks))]
    for c in range(n_blocks):
        rows = pl.ds(c * tq, tq)
        s = pending.pop(0)
        if c + ahead < n_blocks:
            pending.append(scores(c + ahead))
        p = jnp.exp(s - jnp.max(s, axis=-1, keepdims=True))
        l = jnp.sum(p, axis=-1, keepdims=True)
        pb = p.astype(BF16)
        o, off = None, 0
        for (_, v_ref), (n_k, _) in zip(srcs, src_cfg):
            o_src = jnp.dot(pb[:, off:off + n_k], v_ref[...], preferred_element_type=F32)
            o = o_src if o is None else o + o_src
            off += n_k
        o_ref[rows, :] = (o * (1.0 / l)).astype(o_ref.dtype)


def _attend(q_arr, q_row_off, q_cb0, n_q, q_rope, srcs, g_q, g_k, cos, sin, rope_chunk, norm_dim, shift,
            B, n_kv_heads, group, dv, name):
    n_chunks = len(srcs[0][0])
    qw = n_chunks * LANE
    tq = _tile(n_q, 512, 16)
    assert q_row_off % n_q == 0
    qb = q_row_off // n_q
    in_specs = [pl.BlockSpec((n_q, qw), lambda b, hk, g: (qb + b, q_cb0 + hk * group + g))]
    args = [q_arr]
    src_cfg = []
    for k_chunks, (v_arr, v_cb0), row_off, n_k, rope in srcs:
        assert row_off % n_k == 0 and n_k % min(n_k, 512) == 0
        rb = row_off // n_k
        for arr, cb_fn in k_chunks:
            in_specs.append(pl.BlockSpec((n_k, LANE), (lambda b, hk, g, rb=rb, f=cb_fn: (rb + b, f(hk)))))
            args.append(arr)
        in_specs.append(pl.BlockSpec((n_k, dv), (lambda b, hk, g, rb=rb, c0=v_cb0: (rb + b, c0 + hk))))
        args.append(v_arr)
        src_cfg.append((n_k, rope))
    fixed = lambda b, hk, g: (0, 0)
    in_specs += [pl.BlockSpec((1, qw), fixed), pl.BlockSpec((1, qw), fixed),
                 pl.BlockSpec(cos.shape, fixed), pl.BlockSpec(sin.shape, fixed)]
    args += [g_q.reshape(1, qw), g_k.reshape(1, qw), cos, sin]
    assert cos.shape[0] >= max([n_q] + [n_k for n_k, rope in src_cfg if rope])
    n_all = sum(n_k for n_k, _ in src_cfg)
    kern = functools.partial(_attn_kernel, n_chunks=n_chunks, src_cfg=tuple(src_cfg), q_rope=q_rope,
                             keys_once=group > 1, rope_chunk=rope_chunk, inv_d=1.0 / norm_dim, shift=shift,
                             tq=tq, n_q=n_q)
    return _call(
        kern, (B, n_kv_heads, group), in_specs,
        pl.BlockSpec((n_q, dv), lambda b, hk, g: (b, hk * group + g)),
        jax.ShapeDtypeStruct((B * n_q, n_kv_heads * group * dv), BF16), name,
        scratch=[pltpu.VMEM((n_all, qw), BF16)])(*args)


def _conv_kernel(b_ref, c_ref, u_ref, w_ref, o_ref):
    v = c_ref[...].astype(F32) * u_ref[...].astype(F32)
    n = v.shape[0]
    row = lax.broadcasted_iota(jnp.int32, v.shape, 0)
    prev = jnp.where(row == 0, 0.0, pltpu.roll(v, 1, 0))
    nxt = jnp.where(row == n - 1, 0.0, pltpu.roll(v, n - 1, 0))
    y = prev * w_ref[0:1, :] + v * w_ref[1:2, :] + nxt * w_ref[2:3, :]
    o_ref[...] = (b_ref[...].astype(F32) * y).astype(o_ref.dtype)


def _conv(px, row_off, n_seq, seq_len, col_b, col_c, col_u, CW, conv_w):
    tc = _tile(CW, 512, LANE, col_b, col_c, col_u)
    rb = row_off // seq_len
    assert row_off % seq_len == 0

    def spec(col):
        return pl.BlockSpec((seq_len, tc), lambda s, j, col=col: (rb + s, col // tc + j))

    return _call(
        _conv_kernel, (n_seq, CW // tc),
        [spec(col_b), spec(col_c), spec(col_u), pl.BlockSpec((3, tc), lambda s, j: (0, j))],
        pl.BlockSpec((seq_len, tc), lambda s, j: (s, j)),
        jax.ShapeDtypeStruct((n_seq * seq_len, CW), BF16), "short_conv")(px, px, px, conv_w)


def _merge_kernel(*refs, n_a, n_src):
    o_refs, (wa_ref, wb_ref, wc_ref, ga_ref, gb_ref, gc_ref, o_ref) = refs[:3 * n_src], refs[3 * n_src:]

    def branch(k):
        o = o_refs[k * n_src][...]
        if n_src == 2:
            o = jnp.where(pl.program_id(0) < n_a, o, o_refs[k * n_src + 1][...])
        return o

    def term(o, w, g):
        return jax.nn.sigmoid(g[...].astype(F32)) * jnp.dot(o, w[...], preferred_element_type=F32)
    h = term(branch(0), wa_ref, ga_ref) + term(branch(1), wb_ref, gb_ref) + term(branch(2), wc_ref, gc_ref)
    o_ref[...] = h.astype(o_ref.dtype)


def _merge(oa, ob, oc, w_pa, w_pb, w_pc, l, px, gate_col, n_rows):
    D = w_pa.shape[2]
    tm = _tile(n_rows, 1024, 16, oa[0].shape[0])
    tn = _tile(D, 512, LANE, gate_col)
    gb0 = gate_col // tn
    nb = D // tn

    n_src = 1 if oa[1] is None else 2
    a_specs, a_args, na = [], [], 0
    for lat, ctx in (oa, ob, oc):
        if ctx is None:
            a_specs.append(pl.BlockSpec((tm, lat.shape[1]), lambda i, j: (i, 0)))
            a_args.append(lat)
        else:
            assert lat.shape[0] % tm == 0
            specs, args, na = _two_source_specs(lat, ctx, (tm, lat.shape[1]), tm, lambda j: 0)
            a_specs += specs
            a_args += args

    def w_spec(w):
        return _layer_spec(l, (w.shape[1], tn), lambda i, j: (0, j))

    def g_spec(k):
        return pl.BlockSpec((tm, tn), lambda i, j, k=k: (i, gb0 + k * nb + j))

    return _call(
        functools.partial(_merge_kernel, n_a=na, n_src=n_src), (n_rows // tm, nb),
        a_specs + [w_spec(w_pa), w_spec(w_pb), w_spec(w_pc), g_spec(0), g_spec(1), g_spec(2)],
        pl.BlockSpec((tm, tn), lambda i, j: (i, j)),
        jax.ShapeDtypeStruct((n_rows, D), BF16), "merge")(*a_args, w_pa, w_pb, w_pc, px, px, px)


def _mm_res_kernel(a_ref, b_ref, *refs, n_a):
    x_refs, (gt_ref, o_ref) = refs[:-2], refs[-2:]
    x = x_refs[0][...]
    if len(x_refs) == 2:
        x = jnp.where(pl.program_id(0) < n_a, x, x_refs[1][...])
    acc = jnp.dot(a_ref[...], b_ref[...], preferred_element_type=F32)
    o_ref[...] = x + gt_ref[...] * acc


def _mm_res(a, b, X, mods3, tok, l, which, n_rows, Xb=None):
    K = a.shape[1]
    D = b.shape[2]
    tm = _tile(n_rows, 1024, 16, tok.N)
    tn = _tile(D, 512, LANE)

    def gt_idx(i, j):
        row = tok.mod_row(i * tm)
        return ((l * MOD_ROWS + row) * N_MOD + which, 0, j)

    if Xb is None:
        x_specs, x_args, na = [pl.BlockSpec((tm, tn), lambda i, j: (i, j))], [X], 0
    else:
        assert X.shape[0] % tm == 0
        x_specs, x_args, na = _two_source_specs(X, Xb, (tm, tn), tm, lambda j: j)
    return _call(
        functools.partial(_mm_res_kernel, n_a=na), (n_rows // tm, D // tn),
        [pl.BlockSpec((tm, K), lambda i, j: (i, 0)), _layer_spec(l, (K, tn), lambda i, j: (0, j))]
        + x_specs + [pl.BlockSpec((None, 1, tn), gt_idx)],
        pl.BlockSpec((tm, tn), lambda i, j: (i, j)),
        jax.ShapeDtypeStruct((n_rows, D), F32), "proj_residual")(a, b, *x_args, mods3)


def _swiglu_up_kernel(h_ref, w_ref, o_ref):
    gu = jnp.dot(h_ref[...], w_ref[...], preferred_element_type=F32)
    f = o_ref.shape[1]
    o_ref[...] = (_silu(gu[:, :f]) * gu[:, f:]).astype(o_ref.dtype)


def _swiglu_up(h, w_gu, l):
    M, D = h.shape
    f = w_gu.shape[2] // 2
    tm = _tile(M, 1024, 16)
    return _call(
        _swiglu_up_kernel, (M // tm,),
        [pl.BlockSpec((tm, D), lambda i: (i, 0)), _layer_spec(l, (D, 2 * f), lambda i: (0, 0))],
        pl.BlockSpec((tm, f), lambda i: (i, 0)),
        jax.ShapeDtypeStruct((M, f), BF16), "shared_up")(h, w_gu)


SLAB_ROWS = 16
STAGE_PITCH = 24
EXPERT_TILE = 256
HI_MASK = -65536


def _pack_pairs(v):
    half = v.shape[1] // 2
    bits = lax.bitcast_convert_type(v.astype(BF16).astype(F32), jnp.int32)
    return lax.shift_right_logical(bits[:, :half], 16) | (bits[:, half:] & HI_MASK)


def _unpack_lo(w):
    return lax.bitcast_convert_type(lax.shift_left(w, 16), F32)


def _unpack_hi(w):
    return lax.bitcast_convert_type(w & HI_MASK, F32)


def _dispatch_kernel(pos_ref, zt_ref, h_ref, xs_hbm, slab_ref, zero_ref, sem, *, tm, n_experts):
    W = slab_ref.shape[1]
    trows = EXPERT_TILE * SLAB_ROWS

    @pl.when(pl.program_id(0) == 0)
    def _():
        zero_ref[...] = jnp.zeros_like(zero_ref)
        def zcopy(t):
            return pltpu.make_async_copy(zero_ref, xs_hbm.at[pl.ds(pl.multiple_of(t * trows, trows), trows), :], sem)
        for e in range(n_experts):
            zcopy(zt_ref[0, e]).start()
        for e in range(n_experts):
            zcopy(zt_ref[0, e]).wait()
        n_used = zt_ref[0, n_experts]
        n_tiles = xs_hbm.shape[0] // trows

        def zstart(t, carry):
            zcopy(t).start()
            return carry

        def zwait(t, carry):
            zcopy(t).wait()
            return carry

        lax.fori_loop(n_used, n_tiles, zstart, 0)
        lax.fori_loop(n_used, n_tiles, zwait, 0)

    words = _pack_pairs(h_ref[...])
    for s in range(SLAB_ROWS):
        slab_ref[pl.ds(s, tm, stride=STAGE_PITCH), :] = words[:, s * W:(s + 1) * W]

    def row_copy(r, k):
        p = pos_ref[0, r * TOP_K + k]
        return pltpu.make_async_copy(
            slab_ref.at[pl.ds(pl.multiple_of(r * STAGE_PITCH, 8), SLAB_ROWS), :],
            xs_hbm.at[pl.ds(pl.multiple_of(p * SLAB_ROWS, SLAB_ROWS), SLAB_ROWS), :], sem)

    def issue(r, carry):
        for k in range(TOP_K):
            row_copy(r, k).start(priority=k % 2)
        return carry

    lax.fori_loop(0, tm, issue, 0)
    for k in range(TOP_K):
        pltpu.make_async_copy(slab_ref.at[pl.ds(0, tm * SLAB_ROWS), :],
                              xs_hbm.at[pl.ds(0, tm * SLAB_ROWS), :], sem).wait()


def _dispatch(h2, pos, zero_tiles, n_slots):
    M, D = h2.shape
    W = D // (2 * SLAB_ROWS)
    assert W == LANE, "strided slab access needs LANE-wide slab rows"
    E = zero_tiles.shape[0] - 1
    tm = _tile(M, 256, 16)
    kern = functools.partial(_dispatch_kernel, tm=tm, n_experts=E)
    return _call(
        kern, (M // tm,),
        [pl.BlockSpec((None, 1, tm * TOP_K), lambda i: (i, 0, 0), memory_space=pltpu.SMEM),
         pl.BlockSpec((1, E + 1), lambda i: (0, 0), memory_space=pltpu.SMEM),
         pl.BlockSpec((tm, D), lambda i: (i, 0))],
        pl.BlockSpec(memory_space=pl.ANY),
        jax.ShapeDtypeStruct((n_slots * SLAB_ROWS, W), jnp.int32), "moe_dispatch",
        scratch=[pltpu.VMEM((tm * STAGE_PITCH, W), jnp.int32),
                 pltpu.VMEM((EXPERT_TILE * SLAB_ROWS, W), jnp.int32),
                 pltpu.SemaphoreType.DMA(())])(
            pos.reshape(M // tm, 1, tm * TOP_K), zero_tiles.reshape(1, E + 1), h2)


def _expert_kernel(te_ref, nu_ref, x_ref, wgu_ref, wdn_ref, o_ref):
    W = x_ref.shape[1]
    f = wdn_ref.shape[0]

    @pl.when(pl.program_id(0) < nu_ref[0])
    def _():
        words = [x_ref[pl.ds(s, EXPERT_TILE, stride=SLAB_ROWS), :] for s in range(SLAB_ROWS)]
        x = jnp.concatenate([_unpack_lo(w).astype(BF16) for w in words]
                            + [_unpack_hi(w).astype(BF16) for w in words], axis=1)
        gu = jnp.dot(x, wgu_ref[...], preferred_element_type=F32)
        act = (_silu(gu[:, :f]) * gu[:, f:]).astype(BF16)
        y = _pack_pairs(jnp.dot(act, wdn_ref[...], preferred_element_type=F32))
        for s in range(SLAB_ROWS):
            o_ref[pl.ds(s, EXPERT_TILE, stride=SLAB_ROWS), :] = y[:, s * W:(s + 1) * W]

    @pl.when(pl.program_id(0) >= nu_ref[0])
    def _():
        o_ref[...] = jnp.zeros_like(o_ref)


def _experts(xs, tile_expert, n_used, w_gu, w_down, l):
    _, E, D, F2 = w_gu.shape
    W = xs.shape[1]
    trows = EXPERT_TILE * SLAB_ROWS
    n_tiles = xs.shape[0] // trows
    grid_spec = pltpu.PrefetchScalarGridSpec(
        num_scalar_prefetch=2, grid=(n_tiles,),
        in_specs=[pl.BlockSpec((trows, W), lambda i, te, nu: (jnp.minimum(i, nu[0] - 1), 0)),
                  pl.BlockSpec((None, None, D, F2), lambda i, te, nu: (l, te[i], 0, 0)),
                  pl.BlockSpec((None, None, F2 // 2, D), lambda i, te, nu: (l, te[i], 0, 0))],
        out_specs=pl.BlockSpec((trows, W), lambda i, te, nu: (i, 0)))
    return pl.pallas_call(
        _expert_kernel, grid_spec=grid_spec, out_shape=jax.ShapeDtypeStruct(xs.shape, jnp.int32),
        name="moe_experts",
        compiler_params=pltpu.CompilerParams(dimension_semantics=("arbitrary",),
                                             vmem_limit_bytes=VMEM_LIMIT_BYTES))(
            tile_expert, n_used, xs, w_gu, w_down)


def _combine_kernel(pos_ref, posn_ref, wt_ref, sh_ref, x_ref, gt_ref, ys_hbm, o_ref, buf_a, buf_b, sem, *, tm):
    j = pl.program_id(0)
    W = buf_a.shape[1]
    half = SLAB_ROWS * W

    def issue(pref, first_tok, buf, s_idx):
        def body(r, carry):
            for k in range(TOP_K):
                p = pref[0, (first_tok + r) * TOP_K + k]
                pltpu.make_async_copy(
                    ys_hbm.at[pl.ds(pl.multiple_of(p * SLAB_ROWS, SLAB_ROWS), SLAB_ROWS), :],
                    buf.at[pl.ds(pl.multiple_of((k * tm + r) * STAGE_PITCH, 8), SLAB_ROWS), :],
                    sem.at[s_idx]).start()
            return carry
        lax.fori_loop(0, tm, body, 0)

    def drain(buf, s_idx):
        for k in range(TOP_K):
            pltpu.make_async_copy(ys_hbm.at[pl.ds(0, tm * SLAB_ROWS), :],
                                  buf.at[pl.ds(0, tm * SLAB_ROWS), :], sem.at[s_idx]).wait()

    def reduce(buf, r0):
        rows = pl.ds(r0, tm)
        wt = wt_ref[rows, :]
        wk = [wt[:, k:k + 1] for k in range(TOP_K)]
        for s in range(SLAB_ROWS):
            lo = hi = None
            for k in range(TOP_K):
                w = buf[pl.ds(k * tm * STAGE_PITCH + s, tm, stride=STAGE_PITCH), :]
                lo_k, hi_k = wk[k] * _unpack_lo(w), wk[k] * _unpack_hi(w)
                lo, hi = (lo_k, hi_k) if lo is None else (lo + lo_k, hi + hi_k)
            for c0, routed in ((s * W, lo), (half + s * W, hi)):
                cols = slice(c0, c0 + W)
                o_ref[rows, cols] = x_ref[rows, cols] + gt_ref[:, cols] * (sh_ref[rows, cols].astype(F32) + routed)

    @pl.when(j == 0)
    def _():
        issue(pos_ref, 0, buf_a, 0)

    issue(pos_ref, tm, buf_b, 1)
    drain(buf_a, 0)
    reduce(buf_a, 0)

    @pl.when(j + 1 < pl.num_programs(0))
    def _():
        issue(posn_ref, 0, buf_a, 0)

    drain(buf_b, 1)
    reduce(buf_b, tm)


def _combine(ys, pos, wt, sh, X, mods3, tok, l, which, n_rows):
    D = X.shape[1]
    W = ys.shape[1]
    E = wt.shape[1]
    tm = _tile(n_rows // 2, 128, 16, tok.N // 2)
    tb = 2 * tm
    n_steps = n_rows // tb
    pos3 = pos.reshape(n_steps, 1, tb * TOP_K)

    def gt_idx(j):
        row = tok.mod_row(j * tb)
        return ((l * MOD_ROWS + row) * N_MOD + which, 0, 0)

    kern = functools.partial(_combine_kernel, tm=tm)
    return _call(
        kern, (n_steps,),
        [pl.BlockSpec((None, 1, tb * TOP_K), lambda j: (j, 0, 0), memory_space=pltpu.SMEM),
         pl.BlockSpec((None, 1, tb * TOP_K), lambda j: (jnp.minimum(j + 1, n_steps - 1), 0, 0),
                      memory_space=pltpu.SMEM),
         pl.BlockSpec((tb, E), lambda j: (j, 0)),
         pl.BlockSpec((tb, D), lambda j: (j, 0)),
         pl.BlockSpec((tb, D), lambda j: (j, 0)),
         pl.BlockSpec((None, 1, D), gt_idx),
         pl.BlockSpec(memory_space=pl.ANY)],
        pl.BlockSpec((tb, D), lambda j: (j, 0)),
        jax.ShapeDtypeStruct((n_rows, D), F32), "moe_combine",
        scratch=[pltpu.VMEM((TOP_K * tm * STAGE_PITCH, W), jnp.int32),
                 pltpu.VMEM((TOP_K * tm * STAGE_PITCH, W), jnp.int32),
                 pltpu.SemaphoreType.DMA((2,))])(pos3, pos3, wt, sh, X, mods3, ys)


def _routing_plan(idx, rank, counts, n_tiles):
    E = counts.shape[0]
    padded = (counts + EXPERT_TILE - 1) // EXPERT_TILE * EXPERT_TILE
    ends = jnp.cumsum(padded)
    base = ends - padded
    experts = jnp.arange(E, dtype=jnp.int32)
    pos = jnp.sum(jnp.where(idx[..., None] == experts, base, 0), axis=-1) + rank
    tile_start = jnp.arange(n_tiles, dtype=jnp.int32) * EXPERT_TILE
    tile_expert = jnp.minimum(jnp.sum((ends[None, :] <= tile_start[:, None]).astype(jnp.int32), axis=1), E - 1)
    n_used = (ends[-1] // EXPERT_TILE).reshape(1)
    last_tile = jnp.maximum(ends // EXPERT_TILE - 1, 0)
    return pos.astype(jnp.int32), tile_expert.astype(jnp.int32), n_used.astype(jnp.int32), last_tile.astype(jnp.int32)


def kernel(x, c, ctx, c_ctx, w_mod, b_mod, g_mix, g_ffn, w_in, mla_q_norm_g, mla_kv_norm_g, w_uq, w_ukv,
           mla_q_head_g, mla_k_head_g, gqa_q_g, gqa_k_g, conv_w, w_pa, w_pb, w_pc, w_o, w_router, b_router,
           w_exp_gu, w_exp_down, w_sh_gu, w_sh_down):
    B, N, D = x.shape
    CTX = ctx.shape[1]
    L = w_mod.shape[0]
    QL = mla_q_norm_g.shape[1]
    KVL = mla_kv_norm_g.shape[1]
    HA = w_uq.shape[2] // MLA_QK
    HG = w_pc.shape[1] // GQA_HD
    CW = conv_w.shape[2]
    D_IN = w_in.shape[2]
    HKV = (D_IN - (KVL + MLA_ROPE + QL + HG * GQA_HD + 3 * CW + 3 * D)) // (2 * GQA_HD)
    GKV = HKV * GQA_HD
    E = w_router.shape[2]

    tok = _Tok(B, N, CTX)
    n_lat, n_ctx = tok.n_lat, tok.n_ctx

    KV_COLS = KVL + 2 * GKV + LANE
    col_gk, col_gv, col_kr = KVL, KVL + GKV, KVL + 2 * GKV
    main0 = KVL + MLA_ROPE + 2 * GKV
    col_cq, col_gq = 0, QL
    col_b, col_c, col_u = QL + HG * GQA_HD, QL + HG * GQA_HD + CW, QL + HG * GQA_HD + 2 * CW
    col_gate = col_u + CW

    cvec = jnp.zeros((MOD_ROWS, D), F32).at[:B].set(c).at[B].set(c_ctx)
    mods = _mods(cvec, w_mod, b_mod)
    mods3 = mods.reshape(L * MOD_ROWS * N_MOD, 1, D)

    cos_a, sin_a = _rope_tables(N, MLA_ROPE)
    cos_g, sin_g = _rope_tables(N, GQA_HD)

    w_pa_b, w_pb_b, w_pc_b, w_o_b = (w.astype(BF16) for w in (w_pa, w_pb, w_pc, w_o))
    w_exp_gu_b, w_exp_down_b = w_exp_gu.astype(BF16), w_exp_down.astype(BF16)
    w_sh_gu_b, w_sh_down_b = w_sh_gu.astype(BF16), w_sh_down.astype(BF16)
    w_main_b = w_in[:, :, main0:].astype(BF16)

    X, Xb = x.reshape(n_lat, D), ctx.reshape(n_ctx, D)

    for l in range(L):
        last = l == L - 1
        n_rows = n_lat if last else n_lat + n_ctx

        w_kv = jnp.concatenate(
            [w_in[l, :, :KVL].astype(BF16), w_in[l, :, KVL + MLA_ROPE:main0].astype(BF16),
             w_in[l, :, KVL:KVL + MLA_ROPE].astype(BF16), jnp.zeros((D, LANE - MLA_ROPE), BF16)], axis=1)
        w_uq_p = jnp.pad(w_uq[l].reshape(QL, HA, MLA_QK),
                         ((0, 0), (0, 0), (0, MLA_QK_PAD - MLA_QK))).reshape(QL, HA * MLA_QK_PAD).astype(BF16)
        w_ukv_r = w_ukv[l].reshape(KVL, HA, MLA_NOPE + MLA_V)
        w_ukv_p = jnp.concatenate([w_ukv_r[:, :, :MLA_NOPE].reshape(KVL, HA * MLA_NOPE),
                                   w_ukv_r[:, :, MLA_NOPE:].reshape(KVL, HA * MLA_V)], axis=1).astype(BF16)
        g_qa = jnp.pad(mla_q_head_g[l] * (1.0 / math.sqrt(MLA_QK)), (0, MLA_QK_PAD - MLA_QK))
        g_ka = jnp.pad(mla_k_head_g[l], (0, MLA_QK_PAD - MLA_QK))
        g_qg = gqa_q_g[l] * (1.0 / math.sqrt(GQA_HD))
        g_kg = gqa_k_g[l]

        h = _norm_mod(X, 0, n_rows, g_mix[l], mods3, tok, l, 0, 0, Xb=None if last else Xb)
        px = _mm(h, w_main_b, name="in_proj", layer=l)
        if last:
            c_src, c_off = (X, n_lat) if Xb is None else (Xb, 0)
            hc = _norm_mod(c_src, c_off, n_ctx, g_mix[l], mods3, tok, l, 0, n_lat)
            pkv_x = _mm(h, w_kv, tn_pref=KV_COLS, name="in_proj_kv")
            pkv_c = _mm(hc, w_kv, tn_pref=KV_COLS, name="in_proj_kv")
            kv_parts = [(pkv_x, n_lat, 0), (pkv_c, n_ctx, n_lat)]
        else:
            pkv = _mm(h, w_kv, tn_pref=KV_COLS, name="in_proj_kv")
            kv_parts = [(pkv, n_rows, 0)]

        kvr = [_norm_mm(p, 0, KVL, mla_kv_norm_g[l], w_ukv_p, "mla_kv_up") for p, _, _ in kv_parts]
        pk = [p for p, _, _ in kv_parts]
        q_raw = _norm_mm(px, col_cq, QL, mla_q_norm_g[l], w_uq_p, "mla_q_up")

        ctx_i = len(kv_parts) - 1
        ctx_off = 0 if last else n_lat
        vA0 = HA * MLA_NOPE // MLA_V

        def mla_src(i, row_off, n_k, rope):
            return ([(kvr[i], lambda hk: hk), (pk[i], lambda hk: col_kr // LANE)], (kvr[i], vA0), row_off, n_k, rope)

        def gqa_src(i, row_off, n_k, rope):
            return ([(pk[i], lambda hk: col_gk // LANE + hk)], (pk[i], col_gv // GQA_HD), row_off, n_k, rope)

        mla_args = (g_qa, g_ka, cos_a, sin_a, 1, MLA_QK, MLA_ROPE // 4, B, HA, 1, MLA_V)
        gqa_args = (g_qg, g_kg, cos_g, sin_g, 0, GQA_HD, GQA_HD // 4, B, HKV, HG // HKV, GQA_HD)
        oa = _attend(q_raw, 0, 0, N, True, [mla_src(ctx_i, ctx_off, CTX, False), mla_src(0, 0, N, True)],
                     *mla_args, "mla_attn")
        og = _attend(px, 0, col_gq // GQA_HD, N, True, [gqa_src(ctx_i, ctx_off, CTX, False), gqa_src(0, 0, N, True)],
                     *gqa_args, "gqa_attn")
        ob = _conv(px, 0, B, N, col_b, col_c, col_u, CW, conv_w[l])
        oa_c = og_c = ob_c = None
        if not last:
            oa_c = _attend(q_raw, n_lat, 0, CTX, False, [mla_src(0, n_lat, CTX, False)], *mla_args, "mla_attn_ctx")
            og_c = _attend(px, n_lat, col_gq // GQA_HD, CTX, False, [gqa_src(0, n_lat, CTX, False)],
                           *gqa_args, "gqa_attn_ctx")
            ob_c = _conv(px, n_lat, B, CTX, col_b, col_c, col_u, CW, conv_w[l])

        hm = _merge((oa, oa_c), (ob, ob_c), (og, og_c), w_pa_b, w_pb_b, w_pc_b, l, px, col_gate, n_rows)
        X = _mm_res(hm, w_o_b, X, mods3, tok, l, 2, n_rows, Xb=None if last else Xb)
        Xb = None

        h2, idx_m, wt_m, rank_m, cnt = _norm_router(X, n_rows, g_ffn[l], mods3, tok, l, w_router[l], b_router[l])
        n_tiles = n_rows * TOP_K // EXPERT_TILE + E
        pos, tile_expert, n_used, last_tile = _routing_plan(idx_m[:, :TOP_K], rank_m[:, :TOP_K], cnt[0], n_tiles)
        xs = _dispatch(h2, pos, jnp.concatenate([last_tile, n_used]), n_tiles * EXPERT_TILE)
        ys = _experts(xs, tile_expert, n_used, w_exp_gu_b, w_exp_down_b, l)
        sh = _mm(_swiglu_up(h2, w_sh_gu_b, l), w_sh_down_b, name="shared_down", layer=l)
        X = _combine(ys, pos, wt_m, sh, X, mods3, tok, l, 5, n_rows)

    return X.reshape(B, N, D)
```

```python
import functools
import math

import jax
import jax.numpy as jnp
from jax import lax
from jax.experimental import pallas as pl
from jax.experimental.pallas import tpu as pltpu

F32 = jnp.float32
BF16 = jnp.bfloat16

GRID_W = 64
ROPE_BASE = 10000.0
EPS = 1e-6
MLA_NOPE = 128
MLA_ROPE = 64
MLA_QK = MLA_NOPE + MLA_ROPE
MLA_V = 128
GQA_HD = 128
TOP_K = 4
ROUTE_SCALE = 2.5
N_MOD = 6

LANE = 128
MLA_QK_PAD = 2 * LANE
MOD_ROWS = 16
VMEM_LIMIT_BYTES = 56 * 1024 * 1024


def _tile(dim, pref, unit, *also):
    t = min(pref, dim) // unit * unit
    while t > unit:
        if dim % t == 0 and all(a % t == 0 for a in also):
            return t
        t -= unit
    assert dim % unit == 0 and all(a % unit == 0 for a in also), (dim, unit, also)
    return unit


def _call(kernel, grid, in_specs, out_specs, out_shape, name, scratch=()):
    return pl.pallas_call(
        kernel, grid=grid, in_specs=in_specs, out_specs=out_specs, out_shape=out_shape,
        scratch_shapes=list(scratch), name=name,
        compiler_params=pltpu.CompilerParams(
            dimension_semantics=("arbitrary",) * len(grid), vmem_limit_bytes=VMEM_LIMIT_BYTES))


def _silu(v):
    return v * jax.nn.sigmoid(v)


def _layer_spec(l, block, index):
    return pl.BlockSpec((None,) + tuple(block), lambda *g: (l,) + tuple(index(*g)))


def _mod_kernel(c_ref, w_ref, b_ref, o_ref):
    s = _silu(c_ref[...]).astype(BF16)
    o_ref[0] = jnp.dot(s, w_ref[0].astype(BF16), preferred_element_type=F32) + b_ref[0]


def _mods(cvec, w_mod, b_mod):
    L, D, N6 = w_mod.shape
    tn = _tile(N6, 512, LANE)
    return _call(
        _mod_kernel, (L, N6 // tn),
        [pl.BlockSpec((MOD_ROWS, D), lambda l, j: (0, 0)),
         pl.BlockSpec((1, D, tn), lambda l, j: (l, 0, j)),
         pl.BlockSpec((1, 1, tn), lambda l, j: (l, 0, j))],
        pl.BlockSpec((1, MOD_ROWS, tn), lambda l, j: (l, 0, j)),
        jax.ShapeDtypeStruct((L, MOD_ROWS, N6), F32), "adaln_mod")(cvec, w_mod, b_mod.reshape(L, 1, N6))


class _Tok:
    def __init__(self, B, N, CTX):
        self.B, self.N, self.CTX = B, N, CTX
        self.n_lat = B * N
        self.n_ctx = B * CTX

    def mod_row(self, tok0):
        return jnp.where(tok0 < self.n_lat, tok0 // self.N, self.B)


def _mod_spec(tok, l, which, tm, tok_off, D):
    def idx(i, *_):
        row = tok.mod_row(i * tm + tok_off)
        return ((l * MOD_ROWS + row) * N_MOD + which, 0, 0)
    return pl.BlockSpec((None, 1, D), idx)


def _two_source_specs(X, Xb, block, row_tile, col_of):
    na = X.shape[0] // row_tile
    return ([pl.BlockSpec(block, lambda i, *r: (jnp.minimum(i, na - 1), col_of(*r))),
             pl.BlockSpec(block, lambda i, *r: (jnp.maximum(i - na, 0), col_of(*r)))], [X, Xb], na)


def _norm_mod_kernel(*refs, n_a):
    x_refs, (g_ref, sh_ref, sc_ref, o_ref) = refs[:-4], refs[-4:]
    x = x_refs[0][...]
    if len(x_refs) == 2:
        x = jnp.where(pl.program_id(0) < n_a, x, x_refs[1][...])
    r = lax.rsqrt(jnp.mean(x * x, axis=-1, keepdims=True) + EPS)
    o_ref[...] = ((x * r * g_ref[...]) * (1.0 + sc_ref[...]) + sh_ref[...]).astype(o_ref.dtype)


def _norm_mod(X, row_off, n_rows, g, mods3, tok, l, which_shift, tok_off, Xb=None):
    D = X.shape[1]
    tm = _tile(n_rows, 256, 16, tok.N, row_off) if row_off else _tile(n_rows, 256, 16, tok.N)
    ro = row_off // tm
    if Xb is None:
        x_specs, x_args, na = [pl.BlockSpec((tm, D), lambda i: (i + ro, 0))], [X], 0
    else:
        assert row_off == 0 and X.shape[0] % tm == 0
        x_specs, x_args, na = _two_source_specs(X, Xb, (tm, D), tm, lambda: 0)
    return _call(
        functools.partial(_norm_mod_kernel, n_a=na), (n_rows // tm,),
        x_specs + [pl.BlockSpec((1, D), lambda i: (0, 0)),
                   _mod_spec(tok, l, which_shift, tm, tok_off, D),
                   _mod_spec(tok, l, which_shift + 1, tm, tok_off, D)],
        pl.BlockSpec((tm, D), lambda i: (i, 0)),
        jax.ShapeDtypeStruct((n_rows, D), BF16), "norm_mod")(*x_args, g.reshape(1, D), mods3, mods3)


def _norm_router_kernel(x_ref, g_ref, sh_ref, sc_ref, wr_ref, br_ref,
                        h_ref, idx_ref, wt_ref, rank_ref, cnt_ref, carry_ref):
    @pl.when(pl.program_id(0) == 0)
    def _():
        carry_ref[...] = jnp.zeros_like(carry_ref)

    x = x_ref[...]
    r = lax.rsqrt(jnp.mean(x * x, axis=-1, keepdims=True) + EPS)
    h = (x * r * g_ref[...]) * (1.0 + sc_ref[...]) + sh_ref[...]
    h_ref[...] = h.astype(h_ref.dtype)
    n_e = br_ref.shape[1]
    h_hi = h.astype(BF16)
    h_lo = (h - h_hi.astype(F32)).astype(BF16)
    w_both = wr_ref[...]
    first = jnp.dot(h_hi, w_both, preferred_element_type=F32)
    logits = first[:, :n_e] + first[:, n_e:] + jnp.dot(h_lo, w_both[:, :n_e], preferred_element_type=F32)
    s = jax.nn.sigmoid(logits)
    sel = s + br_ref[...]
    tm, n_e = sel.shape
    lane = lax.broadcasted_iota(jnp.int32, sel.shape, 1).astype(F32)
    picked = jnp.zeros(sel.shape, F32)
    firsts = []
    for _ in range(TOP_K):
        cur = jnp.where(picked > 0.0, -jnp.inf, sel)
        m = jnp.max(cur, axis=-1, keepdims=True)
        first = jnp.min(jnp.where(cur == m, lane, float(n_e)), axis=-1, keepdims=True)
        picked = jnp.where(lane == first, 1.0, picked)
        firsts.append(first)
    w = picked * s
    gate = w / jnp.sum(w, axis=-1, keepdims=True) * ROUTE_SCALE

    earlier = (lax.broadcasted_iota(jnp.int32, (tm, tm), 1) < lax.broadcasted_iota(jnp.int32, (tm, tm), 0))
    cum = jnp.dot(earlier.astype(BF16), picked.astype(BF16), preferred_element_type=F32) + carry_ref[...]
    idx_m = jnp.zeros(sel.shape, F32)
    wt_m = jnp.zeros(sel.shape, F32)
    rank_m = jnp.zeros(sel.shape, F32)
    for k, first in enumerate(firsts):
        mine = lane == first
        idx_m = jnp.where(lane == float(k), first, idx_m)
        wt_m = jnp.where(lane == float(k), jnp.sum(jnp.where(mine, gate, 0.0), axis=-1, keepdims=True), wt_m)
        rank_m = jnp.where(lane == float(k), jnp.sum(jnp.where(mine, cum, 0.0), axis=-1, keepdims=True), rank_m)
    idx_ref[...] = idx_m.astype(jnp.int32)
    wt_ref[...] = wt_m
    rank_ref[...] = rank_m.astype(jnp.int32)
    carry_ref[...] += jnp.sum(picked, axis=0, keepdims=True)
    cnt_ref[...] = carry_ref[...].astype(jnp.int32)


def _norm_router(X, n_rows, g, mods3, tok, l, w_router, b_router):
    D = X.shape[1]
    E = w_router.shape[1]
    w_hi = w_router.astype(BF16)
    w_both = jnp.concatenate([w_hi, (w_router - w_hi.astype(F32)).astype(BF16)], axis=1)
    tm = _tile(n_rows, 256, 16, tok.N)
    row = lambda i: (i, 0)
    fixed = lambda i: (0, 0)
    return _call(
        _norm_router_kernel, (n_rows // tm,),
        [pl.BlockSpec((tm, D), row), pl.BlockSpec((1, D), fixed),
         _mod_spec(tok, l, 3, tm, 0, D), _mod_spec(tok, l, 4, tm, 0, D),
         pl.BlockSpec((D, 2 * E), fixed), pl.BlockSpec((1, E), fixed)],
        [pl.BlockSpec((tm, D), row), pl.BlockSpec((tm, E), row), pl.BlockSpec((tm, E), row),
         pl.BlockSpec((tm, E), row), pl.BlockSpec((1, E), fixed)],
        [jax.ShapeDtypeStruct((n_rows, D), BF16), jax.ShapeDtypeStruct((n_rows, E), jnp.int32),
         jax.ShapeDtypeStruct((n_rows, E), F32), jax.ShapeDtypeStruct((n_rows, E), jnp.int32),
         jax.ShapeDtypeStruct((1, E), jnp.int32)],
        "norm_router", scratch=[pltpu.VMEM((1, E), F32)])(
            X, g.reshape(1, D), mods3, mods3, w_both, b_router.reshape(1, E))


def _mm_kernel(a_ref, b_ref, o_ref):
    o_ref[...] = jnp.dot(a_ref[...], b_ref[...], preferred_element_type=F32).astype(o_ref.dtype)


def _mm(a, b, tm_pref=1024, tn_pref=1024, name="mm", layer=None):
    M, K = a.shape
    N = b.shape[-1]
    tm = _tile(M, tm_pref, 16)
    tn = _tile(N, tn_pref, LANE)
    b_spec = (pl.BlockSpec((K, tn), lambda i, j: (0, j)) if layer is None
              else _layer_spec(layer, (K, tn), lambda i, j: (0, j)))
    return _call(
        _mm_kernel, (M // tm, N // tn),
        [pl.BlockSpec((tm, K), lambda i, j: (i, 0)), b_spec],
        pl.BlockSpec((tm, tn), lambda i, j: (i, j)),
        jax.ShapeDtypeStruct((M, N), BF16), name)(a, b)


def _norm_mm_kernel(a_ref, g_ref, b_ref, o_ref):
    a = a_ref[...].astype(F32)
    r = lax.rsqrt(jnp.mean(a * a, axis=-1, keepdims=True) + EPS)
    an = (a * r * g_ref[...]).astype(BF16)
    o_ref[...] = jnp.dot(an, b_ref[...], preferred_element_type=F32).astype(o_ref.dtype)


def _norm_mm(a, col_off, K, g, b, name):
    M = a.shape[0]
    N = b.shape[1]
    tm = _tile(M, 512, 16)
    cb = col_off // K
    assert col_off % K == 0
    return _call(
        _norm_mm_kernel, (M // tm,),
        [pl.BlockSpec((tm, K), lambda i: (i, cb)),
         pl.BlockSpec((1, K), lambda i: (0, 0)),
         pl.BlockSpec((K, N), lambda i: (0, 0))],
        pl.BlockSpec((tm, N), lambda i: (i, 0)),
        jax.ShapeDtypeStruct((M, N), BF16), name)(a, g.reshape(1, K), b)


def _prep_head(xs, g_ref, rope, rope_chunk, inv_d, shift):
    ones = jnp.ones((len(xs) * LANE, LANE), BF16)
    ssq = jnp.dot(jnp.concatenate([(x * x).astype(BF16) for x in xs], axis=1), ones, preferred_element_type=F32)
    r = lax.rsqrt(ssq * inv_d + EPS)
    out = []
    for c, x in enumerate(xs):
        y = x * r * g_ref[:, c * LANE:(c + 1) * LANE]
        if rope is not None and c == rope_chunk:
            cos, sin = rope
            src = lax.broadcasted_iota(jnp.int32, (LANE, LANE), 0)
            dst = lax.broadcasted_iota(jnp.int32, (LANE, LANE), 1)
            partner = jnp.where((dst % (2 * shift)) < shift, dst + shift, dst - shift)
            swap = (src == partner).astype(BF16)
            y = y * cos + jnp.dot(y.astype(BF16), swap, preferred_element_type=F32) * sin
        out.append(y.astype(BF16))
    return out


def _rope_tables(n_pos, rope_dim):
    nf = rope_dim // 4
    t = jnp.arange(n_pos, dtype=jnp.int32)
    row = (t // GRID_W).astype(F32)
    col = (t % GRID_W).astype(F32)
    inv = ROPE_BASE ** (-jnp.arange(nf, dtype=F32) / nf)
    a_row, a_col = row[:, None] * inv, col[:, None] * inv
    cos = jnp.concatenate([jnp.cos(a_row)] * 2 + [jnp.cos(a_col)] * 2, axis=-1)
    sin = jnp.concatenate([-jnp.sin(a_row), jnp.sin(a_row), -jnp.sin(a_col), jnp.sin(a_col)], axis=-1)
    cos = jnp.pad(cos, ((0, 0), (0, LANE - rope_dim)), constant_values=1.0)
    sin = jnp.pad(sin, ((0, 0), (0, LANE - rope_dim)))
    return cos, sin


def _attn_kernel(*refs, n_chunks, src_cfg, q_rope, keys_once, rope_chunk, inv_d, shift, tq, n_q):
    it = iter(refs)
    q_ref = next(it)
    srcs = [([next(it) for _ in range(n_chunks)], next(it)) for _ in src_cfg]
    gq_ref, gk_ref, cos_ref, sin_ref, o_ref, k_scr = (next(it) for _ in range(6))

    def prep(chunk_refs, rows, g_ref, rope):
        xs = [r[rows, :].astype(F32) for r in chunk_refs]
        tab = (cos_ref[rows, :], sin_ref[rows, :]) if rope else None
        return _prep_head(xs, g_ref, tab, rope_chunk, inv_d, shift)

    def prep_keys():
        off = 0
        for (k_chunks, _), (n_k, rope) in zip(srcs, src_cfg):
            bk = min(n_k, 512)
            for r0 in range(0, n_k, bk):
                for c, y in enumerate(prep(k_chunks, pl.ds(r0, bk), gk_ref, rope)):
                    k_scr[pl.ds(off + r0, bk), c * LANE:(c + 1) * LANE] = y
            off += n_k

    if keys_once:
        pl.when(pl.program_id(2) == 0)(prep_keys)
    else:
        prep_keys()

    def scores(c):
        q = jnp.concatenate(prep([q_ref.at[:, j * LANE:(j + 1) * LANE] for j in range(n_chunks)],
                                 pl.ds(c * tq, tq), gq_ref, q_rope), axis=1)
        return lax.dot_general(q, k_scr[...], (((1,), (1,)), ((), ())), preferred_element_type=F32)

    n_blocks = n_q // tq
    ahead = 1
    pending = [scores(c) for c in range(min(ahead, n_blocks))]
    for c in range(n_blocks):
        rows = pl.ds(c * tq, tq)
        s = pending.pop(0)
        if c + ahead < n_blocks:
            pending.append(scores(c + ahead))
        p = jnp.exp(s - jnp.max(s, axis=-1, keepdims=True))
        l = jnp.sum(p, axis=-1, keepdims=True)
        pb = p.astype(BF16)
        o, off = None, 0
        for (_, v_ref), (n_k, _) in zip(srcs, src_cfg):
            o_src = jnp.dot(pb[:, off:off + n_k], v_ref[...], preferred_element_type=F32)
            o = o_src if o is None else o + o_src
            off += n_k
        o_ref[rows, :] = (o * (1.0 / l)).astype(o_ref.dtype)


def _attend(q_arr, q_row_off, q_cb0, n_q, q_rope, srcs, g_q, g_k, cos, sin, rope_chunk, norm_dim, shift,
            B, n_kv_heads, group, dv, name):
    n_chunks = len(srcs[0][0])
    qw = n_chunks * LANE
    tq = _tile(n_q, 512, 16)
    assert q_row_off % n_q == 0
    qb = q_row_off // n_q
    in_specs = [pl.BlockSpec((n_q, qw), lambda b, hk, g: (qb + b, q_cb0 + hk * group + g))]
    args = [q_arr]
    src_cfg = []
    for k_chunks, (v_arr, v_cb0), row_off, n_k, rope in srcs:
        assert row_off % n_k == 0 and n_k % min(n_k, 512) == 0
        rb = row_off // n_k
        for arr, cb_fn in k_chunks:
            in_specs.append(pl.BlockSpec((n_k, LANE), (lambda b, hk, g, rb=rb, f=cb_fn: (rb + b, f(hk)))))
            args.append(arr)
        in_specs.append(pl.BlockSpec((n_k, dv), (lambda b, hk, g, rb=rb, c0=v_cb0: (rb + b, c0 + hk))))
        args.append(v_arr)
        src_cfg.append((n_k, rope))
    fixed = lambda b, hk, g: (0, 0)
    in_specs += [pl.BlockSpec((1, qw), fixed), pl.BlockSpec((1, qw), fixed),
                 pl.BlockSpec(cos.shape, fixed), pl.BlockSpec(sin.shape, fixed)]
    args += [g_q.reshape(1, qw), g_k.reshape(1, qw), cos, sin]
    assert cos.shape[0] >= max([n_q] + [n_k for n_k, rope in src_cfg if rope])
    n_all = sum(n_k for n_k, _ in src_cfg)
    kern = functools.partial(_attn_kernel, n_chunks=n_chunks, src_cfg=tuple(src_cfg), q_rope=q_rope,
                             keys_once=group > 1, rope_chunk=rope_chunk, inv_d=1.0 / norm_dim, shift=shift,
                             tq=tq, n_q=n_q)
    return _call(
        kern, (B, n_kv_heads, group), in_specs,
        pl.BlockSpec((n_q, dv), lambda b, hk, g: (b, hk * group + g)),
        jax.ShapeDtypeStruct((B * n_q, n_kv_heads * group * dv), BF16), name,
        scratch=[pltpu.VMEM((n_all, qw), BF16)])(*args)


def _conv_kernel(b_ref, c_ref, u_ref, w_ref, o_ref):
    v = c_ref[...].astype(F32) * u_ref[...].astype(F32)
    n = v.shape[0]
    row = lax.broadcasted_iota(jnp.int32, v.shape, 0)
    prev = jnp.where(row == 0, 0.0, pltpu.roll(v, 1, 0))
    nxt = jnp.where(row == n - 1, 0.0, pltpu.roll(v, n - 1, 0))
    y = prev * w_ref[0:1, :] + v * w_ref[1:2, :] + nxt * w_ref[2:3, :]
    o_ref[...] = (b_ref[...].astype(F32) * y).astype(o_ref.dtype)


def _conv(px, row_off, n_seq, seq_len, col_b, col_c, col_u, CW, conv_w):
    tc = _tile(CW, 512, LANE, col_b, col_c, col_u)
    rb = row_off // seq_len
    assert row_off % seq_len == 0

    def spec(col):
        return pl.BlockSpec((seq_len, tc), lambda s, j, col=col: (rb + s, col // tc + j))

    return _call(
        _conv_kernel, (n_seq, CW // tc),
        [spec(col_b), spec(col_c), spec(col_u), pl.BlockSpec((3, tc), lambda s, j: (0, j))],
        pl.BlockSpec((seq_len, tc), lambda s, j: (s, j)),
        jax.ShapeDtypeStruct((n_seq * seq_len, CW), BF16), "short_conv")(px, px, px, conv_w)


def _merge_kernel(*refs, n_a, n_src):
    o_refs, (wa_ref, wb_ref, wc_ref, ga_ref, gb_ref, gc_ref, o_ref) = refs[:3 * n_src], refs[3 * n_src:]

    def branch(k):
        o = o_refs[k * n_src][...]
        if n_src == 2:
            o = jnp.where(pl.program_id(0) < n_a, o, o_refs[k * n_src + 1][...])
        return o

    def term(o, w, g):
        return jax.nn.sigmoid(g[...].astype(F32)) * jnp.dot(o, w[...], preferred_element_type=F32)
    h = term(branch(0), wa_ref, ga_ref) + term(branch(1), wb_ref, gb_ref) + term(branch(2), wc_ref, gc_ref)
    o_ref[...] = h.astype(o_ref.dtype)


def _merge(oa, ob, oc, w_pa, w_pb, w_pc, l, px, gate_col, n_rows):
    D = w_pa.shape[2]
    tm = _tile(n_rows, 1024, 16, oa[0].shape[0])
    tn = _tile(D, 512, LANE, gate_col)
    gb0 = gate_col // tn
    nb = D // tn

    n_src = 1 if oa[1] is None else 2
    a_specs, a_args, na = [], [], 0
    for lat, ctx in (oa, ob, oc):
        if ctx is None:
            a_specs.append(pl.BlockSpec((tm, lat.shape[1]), lambda i, j: (i, 0)))
            a_args.append(lat)
        else:
            assert lat.shape[0] % tm == 0
            specs, args, na = _two_source_specs(lat, ctx, (tm, lat.shape[1]), tm, lambda j: 0)
            a_specs += specs
            a_args += args

    def w_spec(w):
        return _layer_spec(l, (w.shape[1], tn), lambda i, j: (0, j))

    def g_spec(k):
        return pl.BlockSpec((tm, tn), lambda i, j, k=k: (i, gb0 + k * nb + j))

    return _call(
        functools.partial(_merge_kernel, n_a=na, n_src=n_src), (n_rows // tm, nb),
        a_specs + [w_spec(w_pa), w_spec(w_pb), w_spec(w_pc), g_spec(0), g_spec(1), g_spec(2)],
        pl.BlockSpec((tm, tn), lambda i, j: (i, j)),
        jax.ShapeDtypeStruct((n_rows, D), BF16), "merge")(*a_args, w_pa, w_pb, w_pc, px, px, px)


def _mm_res_kernel(a_ref, b_ref, *refs, n_a):
    x_refs, (gt_ref, o_ref) = refs[:-2], refs[-2:]
    x = x_refs[0][...]
    if len(x_refs) == 2:
        x = jnp.where(pl.program_id(0) < n_a, x, x_refs[1][...])
    acc = jnp.dot(a_ref[...], b_ref[...], preferred_element_type=F32)
    o_ref[...] = x + gt_ref[...] * acc


def _mm_res(a, b, X, mods3, tok, l, which, n_rows, Xb=None):
    K = a.shape[1]
    D = b.shape[2]
    tm = _tile(n_rows, 1024, 16, tok.N)
    tn = _tile(D, 512, LANE)

    def gt_idx(i, j):
        row = tok.mod_row(i * tm)
        return ((l * MOD_ROWS + row) * N_MOD + which, 0, j)

    if Xb is None:
        x_specs, x_args, na = [pl.BlockSpec((tm, tn), lambda i, j: (i, j))], [X], 0
    else:
        assert X.shape[0] % tm == 0
        x_specs, x_args, na = _two_source_specs(X, Xb, (tm, tn), tm, lambda j: j)
    return _call(
        functools.partial(_mm_res_kernel, n_a=na), (n_rows // tm, D // tn),
        [pl.BlockSpec((tm, K), lambda i, j: (i, 0)), _layer_spec(l, (K, tn), lambda i, j: (0, j))]
        + x_specs + [pl.BlockSpec((None, 1, tn), gt_idx)],
        pl.BlockSpec((tm, tn), lambda i, j: (i, j)),
        jax.ShapeDtypeStruct((n_rows, D), F32), "proj_residual")(a, b, *x_args, mods3)


def _swiglu_up_kernel(h_ref, w_ref, o_ref):
    gu = jnp.dot(h_ref[...], w_ref[...], preferred_element_type=F32)
    f = o_ref.shape[1]
    o_ref[...] = (_silu(gu[:, :f]) * gu[:, f:]).astype(o_ref.dtype)


def _swiglu_up(h, w_gu, l):
    M, D = h.shape
    f = w_gu.shape[2] // 2
    tm = _tile(M, 1024, 16)
    return _call(
        _swiglu_up_kernel, (M // tm,),
        [pl.BlockSpec((tm, D), lambda i: (i, 0)), _layer_spec(l, (D, 2 * f), lambda i: (0, 0))],
        pl.BlockSpec((tm, f), lambda i: (i, 0)),
        jax.ShapeDtypeStruct((M, f), BF16), "shared_up")(h, w_gu)


SLAB_ROWS = 16
STAGE_PITCH = 24
EXPERT_TILE = 256
HI_MASK = -65536


def _pack_pairs(v):
    half = v.shape[1] // 2
    bits = lax.bitcast_convert_type(v.astype(BF16).astype(F32), jnp.int32)
    return lax.shift_right_logical(bits[:, :half], 16) | (bits[:, half:] & HI_MASK)


def _unpack_lo(w):
    return lax.bitcast_convert_type(lax.shift_left(w, 16), F32)


def _unpack_hi(w):
    return lax.bitcast_convert_type(w & HI_MASK, F32)


def _dispatch_kernel(pos_ref, zt_ref, h_ref, xs_hbm, slab_ref, zero_ref, sem, *, tm, n_experts):
    W = slab_ref.shape[1]
    trows = EXPERT_TILE * SLAB_ROWS

    @pl.when(pl.program_id(0) == 0)
    def _():
        zero_ref[...] = jnp.zeros_like(zero_ref)
        def zcopy(t):
            return pltpu.make_async_copy(zero_ref, xs_hbm.at[pl.ds(pl.multiple_of(t * trows, trows), trows), :], sem)
        for e in range(n_experts):
            zcopy(zt_ref[0, e]).start()
        for e in range(n_experts):
            zcopy(zt_ref[0, e]).wait()
        n_used = zt_ref[0, n_experts]
        n_tiles = xs_hbm.shape[0] // trows

        def zstart(t, carry):
            zcopy(t).start()
            return carry

        def zwait(t, carry):
            zcopy(t).wait()
            return carry

        lax.fori_loop(n_used, n_tiles, zstart, 0)
        lax.fori_loop(n_used, n_tiles, zwait, 0)

    words = _pack_pairs(h_ref[...])
    for s in range(SLAB_ROWS):
        slab_ref[pl.ds(s, tm, stride=STAGE_PITCH), :] = words[:, s * W:(s + 1) * W]

    def row_copy(r, k):
        p = pos_ref[0, r * TOP_K + k]
        return pltpu.make_async_copy(
            slab_ref.at[pl.ds(pl.multiple_of(r * STAGE_PITCH, 8), SLAB_ROWS), :],
            xs_hbm.at[pl.ds(pl.multiple_of(p * SLAB_ROWS, SLAB_ROWS), SLAB_ROWS), :], sem)

    def issue(r, carry):
        for k in range(TOP_K):
            row_copy(r, k).start(priority=k % 2)
        return carry

    lax.fori_loop(0, tm, issue, 0)
    for k in range(TOP_K):
        pltpu.make_async_copy(slab_ref.at[pl.ds(0, tm * SLAB_ROWS), :],
                              xs_hbm.at[pl.ds(0, tm * SLAB_ROWS), :], sem).wait()


def _dispatch(h2, pos, zero_tiles, n_slots):
    M, D = h2.shape
    W = D // (2 * SLAB_ROWS)
    assert W == LANE, "strided slab access needs LANE-wide slab rows"
    E = zero_tiles.shape[0] - 1
    tm = _tile(M, 256, 16)
    kern = functools.partial(_dispatch_kernel, tm=tm, n_experts=E)
    return _call(
        kern, (M // tm,),
        [pl.BlockSpec((None, 1, tm * TOP_K), lambda i: (i, 0, 0), memory_space=pltpu.SMEM),
         pl.BlockSpec((1, E + 1), lambda i: (0, 0), memory_space=pltpu.SMEM),
         pl.BlockSpec((tm, D), lambda i: (i, 0))],
        pl.BlockSpec(memory_space=pl.ANY),
        jax.ShapeDtypeStruct((n_slots * SLAB_ROWS, W), jnp.int32), "moe_dispatch",
        scratch=[pltpu.VMEM((tm * STAGE_PITCH, W), jnp.int32),
                 pltpu.VMEM((EXPERT_TILE * SLAB_ROWS, W), jnp.int32),
                 pltpu.SemaphoreType.DMA(())])(
            pos.reshape(M // tm, 1, tm * TOP_K), zero_tiles.reshape(1, E + 1), h2)


def _expert_kernel(te_ref, nu_ref, x_ref, wgu_ref, wdn_ref, o_ref):
    W = x_ref.shape[1]
    f = wdn_ref.shape[0]

    @pl.when(pl.program_id(0) < nu_ref[0])
    def _():
        words = [x_ref[pl.ds(s, EXPERT_TILE, stride=SLAB_ROWS), :] for s in range(SLAB_ROWS)]
        x = jnp.concatenate([_unpack_lo(w).astype(BF16) for w in words]
                            + [_unpack_hi(w).astype(BF16) for w in words], axis=1)
        gu = jnp.dot(x, wgu_ref[...], preferred_element_type=F32)
        act = (_silu(gu[:, :f]) * gu[:, f:]).astype(BF16)
        y = _pack_pairs(jnp.dot(act, wdn_ref[...], preferred_element_type=F32))
        for s in range(SLAB_ROWS):
            o_ref[pl.ds(s, EXPERT_TILE, stride=SLAB_ROWS), :] = y[:, s * W:(s + 1) * W]

    @pl.when(pl.program_id(0) >= nu_ref[0])
    def _():
        o_ref[...] = jnp.zeros_like(o_ref)


def _experts(xs, tile_expert, n_used, w_gu, w_down, l):
    _, E, D, F2 = w_gu.shape
    W = xs.shape[1]
    trows = EXPERT_TILE * SLAB_ROWS
    n_tiles = xs.shape[0] // trows
    grid_spec = pltpu.PrefetchScalarGridSpec(
        num_scalar_prefetch=2, grid=(n_tiles,),
        in_specs=[pl.BlockSpec((trows, W), lambda i, te, nu: (jnp.minimum(i, nu[0] - 1), 0)),
                  pl.BlockSpec((None, None, D, F2), lambda i, te, nu: (l, te[i], 0, 0)),
                  pl.BlockSpec((None, None, F2 // 2, D), lambda i, te, nu: (l, te[i], 0, 0))],
        out_specs=pl.BlockSpec((trows, W), lambda i, te, nu: (i, 0)))
    return pl.pallas_call(
        _expert_kernel, grid_spec=grid_spec, out_shape=jax.ShapeDtypeStruct(xs.shape, jnp.int32),
        name="moe_experts",
        compiler_params=pltpu.CompilerParams(dimension_semantics=("arbitrary",),
                                             vmem_limit_bytes=VMEM_LIMIT_BYTES))(
            tile_expert, n_used, xs, w_gu, w_down)


def _combine_kernel(pos_ref, posn_ref, wt_ref, sh_ref, x_ref, gt_ref, ys_hbm, o_ref, buf_a, buf_b, sem, *, tm):
    j = pl.program_id(0)
    W = buf_a.shape[1]
    half = SLAB_ROWS * W

    def issue(pref, first_tok, buf, s_idx):
        def body(r, carry):
            for k in range(TOP_K):
                p = pref[0, (first_tok + r) * TOP_K + k]
                pltpu.make_async_copy(
                    ys_hbm.at[pl.ds(pl.multiple_of(p * SLAB_ROWS, SLAB_ROWS), SLAB_ROWS), :],
                    buf.at[pl.ds(pl.multiple_of((k * tm + r) * STAGE_PITCH, 8), SLAB_ROWS), :],
                    sem.at[s_idx]).start(priority=k % 2)
            return carry
        lax.fori_loop(0, tm, body, 0)

    def drain(buf, s_idx):
        for k in range(TOP_K):
            pltpu.make_async_copy(ys_hbm.at[pl.ds(0, tm * SLAB_ROWS), :],
                                  buf.at[pl.ds(0, tm * SLAB_ROWS), :], sem.at[s_idx]).wait()

    def reduce(buf, r0):
        rows = pl.ds(r0, tm)
        wt = wt_ref[rows, :]
        wk = [wt[:, k:k + 1] for k in range(TOP_K)]
        for s in range(SLAB_ROWS):
            lo = hi = None
            for k in range(TOP_K):
                w = buf[pl.ds(k * tm * STAGE_PITCH + s, tm, stride=STAGE_PITCH), :]
                lo_k, hi_k = wk[k] * _unpack_lo(w), wk[k] * _unpack_hi(w)
                lo, hi = (lo_k, hi_k) if lo is None else (lo + lo_k, hi + hi_k)
            for c0, routed in ((s * W, lo), (half + s * W, hi)):
                cols = slice(c0, c0 + W)
                o_ref[rows, cols] = x_ref[rows, cols] + gt_ref[:, cols] * (sh_ref[rows, cols].astype(F32) + routed)

    @pl.when(j == 0)
    def _():
        issue(pos_ref, 0, buf_a, 0)

    issue(pos_ref, tm, buf_b, 1)
    drain(buf_a, 0)
    reduce(buf_a, 0)

    @pl.when(j + 1 < pl.num_programs(0))
    def _():
        issue(posn_ref, 0, buf_a, 0)

    drain(buf_b, 1)
    reduce(buf_b, tm)


def _combine(ys, pos, wt, sh, X, mods3, tok, l, which, n_rows):
    D = X.shape[1]
    W = ys.shape[1]
    E = wt.shape[1]
    tm = _tile(n_rows // 2, 128, 16, tok.N // 2)
    tb = 2 * tm
    n_steps = n_rows // tb
    pos3 = pos.reshape(n_steps, 1, tb * TOP_K)

    def gt_idx(j):
        row = tok.mod_row(j * tb)
        return ((l * MOD_ROWS + row) * N_MOD + which, 0, 0)

    kern = functools.partial(_combine_kernel, tm=tm)
    return _call(
        kern, (n_steps,),
        [pl.BlockSpec((None, 1, tb * TOP_K), lambda j: (j, 0, 0), memory_space=pltpu.SMEM),
         pl.BlockSpec((None, 1, tb * TOP_K), lambda j: (jnp.minimum(j + 1, n_steps - 1), 0, 0),
                      memory_space=pltpu.SMEM),
         pl.BlockSpec((tb, E), lambda j: (j, 0)),
         pl.BlockSpec((tb, D), lambda j: (j, 0)),
         pl.BlockSpec((tb, D), lambda j: (j, 0)),
         pl.BlockSpec((None, 1, D), gt_idx),
         pl.BlockSpec(memory_space=pl.ANY)],
        pl.BlockSpec((tb, D), lambda j: (j, 0)),
        jax.ShapeDtypeStruct((n_rows, D), F32), "moe_combine",
        scratch=[pltpu.VMEM((TOP_K * tm * STAGE_PITCH, W), jnp.int32),
                 pltpu.VMEM((TOP_K * tm * STAGE_PITCH, W), jnp.int32),
                 pltpu.SemaphoreType.DMA((2,))])(pos3, pos3, wt, sh, X, mods3, ys)


def _routing_plan(idx, rank, counts, n_tiles):
    E = counts.shape[0]
    padded = (counts + EXPERT_TILE - 1) // EXPERT_TILE * EXPERT_TILE
    ends = jnp.cumsum(padded)
    base = ends - padded
    experts = jnp.arange(E, dtype=jnp.int32)
    pos = jnp.sum(jnp.where(idx[..., None] == experts, base, 0), axis=-1) + rank
    tile_start = jnp.arange(n_tiles, dtype=jnp.int32) * EXPERT_TILE
    tile_expert = jnp.minimum(jnp.sum((ends[None, :] <= tile_start[:, None]).astype(jnp.int32), axis=1), E - 1)
    n_used = (ends[-1] // EXPERT_TILE).reshape(1)
    last_tile = jnp.maximum(ends // EXPERT_TILE - 1, 0)
    return pos.astype(jnp.int32), tile_expert.astype(jnp.int32), n_used.astype(jnp.int32), last_tile.astype(jnp.int32)


def kernel(x, c, ctx, c_ctx, w_mod, b_mod, g_mix, g_ffn, w_in, mla_q_norm_g, mla_kv_norm_g, w_uq, w_ukv,
           mla_q_head_g, mla_k_head_g, gqa_q_g, gqa_k_g, conv_w, w_pa, w_pb, w_pc, w_o, w_router, b_router,
           w_exp_gu, w_exp_down, w_sh_gu, w_sh_down):
    B, N, D = x.shape
    CTX = ctx.shape[1]
    L = w_mod.shape[0]
    QL = mla_q_norm_g.shape[1]
    KVL = mla_kv_norm_g.shape[1]
    HA = w_uq.shape[2] // MLA_QK
    HG = w_pc.shape[1] // GQA_HD
    CW = conv_w.shape[2]
    D_IN = w_in.shape[2]
    HKV = (D_IN - (KVL + MLA_ROPE + QL + HG * GQA_HD + 3 * CW + 3 * D)) // (2 * GQA_HD)
    GKV = HKV * GQA_HD
    E = w_router.shape[2]

    tok = _Tok(B, N, CTX)
    n_lat, n_ctx = tok.n_lat, tok.n_ctx

    KV_COLS = KVL + 2 * GKV + LANE
    col_gk, col_gv, col_kr = KVL, KVL + GKV, KVL + 2 * GKV
    main0 = KVL + MLA_ROPE + 2 * GKV
    col_cq, col_gq = 0, QL
    col_b, col_c, col_u = QL + HG * GQA_HD, QL + HG * GQA_HD + CW, QL + HG * GQA_HD + 2 * CW
    col_gate = col_u + CW

    cvec = jnp.zeros((MOD_ROWS, D), F32).at[:B].set(c).at[B].set(c_ctx)
    mods = _mods(cvec, w_mod, b_mod)
    mods3 = mods.reshape(L * MOD_ROWS * N_MOD, 1, D)

    cos_a, sin_a = _rope_tables(N, MLA_ROPE)
    cos_g, sin_g = _rope_tables(N, GQA_HD)

    w_pa_b, w_pb_b, w_pc_b, w_o_b = (w.astype(BF16) for w in (w_pa, w_pb, w_pc, w_o))
    w_exp_gu_b, w_exp_down_b = w_exp_gu.astype(BF16), w_exp_down.astype(BF16)
    w_sh_gu_b, w_sh_down_b = w_sh_gu.astype(BF16), w_sh_down.astype(BF16)
    w_main_b = w_in[:, :, main0:].astype(BF16)

    X, Xb = x.reshape(n_lat, D), ctx.reshape(n_ctx, D)

    for l in range(L):
        last = l == L - 1
        n_rows = n_lat if last else n_lat + n_ctx

        w_kv = jnp.concatenate(
            [w_in[l, :, :KVL].astype(BF16), w_in[l, :, KVL + MLA_ROPE:main0].astype(BF16),
             w_in[l, :, KVL:KVL + MLA_ROPE].astype(BF16), jnp.zeros((D, LANE - MLA_ROPE), BF16)], axis=1)
        w_uq_p = jnp.pad(w_uq[l].reshape(QL, HA, MLA_QK),
                         ((0, 0), (0, 0), (0, MLA_QK_PAD - MLA_QK))).reshape(QL, HA * MLA_QK_PAD).astype(BF16)
        w_ukv_r = w_ukv[l].reshape(KVL, HA, MLA_NOPE + MLA_V)
        w_ukv_p = jnp.concatenate([w_ukv_r[:, :, :MLA_NOPE].reshape(KVL, HA * MLA_NOPE),
                                   w_ukv_r[:, :, MLA_NOPE:].reshape(KVL, HA * MLA_V)], axis=1).astype(BF16)
        g_qa = jnp.pad(mla_q_head_g[l] * (1.0 / math.sqrt(MLA_QK)), (0, MLA_QK_PAD - MLA_QK))
        g_ka = jnp.pad(mla_k_head_g[l], (0, MLA_QK_PAD - MLA_QK))
        g_qg = gqa_q_g[l] * (1.0 / math.sqrt(GQA_HD))
        g_kg = gqa_k_g[l]

        h = _norm_mod(X, 0, n_rows, g_mix[l], mods3, tok, l, 0, 0, Xb=None if last else Xb)
        px = _mm(h, w_main_b, name="in_proj", layer=l)
        if last:
            c_src, c_off = (X, n_lat) if Xb is None else (Xb, 0)
            hc = _norm_mod(c_src, c_off, n_ctx, g_mix[l], mods3, tok, l, 0, n_lat)
            pkv_x = _mm(h, w_kv, tn_pref=KV_COLS, name="in_proj_kv")
            pkv_c = _mm(hc, w_kv, tn_pref=KV_COLS, name="in_proj_kv")
            kv_parts = [(pkv_x, n_lat, 0), (pkv_c, n_ctx, n_lat)]
        else:
            pkv = _mm(h, w_kv, tn_pref=KV_COLS, name="in_proj_kv")
            kv_parts = [(pkv, n_rows, 0)]

        kvr = [_norm_mm(p, 0, KVL, mla_kv_norm_g[l], w_ukv_p, "mla_kv_up") for p, _, _ in kv_parts]
        pk = [p for p, _, _ in kv_parts]
        q_raw = _norm_mm(px, col_cq, QL, mla_q_norm_g[l], w_uq_p, "mla_q_up")

        ctx_i = len(kv_parts) - 1
        ctx_off = 0 if last else n_lat
        vA0 = HA * MLA_NOPE // MLA_V

        def mla_src(i, row_off, n_k, rope):
            return ([(kvr[i], lambda hk: hk), (pk[i], lambda hk: col_kr // LANE)], (kvr[i], vA0), row_off, n_k, rope)

        def gqa_src(i, row_off, n_k, rope):
            return ([(pk[i], lambda hk: col_gk // LANE + hk)], (pk[i], col_gv // GQA_HD), row_off, n_k, rope)

        mla_args = (g_qa, g_ka, cos_a, sin_a, 1, MLA_QK, MLA_ROPE // 4, B, HA, 1, MLA_V)
        gqa_args = (g_qg, g_kg, cos_g, sin_g, 0, GQA_HD, GQA_HD // 4, B, HKV, HG // HKV, GQA_HD)
        oa = _attend(q_raw, 0, 0, N, True, [mla_src(ctx_i, ctx_off, CTX, False), mla_src(0, 0, N, True)],
                     *mla_args, "mla_attn")
        og = _attend(px, 0, col_gq // GQA_HD, N, True, [gqa_src(ctx_i, ctx_off, CTX, False), gqa_src(0, 0, N, True)],
                     *gqa_args, "gqa_attn")
        ob = _conv(px, 0, B, N, col_b, col_c, col_u, CW, conv_w[l])
        oa_c = og_c = ob_c = None
        if not last:
            oa_c = _attend(q_raw, n_lat, 0, CTX, False, [mla_src(0, n_lat, CTX, False)], *mla_args, "mla_attn_ctx")
            og_c = _attend(px, n_lat, col_gq // GQA_HD, CTX, False, [gqa_src(0, n_lat, CTX, False)],
                           *gqa_args, "gqa_attn_ctx")
            ob_c = _conv(px, n_lat, B, CTX, col_b, col_c, col_u, CW, conv_w[l])

        hm = _merge((oa, oa_c), (ob, ob_c), (og, og_c), w_pa_b, w_pb_b, w_pc_b, l, px, col_gate, n_rows)
        X = _mm_res(hm, w_o_b, X, mods3, tok, l, 2, n_rows, Xb=None if last else Xb)
        Xb = None

        h2, idx_m, wt_m, rank_m, cnt = _norm_router(X, n_rows, g_ffn[l], mods3, tok, l, w_router[l], b_router[l])
        n_tiles = n_rows * TOP_K // EXPERT_TILE + E
        pos, tile_expert, n_used, last_tile = _routing_plan(idx_m[:, :TOP_K], rank_m[:, :TOP_K], cnt[0], n_tiles)
        xs = _dispatch(h2, pos, jnp.concatenate([last_tile, n_used]), n_tiles * EXPERT_TILE)
        ys = _experts(xs, tile_expert, n_used, w_exp_gu_b, w_exp_down_b, l)
        sh = _mm(_swiglu_up(h2, w_sh_gu_b, l), w_sh_down_b, name="shared_down", layer=l)
        X = _combine(ys, pos, wt_m, sh, X, mods3, tok, l, 5, n_rows)

    return X.reshape(B, N, D)
```
